```python
import math
import jax, jax.numpy as jnp
from jax import lax
import numpy as np

D_MODEL = 1024
BATCH = 2
SEQ = 16384
DEPTH = 1
DEC_BATCH = 8
DEC_SEQ = 64
PAST_LEN = 2048

CHUNK = 64
LEFT_CHUNKS = 8
BAND_CHUNKS = LEFT_CHUNKS + 1
BAND = BAND_CHUNKS * CHUNK
ATT_REACH = LEFT_CHUNKS * CHUNK
MIX_WIDTH = D_MODEL
ATT_WIDTH = MIX_WIDTH // 2
MLSTM_WIDTH = MIX_WIDTH - ATT_WIDTH
ATT_HEADS = 8
ATT_HEAD_DIM = ATT_WIDTH // ATT_HEADS
MLSTM_HEADS = 4
MLSTM_HEAD_DIM = MLSTM_WIDTH // MLSTM_HEADS
MAX_REL = 128
D_FF = 4 * D_MODEL
DEEPNORM_ALPHA = (2 * DEPTH) ** 0.25
DEEPNORM_BETA = (8 * DEPTH) ** -0.25
LN_EPS = 1e-5
NEG_INF = -1e30
ATT_SCALE = ATT_HEAD_DIM ** -0.5

OFF_QA = 0
OFF_KA = OFF_QA + ATT_WIDTH
OFF_VA = OFF_KA + ATT_WIDTH
OFF_QB = OFF_VA + ATT_WIDTH
OFF_KB = OFF_QB + MLSTM_WIDTH
OFF_VB = OFF_KB + MLSTM_WIDTH
OFF_OB = OFF_VB + MLSTM_WIDTH
OFF_IG = OFF_OB + MLSTM_WIDTH
OFF_FG = OFF_IG + MLSTM_HEADS
IN_WIDTH = OFF_FG + MLSTM_HEADS

kernel_name = "hybrid_band_attn_mlstm_stream_step"


def layer_norm(x, g, b):
    xf = x.astype(jnp.float32)
    mu = xf.mean(-1, keepdims=True)
    var = jnp.square(xf - mu).mean(-1, keepdims=True)
    return ((xf - mu) * lax.rsqrt(var + LN_EPS) * g + b).astype(x.dtype)


def head_norm(h, g):
    mu = h.mean(-1, keepdims=True)
    var = jnp.square(h - mu).mean(-1, keepdims=True)
    hn = (h - mu) * lax.rsqrt(var + LN_EPS)
    B, T = h.shape[0], h.shape[1]
    return hn.reshape(B, T, MLSTM_WIDTH) * g.astype(jnp.float32)


def rel_bias_lookup(rel_bias, rel):
    return rel_bias[:, jnp.clip(rel, -MAX_REL, MAX_REL) + MAX_REL].astype(jnp.float32)


def in_projection(x, w_in, b_in):
    u = jnp.einsum('btd,de->bte', x, w_in) + b_in
    B, T = u.shape[0], u.shape[1]

    def heads(off, width, nh):
        return u[..., off:off + width].reshape(B, T, nh, width // nh)

    qa = heads(OFF_QA, ATT_WIDTH, ATT_HEADS)
    ka = heads(OFF_KA, ATT_WIDTH, ATT_HEADS)
    va = heads(OFF_VA, ATT_WIDTH, ATT_HEADS)
    qb = heads(OFF_QB, MLSTM_WIDTH, MLSTM_HEADS).astype(jnp.float32)
    kb = heads(OFF_KB, MLSTM_WIDTH, MLSTM_HEADS).astype(jnp.float32) * (MLSTM_HEAD_DIM ** -0.5)
    vb = heads(OFF_VB, MLSTM_WIDTH, MLSTM_HEADS).astype(jnp.float32)
    og = jax.nn.sigmoid(u[..., OFF_OB:OFF_OB + MLSTM_WIDTH])
    ig = u[..., OFF_IG:OFF_IG + MLSTM_HEADS].astype(jnp.float32)
    lf = jax.nn.log_sigmoid(u[..., OFF_FG:OFF_FG + MLSTM_HEADS].astype(jnp.float32))
    return qa, ka, va, qb, kb, vb, og, ig, lf


def band_attention_prompt(q, k, v, rel_bias):
    B, S, H, Dh = q.shape
    nc = S // CHUNK
    pad = LEFT_CHUNKS * CHUNK
    qc = q.reshape(B, nc, CHUNK, H, Dh)
    kp = jnp.pad(k, ((0, 0), (pad, 0), (0, 0), (0, 0))).reshape(B, nc + LEFT_CHUNKS, CHUNK, H, Dh)
    vp = jnp.pad(v, ((0, 0), (pad, 0), (0, 0), (0, 0))).reshape(B, nc + LEFT_CHUNKS, CHUNK, H, Dh)
    s = jnp.concatenate(
        [jnp.einsum('bcqhd,bckhd->bchqk', qc, kp[:, j:j + nc]) for j in range(BAND_CHUNKS)],
        axis=-1).astype(jnp.float32) * ATT_SCALE
    qi = jnp.arange(CHUNK)
    kj = jnp.arange(BAND)
    rel = pad + qi[:, None] - kj[None, :]
    s = s + rel_bias_lookup(rel_bias, rel)[None, None]
    kpos = (jnp.arange(nc)[:, None] - LEFT_CHUNKS) * CHUNK + kj[None, :]
    s = jnp.where((kpos >= 0)[None, :, None, None, :], s, NEG_INF)
    p = jax.nn.softmax(s, axis=-1).astype(v.dtype).reshape(B, nc, H, CHUNK, BAND_CHUNKS, CHUNK)
    o = jnp.einsum('bchqk,bckhd->bcqhd', p[..., 0, :], vp[:, 0:nc])
    for j in range(1, BAND_CHUNKS):
        o = o + jnp.einsum('bchqk,bckhd->bcqhd', p[..., j, :], vp[:, j:j + nc])
    return o.reshape(B, S, H * Dh)


def band_attention_sample(q, k_new, v_new, cache_k, cache_v, rel_bias):
    DB, T, H, Dh = q.shape
    R = cache_k.shape[1]
    kk = jnp.concatenate([cache_k.astype(k_new.dtype), k_new], axis=1)
    vv = jnp.concatenate([cache_v.astype(v_new.dtype), v_new], axis=1)
    s = jnp.einsum('bqhd,bkhd->bhqk', q, kk).astype(jnp.float32) * ATT_SCALE
    rel = jnp.arange(T)[:, None] + R - jnp.arange(R + T)[None, :]
    s = s + rel_bias_lookup(rel_bias, rel)[None]
    p = jax.nn.softmax(s, axis=-1).astype(vv.dtype)
    o = jnp.einsum('bhqk,bkhd->bqhd', p, vv)
    return o.reshape(DB, T, H * Dh)


def mlstm_block(C, n, m, q, k, v, ig, lf):
    L = q.shape[2]
    b = jnp.cumsum(lf, axis=-1)
    causal = jnp.tril(jnp.ones((L, L), dtype=bool))
    logw = jnp.where(causal, b[..., :, None] - b[..., None, :] + ig[..., None, :], -jnp.inf)
    inter = b + m[..., None]
    m_t = jnp.maximum(inter, logw.max(-1))
    w = jnp.exp(logw - m_t[..., None]) * jnp.einsum('bhtd,bhsd->bhts', q, k)
    g = jnp.exp(inter - m_t)
    num = g[..., None] * jnp.einsum('bhvk,bhtk->bhtv', C, q) + jnp.einsum('bhts,bhsv->bhtv', w, v)
    den = g * jnp.einsum('bhk,bhtk->bht', n, q) + w.sum(-1)
    h = num / jnp.maximum(jnp.abs(den), jnp.exp(-m_t))[..., None]
    b_last = b[..., -1]
    logw_end = b_last[..., None] - b + ig
    m_new = jnp.maximum(b_last + m, logw_end.max(-1))
    decay = jnp.exp(b_last + m - m_new)
    w_end = jnp.exp(logw_end - m_new[..., None])
    C_new = decay[..., None, None] * C + jnp.einsum('bhs,bhsv,bhsk->bhvk', w_end, v, k)
    n_new = decay[..., None] * n + jnp.einsum('bhs,bhsk->bhk', w_end, k)
    return C_new, n_new, m_new, h


def mlstm_prompt(q, k, v, ig, lf):
    B, S, H, d = q.shape
    nc = S // CHUNK

    def blocks(t):
        return t.reshape(B, nc, CHUNK, H, d).transpose(1, 0, 3, 2, 4)

    def gblocks(t):
        return t.reshape(B, nc, CHUNK, H).transpose(1, 0, 3, 2)

    init = (jnp.zeros((B, H, d, d), jnp.float32), jnp.zeros((B, H, d), jnp.float32),
            jnp.zeros((B, H), jnp.float32))

    def step(carry, xs):
        C, n, m = carry
        C, n, m, h = mlstm_block(C, n, m, *xs)
        return (C, n, m), h

    (C, n, m), hs = lax.scan(step, init, (blocks(q), blocks(k), blocks(v), gblocks(ig), gblocks(lf)))
    h = hs.transpose(1, 0, 3, 2, 4).reshape(B, S, H, d)
    return h, C, n, m


def mlstm_sample(q, k, v, ig, lf, C, n, m):
    C, n, m, h = mlstm_block(C.astype(jnp.float32), n.astype(jnp.float32), m.astype(jnp.float32),
                             jnp.swapaxes(q, 1, 2), jnp.swapaxes(k, 1, 2), jnp.swapaxes(v, 1, 2),
                             jnp.swapaxes(ig, 1, 2), jnp.swapaxes(lf, 1, 2))
    return jnp.swapaxes(h, 1, 2), C, n, m


def merge_and_ffn(x, att_o, mlstm_h, og, norm_g, w_out, ln1_g, ln1_b, w_ff1, b_ff1, w_ff2, b_ff2, ln2_g, ln2_b):
    mb = (og.astype(jnp.float32) * head_norm(mlstm_h, norm_g)).astype(x.dtype)
    mix = jnp.einsum('bte,ed->btd', jnp.concatenate([att_o.astype(x.dtype), mb], axis=-1), w_out)
    h = layer_norm(DEEPNORM_ALPHA * x + mix, ln1_g, ln1_b)
    f = jnp.einsum('btf,fd->btd', jnp.square(jax.nn.relu(jnp.einsum('btd,df->btf', h, w_ff1) + b_ff1)), w_ff2) + b_ff2
    return layer_norm(DEEPNORM_ALPHA * h + f, ln2_g, ln2_b)


def setup_inputs(seed: int = 0) -> dict:
    key = jax.random.key(seed)
    ks = jax.random.split(key, 24)
    f32 = jnp.float32
    rows = min(ATT_REACH, PAST_LEN)
    beta = DEEPNORM_BETA
    col_scale = (jnp.ones((IN_WIDTH,), f32)
                 .at[OFF_VA:OFF_VA + ATT_WIDTH].set(beta)
                 .at[OFF_VB:OFF_VB + MLSTM_WIDTH].set(beta))
    w_in = jax.random.normal(ks[7], (DEPTH, D_MODEL, IN_WIDTH), f32) * (D_MODEL ** -0.5) * col_scale
    b_in = (jax.random.normal(ks[8], (DEPTH, IN_WIDTH), f32) * 0.01).at[:, OFF_FG:OFF_FG + MLSTM_HEADS].add(
        jnp.linspace(3.0, 6.0, MLSTM_HEADS, dtype=f32))
    return {
        "x_prompt": jax.random.normal(ks[0], (BATCH, SEQ, D_MODEL), f32),
        "x_sample": jax.random.normal(ks[1], (DEC_BATCH, DEC_SEQ, D_MODEL), f32),
        "cache_k": jax.random.normal(ks[2], (DEPTH, DEC_BATCH, rows, ATT_HEADS, ATT_HEAD_DIM), f32),
        "cache_v": jax.random.normal(ks[3], (DEPTH, DEC_BATCH, rows, ATT_HEADS, ATT_HEAD_DIM), f32) * beta,
        "state_C": jax.random.normal(ks[4], (DEPTH, DEC_BATCH, MLSTM_HEADS, MLSTM_HEAD_DIM, MLSTM_HEAD_DIM), f32) * 0.1,
        "state_n": jax.random.normal(ks[5], (DEPTH, DEC_BATCH, MLSTM_HEADS, MLSTM_HEAD_DIM), f32) * 0.1,
        "state_m": jax.random.normal(ks[6], (DEPTH, DEC_BATCH, MLSTM_HEADS), f32) * 0.5,
        "w_in": w_in,
        "b_in": b_in,
        "rel_bias": jax.random.normal(ks[9], (DEPTH, ATT_HEADS, 2 * MAX_REL + 1), f32) * 0.1,
        "mlstm_norm_g": 1.0 + 0.02 * jax.random.normal(ks[10], (DEPTH, MLSTM_WIDTH), f32),
        "w_out": jax.random.normal(ks[11], (DEPTH, MIX_WIDTH, D_MODEL), f32) * (MIX_WIDTH ** -0.5) * beta,
        "ln1_g": 1.0 + 0.02 * jax.random.normal(ks[12], (DEPTH, D_MODEL), f32),
        "ln1_b": 0.02 * jax.random.normal(ks[13], (DEPTH, D_MODEL), f32),
        "w_ff1": jax.random.normal(ks[14], (DEPTH, D_MODEL, D_FF), f32) * (D_MODEL ** -0.5) * beta,
        "b_ff1": 0.01 * jax.random.normal(ks[15], (DEPTH, D_FF), f32),
        "w_ff2": jax.random.normal(ks[16], (DEPTH, D_FF, D_MODEL), f32) * (D_FF ** -0.5) * beta,
        "b_ff2": 0.01 * jax.random.normal(ks[17], (DEPTH, D_MODEL), f32),
        "ln2_g": 1.0 + 0.02 * jax.random.normal(ks[18], (DEPTH, D_MODEL), f32),
        "ln2_b": 0.02 * jax.random.normal(ks[19], (DEPTH, D_MODEL), f32),
    }


def reference(x_prompt, x_sample, cache_k, cache_v, state_C, state_n, state_m, w_in, b_in, rel_bias,
              mlstm_norm_g, w_out, ln1_g, ln1_b, w_ff1, b_ff1, w_ff2, b_ff2, ln2_g, ln2_b):
    sd = state_C.dtype
    yp, ys = x_prompt, x_sample
    kp_l, vp_l, Cp_l, np_l, mp_l = [], [], [], [], []
    ks_l, vs_l, Cs_l, ns_l, ms_l = [], [], [], [], []
    for l in range(DEPTH):
        qa, ka, va, qb, kb, vb, og, ig, lf = in_projection(yp, w_in[l], b_in[l])
        att = band_attention_prompt(qa, ka, va, rel_bias[l])
        hb, Cp, npr, mp = mlstm_prompt(qb, kb, vb, ig, lf)
        rows = min(ATT_REACH, yp.shape[1])
        kp_l.append(ka[:, yp.shape[1] - rows:].astype(cache_k.dtype))
        vp_l.append(va[:, yp.shape[1] - rows:].astype(cache_v.dtype))
        Cp_l.append(Cp.astype(sd))
        np_l.append(npr.astype(sd))
        mp_l.append(mp.astype(sd))
        yp = merge_and_ffn(yp, att, hb, og, mlstm_norm_g[l], w_out[l], ln1_g[l], ln1_b[l],
                           w_ff1[l], b_ff1[l], w_ff2[l], b_ff2[l], ln2_g[l], ln2_b[l])
        qa, ka, va, qb, kb, vb, og, ig, lf = in_projection(ys, w_in[l], b_in[l])
        att = band_attention_sample(qa, ka, va, cache_k[l], cache_v[l], rel_bias[l])
        hb, Cs, ns, ms = mlstm_sample(qb, kb, vb, ig, lf, state_C[l], state_n[l], state_m[l])
        ks_l.append(ka.astype(cache_k.dtype))
        vs_l.append(va.astype(cache_v.dtype))
        Cs_l.append(Cs.astype(sd))
        ns_l.append(ns.astype(sd))
        ms_l.append(ms.astype(sd))
        ys = merge_and_ffn(ys, att, hb, og, mlstm_norm_g[l], w_out[l], ln1_g[l], ln1_b[l],
                           w_ff1[l], b_ff1[l], w_ff2[l], b_ff2[l], ln2_g[l], ln2_b[l])
    return (yp, ys,
            jnp.stack(kp_l), jnp.stack(vp_l), jnp.stack(Cp_l), jnp.stack(np_l), jnp.stack(mp_l),
            jnp.stack(ks_l), jnp.stack(vs_l), jnp.stack(Cs_l), jnp.stack(ns_l), jnp.stack(ms_l))
```

```python
import functools

import jax
import jax.numpy as jnp
from jax import lax
from jax.experimental import pallas as pl
from jax.experimental.pallas import tpu as pltpu

F32 = jnp.float32
BF16 = jnp.bfloat16

D_MODEL = 1024
CHUNK = 64
LEFT_CHUNKS = 8
ATT_REACH = LEFT_CHUNKS * CHUNK
ATT_WIDTH = 512
MLSTM_WIDTH = 512
ATT_HEADS = 8
ATT_HEAD_DIM = 64
MLSTM_HEADS = 4
MLSTM_HEAD_DIM = 128
MAX_REL = 128
D_FF = 4 * D_MODEL
DEPTH = 1
DEEPNORM_ALPHA = (2 * DEPTH) ** 0.25
LN_EPS = 1e-5
NEG_INF = -1e30
ATT_SCALE = ATT_HEAD_DIM ** -0.5
KB_SCALE = MLSTM_HEAD_DIM ** -0.5

LANES = 128
MAIN_WIDTH = 3 * ATT_WIDTH + 4 * MLSTM_WIDTH
GATE_PAD = LANES
HEAD_PAIRS = ATT_HEADS // 2
VMEM_LIMIT = 56 * 1024 * 1024


def _cparams(sem):
    return pltpu.CompilerParams(dimension_semantics=sem, vmem_limit_bytes=VMEM_LIMIT)


def _in_proj_kernel(x_ref, w_ref, b_ref, qkva_ref, qkvm_ref, og_ref, gates_ref, klast_ref, vlast_ref,
                    *, tm, keep):
    i = pl.program_id(1)
    last = pl.num_programs(1) - 1
    xb = x_ref[0].astype(BF16)

    def proj(c0, width):
        return (jnp.dot(xb, w_ref[:, c0:c0 + width], preferred_element_type=F32)
                + b_ref[:, c0:c0 + width])

    W = ATT_WIDTH
    qa = proj(0, W)
    qkva_ref[0, :, 0:W] = (qa * ATT_SCALE).astype(BF16)
    ka = proj(W, W)
    qkva_ref[0, :, W:2 * W] = ka.astype(BF16)
    va = proj(2 * W, W)
    qkva_ref[0, :, 2 * W:3 * W] = va.astype(BF16)

    @pl.when(i == last)
    def _():
        klast_ref[0] = ka[tm - keep:, :]
        vlast_ref[0] = va[tm - keep:, :]

    qkvm_ref[0, :, 0:W] = proj(3 * W, W).astype(BF16)
    qkvm_ref[0, :, W:2 * W] = (proj(4 * W, W) * KB_SCALE).astype(BF16)
    qkvm_ref[0, :, 2 * W:3 * W] = proj(5 * W, W).astype(BF16)
    ob = proj(6 * W, W)
    og_ref[0] = 1.0 / (1.0 + jnp.exp(-ob))
    g = proj(MAIN_WIDTH, GATE_PAD)
    log_sig = jnp.minimum(g, 0.0) - jnp.log1p(jnp.exp(-jnp.abs(g)))
    lane = lax.broadcasted_iota(jnp.int32, g.shape, 1)
    gates_ref[0] = jnp.where(lane < MLSTM_HEADS, g, log_sig)


def _in_proj(x, w_cat, b_cat, *, tm, keep):
    B, T, D = x.shape
    nt = T // tm
    W3 = 3 * ATT_WIDTH
    wtot = w_cat.shape[1]
    kern = functools.partial(_in_proj_kernel, tm=tm, keep=keep)
    return pl.pallas_call(
        kern,
        grid=(B, nt),
        in_specs=[
            pl.BlockSpec((1, tm, D), lambda b, i: (b, i, 0)),
            pl.BlockSpec((D, wtot), lambda b, i: (0, 0)),
            pl.BlockSpec((1, wtot), lambda b, i: (0, 0)),
        ],
        out_specs=[
            pl.BlockSpec((1, tm, W3), lambda b, i: (b, i, 0)),
            pl.BlockSpec((1, tm, W3), lambda b, i: (b, i, 0)),
            pl.BlockSpec((1, tm, MLSTM_WIDTH), lambda b, i: (b, i, 0)),
            pl.BlockSpec((1, tm, GATE_PAD), lambda b, i: (b, i, 0)),
            pl.BlockSpec((1, keep, ATT_WIDTH), lambda b, i: (b, 0, 0)),
            pl.BlockSpec((1, keep, ATT_WIDTH), lambda b, i: (b, 0, 0)),
        ],
        out_shape=[
            jax.ShapeDtypeStruct((B, T, W3), BF16),
            jax.ShapeDtypeStruct((B, T, W3), BF16),
            jax.ShapeDtypeStruct((B, T, MLSTM_WIDTH), F32),
            jax.ShapeDtypeStruct((B, T, GATE_PAD), F32),
            jax.ShapeDtypeStruct((B, keep, ATT_WIDTH), F32),
            jax.ShapeDtypeStruct((B, keep, ATT_WIDTH), F32),
        ],
        compiler_params=_cparams(("parallel", "arbitrary")),
        name="in_proj",
    )(x, w_cat, b_cat)


def _bias_table_kernel(g_ref, o_ref, *, rows, keys, cps):
    L = g_ref.shape[-1]
    r = lax.broadcasted_iota(jnp.int32, (rows, keys), 0)
    c = lax.broadcasted_iota(jnp.int32, (rows, keys), 1)
    band_start = (r // CHUNK) * CHUNK
    in_band = (c >= band_start) & (c < band_start + ATT_REACH + CHUNK)
    for h in range(ATT_HEADS):
        g = jnp.broadcast_to(g_ref[h], (rows, L))
        t = pltpu.roll(g, L - rows, 1, stride=1, stride_axis=0)
        tab = jnp.where(in_band, t[:, :keys], NEG_INF)
        o_ref[h // 2, (h % 2) * rows:(h % 2 + 1) * rows, :] = tab


def _bias_table(rel_bias, *, cps, keys):
    rows = cps * CHUNK
    L = ((rows + keys + LANES - 1) // LANES) * LANES
    n_const = rows + ATT_REACH - MAX_REL
    const = jnp.broadcast_to(rel_bias[:, 2 * MAX_REL:], (ATT_HEADS, n_const))
    ramp = rel_bias[:, ::-1]
    tail = jnp.broadcast_to(rel_bias[:, :1], (ATT_HEADS, LANES))
    g = jnp.concatenate([const, ramp, tail], axis=1)[:, :L].reshape(ATT_HEADS, 1, L)
    kern = functools.partial(_bias_table_kernel, rows=rows, keys=keys, cps=cps)
    return pl.pallas_call(
        kern,
        out_shape=jax.ShapeDtypeStruct((HEAD_PAIRS, 2 * rows, keys), F32),
        name="bias_table",
    )(g)


def _attn_kernel(q_ref, kc_ref, vc_ref, kp_ref, vp_ref, bias_ref, o_ref, kk_ref, vv_ref,
                 *, tq, cps, keys, mask_first):
    i = pl.program_id(1)
    rows = cps * CHUNK
    kk_ref[0:ATT_REACH, :] = kp_ref[0].astype(BF16)
    vv_ref[0:ATT_REACH, :] = vp_ref[0].astype(BF16)
    kk_ref[ATT_REACH:ATT_REACH + tq, :] = kc_ref[0]
    vv_ref[ATT_REACH:ATT_REACH + tq, :] = vc_ref[0]
    total = kk_ref.shape[0]
    if total > ATT_REACH + tq:
        kk_ref[ATT_REACH + tq:, :] = jnp.zeros((total - ATT_REACH - tq, ATT_WIDTH), BF16)
        vv_ref[ATT_REACH + tq:, :] = jnp.zeros((total - ATT_REACH - tq, ATT_WIDTH), BF16)

    lane = lax.broadcasted_iota(jnp.int32, (rows, LANES), 1)
    first_head = lane < ATT_HEAD_DIM
    col = lax.broadcasted_iota(jnp.int32, (2 * rows, keys), 1)
    for sub in range(tq // rows):
        off = sub * rows
        for pair in range(HEAD_PAIRS):
            ls = slice(pair * LANES, (pair + 1) * LANES)
            q2 = q_ref[0, off:off + rows, ls]
            zero = jnp.zeros_like(q2)
            qs = jnp.concatenate([jnp.where(first_head, q2, zero), jnp.where(first_head, zero, q2)], axis=0)
            s = lax.dot_general(qs, kk_ref[off:off + keys, ls], (((1,), (1,)), ((), ())),
                                preferred_element_type=F32)
            s = s + bias_ref[pair]
            if mask_first:
                first_valid = jnp.where(i == 0, ATT_REACH - off, 0)
                s = jnp.where(col >= first_valid, s, NEG_INF)
            m = jnp.max(s, axis=-1, keepdims=True)
            e = jnp.exp(s - m)
            p = (e / jnp.sum(e, axis=-1, keepdims=True)).astype(BF16)
            o2 = jnp.dot(p, vv_ref[off:off + keys, ls], preferred_element_type=F32)
            o = jnp.where(first_head, o2[:rows], o2[rows:])
            o_ref[0, off:off + rows, ls] = o.astype(BF16)


def _band_attn(qkva, k_prev, v_prev, prev_map, bias_tab, *, tq, cps, keys, mask_first):
    B, T, _ = qkva.shape
    nt = T // tq
    rows_total = max(ATT_REACH + tq, (tq // (cps * CHUNK) - 1) * cps * CHUNK + keys)
    kern = functools.partial(_attn_kernel, tq=tq, cps=cps, keys=keys, mask_first=mask_first)
    W = ATT_WIDTH
    return pl.pallas_call(
        kern,
        grid=(B, nt),
        in_specs=[
            pl.BlockSpec((1, tq, W), lambda b, i: (b, i, 0)),
            pl.BlockSpec((1, tq, W), lambda b, i: (b, i, 1)),
            pl.BlockSpec((1, tq, W), lambda b, i: (b, i, 2)),
            pl.BlockSpec((1, ATT_REACH, W), prev_map[0]),
            pl.BlockSpec((1, ATT_REACH, W), prev_map[1]),
            pl.BlockSpec(bias_tab.shape, lambda b, i: (0, 0, 0)),
        ],
        out_specs=pl.BlockSpec((1, tq, W), lambda b, i: (b, i, 0)),
        out_shape=jax.ShapeDtypeStruct((B, T, W), BF16),
        scratch_shapes=[pltpu.VMEM((rows_total, W), BF16), pltpu.VMEM((rows_total, W), BF16)],
        compiler_params=_cparams(("parallel", "arbitrary")),
        name="band_attn",
    )(qkva, qkva, qkva, k_prev, v_prev, bias_tab)


def _mlstm_kernel(q_ref, k_ref, v_ref, gates_ref, og_ref, ng_ref, c0_ref, n0_ref, m0_ref,
                  mb_ref, c_out_ref, n_out_ref, m_out_ref, c_s, n_s, m_s, *, tb):
    i = pl.program_id(1)
    last = pl.num_programs(1) - 1
    H, d = MLSTM_HEADS, MLSTM_HEAD_DIM

    @pl.when(i == 0)
    def _():
        c_s[...] = c0_ref[0]
        for h in range(H):
            n_s[h] = jnp.broadcast_to(n0_ref[0, h:h + 1, :], (8, d))
        m_s[...] = m0_ref[0]

    g = gates_ref[0]
    pos = lax.broadcasted_iota(jnp.int32, g.shape, 0) % CHUNK
    lane = lax.broadcasted_iota(jnp.int32, g.shape, 1)
    acc = g
    sh = 1
    while sh < CHUNK:
        shifted = pltpu.roll(acc, sh, 0)
        acc = acc + jnp.where((pos >= sh) & (lane >= H), shifted, 0.0)
        sh *= 2
    gt = acc.T

    ti = lax.broadcasted_iota(jnp.int32, (CHUNK, CHUNK), 0)
    si = lax.broadcasted_iota(jnp.int32, (CHUNK, CHUNK), 1)
    causal = ti >= si

    for c in range(tb // CHUNK):
        r0 = c * CHUNK
        rs = slice(r0, r0 + CHUNK)
        for h in range(H):
            hs = slice(h * d, (h + 1) * d)
            q = q_ref[0, rs, hs]
            k = k_ref[0, rs, hs]
            v = v_ref[0, rs, hs]
            bcol = acc[rs, H + h:H + h + 1]
            igcol = acc[rs, h:h + 1]
            brow = gt[H + h:H + h + 1, rs]
            igrow = gt[h:h + 1, rs]
            m_prev = m_s[0:1, h:h + 1]

            logw = jnp.where(causal, bcol - brow + igrow, -jnp.inf)
            inter = bcol + m_prev
            m_t = jnp.maximum(inter, jnp.max(logw, axis=-1, keepdims=True))
            qk = lax.dot_general(q, k, (((1,), (1,)), ((), ())), preferred_element_type=F32)
            w = jnp.exp(logw - m_t) * qk
            gdec = jnp.exp(inter - m_t)
            c_prev = c_s[h]
            n_prev = n_s[h]
            qc = lax.dot_general(q, c_prev.astype(BF16), (((1,), (1,)), ((), ())),
                                 preferred_element_type=F32)
            qn = lax.dot_general(q, n_prev.astype(BF16), (((1,), (1,)), ((), ())),
                                 preferred_element_type=F32)[:, 0:1]
            num = gdec * qc + jnp.dot(w.astype(BF16), v, preferred_element_type=F32)
            den = gdec * qn + jnp.sum(w, axis=-1, keepdims=True)
            hh = num / jnp.maximum(jnp.abs(den), jnp.exp(-m_t))

            b_last = bcol[CHUNK - 1:CHUNK, :]
            logw_end_row = b_last - brow + igrow
            m_new = jnp.maximum(b_last + m_prev, jnp.max(logw_end_row, axis=-1, keepdims=True))
            decay = jnp.exp(b_last + m_prev - m_new)
            w_end_col = jnp.exp(b_last - bcol + igcol - m_new)
            w_end_row = jnp.exp(logw_end_row - m_new)
            vw = (v.astype(F32) * w_end_col).astype(BF16)
            upd = lax.dot_general(vw, k, (((0,), (0,)), ((), ())), preferred_element_type=F32)
            c_s[h] = decay * c_prev + upd
            wr8 = jnp.broadcast_to(w_end_row, (8, CHUNK)).astype(BF16)
            n_s[h] = decay * n_prev + jnp.dot(wr8, k, preferred_element_type=F32)
            m_s[0:1, h:h + 1] = m_new

            mu = jnp.mean(hh, axis=-1, keepdims=True)
            xc = hh - mu
            var = jnp.mean(xc * xc, axis=-1, keepdims=True)
            hn = xc * lax.rsqrt(var + LN_EPS) * ng_ref[:, hs]
            mb_ref[0, rs, hs] = (og_ref[0, rs, hs] * hn).astype(BF16)

    @pl.when(i == last)
    def _():
        c_out_ref[0] = c_s[...]
        for h in range(H):
            n_out_ref[0, h:h + 1, :] = n_s[h, 0:1, :]
        m_out_ref[0] = m_s[...]


def _mlstm(qkvm, gates, og, norm_g, c0, n0, m0, *, tb):
    B, T, _ = qkvm.shape
    nt = T // tb
    W = MLSTM_WIDTH
    H, d = MLSTM_HEADS, MLSTM_HEAD_DIM
    kern = functools.partial(_mlstm_kernel, tb=tb)
    return pl.pallas_call(
        kern,
        grid=(B, nt),
        in_specs=[
            pl.BlockSpec((1, tb, W), lambda b, i: (b, i, 0)),
            pl.BlockSpec((1, tb, W), lambda b, i: (b, i, 1)),
            pl.BlockSpec((1, tb, W), lambda b, i: (b, i, 2)),
            pl.BlockSpec((1, tb, GATE_PAD), lambda b, i: (b, i, 0)),
            pl.BlockSpec((1, tb, W), lambda b, i: (b, i, 0)),
            pl.BlockSpec((1, W), lambda b, i: (0, 0)),
            pl.BlockSpec((1, H, d, d), lambda b, i: (b, 0, 0, 0)),
            pl.BlockSpec((1, H, d), lambda b, i: (b, 0, 0)),
            pl.BlockSpec((1, 1, H), lambda b, i: (b, 0, 0)),
        ],
        out_specs=[
            pl.BlockSpec((1, tb, W), lambda b, i: (b, i, 0)),
            pl.BlockSpec((1, H, d, d), lambda b, i: (b, 0, 0, 0)),
            pl.BlockSpec((1, H, d), lambda b, i: (b, 0, 0)),
            pl.BlockSpec((1, 1, H), lambda b, i: (b, 0, 0)),
        ],
        out_shape=[
            jax.ShapeDtypeStruct((B, T, W), BF16),
            jax.ShapeDtypeStruct((B, H, d, d), F32),
            jax.ShapeDtypeStruct((B, H, d), F32),
            jax.ShapeDtypeStruct((B, 1, H), F32),
        ],
        scratch_shapes=[
            pltpu.VMEM((H, d, d), F32),
            pltpu.VMEM((H, 8, d), F32),
            pltpu.VMEM((1, H), F32),
        ],
        compiler_params=_cparams(("parallel", "arbitrary")),
        name="mlstm",
    )(qkvm, qkvm, qkvm, gates, og, norm_g, c0, n0, m0)


def _layer_norm(z, g, b):
    mu = jnp.mean(z, axis=-1, keepdims=True)
    zc = z - mu
    var = jnp.mean(zc * zc, axis=-1, keepdims=True)
    return zc * lax.rsqrt(var + LN_EPS) * g + b


def _merge_ffn_kernel(x_ref, att_ref, mb_ref, wo_ref, g1_ref, b1_ref, w1_ref, bf1_ref, w2_ref, bf2_ref,
                      g2_ref, b2_ref, y_ref, *, fc):
    W = ATT_WIDTH
    mix = (jnp.dot(att_ref[0], wo_ref[0:W, :], preferred_element_type=F32)
           + jnp.dot(mb_ref[0], wo_ref[W:2 * W, :], preferred_element_type=F32))
    h = _layer_norm(DEEPNORM_ALPHA * x_ref[0] + mix, g1_ref[...], b1_ref[...])
    hb = h.astype(BF16)
    f = jnp.zeros(h.shape, F32)
    for j in range(D_FF // fc):
        a = jnp.dot(hb, w1_ref[:, j * fc:(j + 1) * fc], preferred_element_type=F32) + bf1_ref[:, j * fc:(j + 1) * fc]
        a = jnp.square(jnp.maximum(a, 0.0)).astype(BF16)
        f = f + jnp.dot(a, w2_ref[j * fc:(j + 1) * fc, :], preferred_element_type=F32)
    f = f + bf2_ref[...]
    y_ref[0] = _layer_norm(DEEPNORM_ALPHA * h + f, g2_ref[...], b2_ref[...])


def _merge_ffn(x, att, mb, wo, g1, b1, w1, bf1, w2, bf2, g2, b2, *, tm, fc):
    B, T, D = x.shape
    nt = T // tm
    W = ATT_WIDTH
    const = lambda shape: pl.BlockSpec(shape, lambda b, i: (0, 0), pipeline_mode=pl.Buffered(1))
    kern = functools.partial(_merge_ffn_kernel, fc=fc)
    return pl.pallas_call(
        kern,
        grid=(B, nt),
        in_specs=[
            pl.BlockSpec((1, tm, D), lambda b, i: (b, i, 0)),
            pl.BlockSpec((1, tm, W), lambda b, i: (b, i, 0)),
            pl.BlockSpec((1, tm, W), lambda b, i: (b, i, 0)),
            const((D, D)), const((1, D)), const((1, D)),
            const((D, D_FF)), const((1, D_FF)), const((D_FF, D)), const((1, D)),
            const((1, D)), const((1, D)),
        ],
        out_specs=pl.BlockSpec((1, tm, D), lambda b, i: (b, i, 0)),
        out_shape=jax.ShapeDtypeStruct((B, T, D), F32),
        compiler_params=_cparams(("parallel", "parallel")),
        name="merge_ffn",
    )(x, att, mb, wo, g1, b1, w1, bf1, w2, bf2, g2, b2)


def _token_tile(t):
    return 512 if t % 512 == 0 else t


def kernel(x_prompt, x_sample, cache_k, cache_v, state_C, state_n, state_m, w_in, b_in, rel_bias,
           mlstm_norm_g, w_out, ln1_g, ln1_b, w_ff1, b_ff1, w_ff2, b_ff2, ln2_g, ln2_b):
    assert w_in.shape[0] == DEPTH == 1
    B, S, D = x_prompt.shape
    DB, T, _ = x_sample.shape
    H, d = MLSTM_HEADS, MLSTM_HEAD_DIM
    assert S % ATT_REACH == 0 and T == CHUNK and cache_k.shape[2] == ATT_REACH

    n_gate = 2 * H
    w_cat = jnp.concatenate(
        [w_in[0, :, :MAIN_WIDTH], jnp.pad(w_in[0, :, MAIN_WIDTH:], ((0, 0), (0, GATE_PAD - n_gate)))],
        axis=1).astype(BF16)
    b_cat = jnp.concatenate(
        [b_in[0, :MAIN_WIDTH], jnp.pad(b_in[0, MAIN_WIDTH:], (0, GATE_PAD - n_gate))])[None, :]
    wo = w_out[0].astype(BF16)
    w1 = w_ff1[0].astype(BF16)
    w2 = w_ff2[0].astype(BF16)
    row = lambda p: p[0][None, :]
    ffn_params = (wo, row(ln1_g), row(ln1_b), w1, row(b_ff1), w2, row(b_ff2), row(ln2_g), row(ln2_b))
    norm_g = row(mlstm_norm_g)

    def layer(x, k_prev, v_prev, prev_map, c0, n0, m0, *, seqs, cps, mask_first):
        Bx, Tx, _ = x.shape
        n_tok = Bx * Tx
        t_seq = n_tok // seqs
        tm = _token_tile(Tx)
        keep = min(ATT_REACH, Tx)
        qkva, qkvm, og, gates, k_last, v_last = _in_proj(x, w_cat, b_cat, tm=tm, keep=keep)
        seq = lambda a: a.reshape(seqs, t_seq, a.shape[-1])
        tq = _token_tile(t_seq)
        keys = -(-(ATT_REACH + cps * CHUNK) // LANES) * LANES
        bias_tab = _bias_table(rel_bias[0], cps=cps, keys=keys)
        if k_prev is None:
            k_prev = v_prev = seq(qkva)
        att = _band_attn(seq(qkva), k_prev, v_prev, prev_map, bias_tab,
                         tq=tq, cps=cps, keys=keys, mask_first=mask_first)
        mb, c_new, n_new, m_new = _mlstm(seq(qkvm), seq(gates), seq(og), norm_g, c0, n0, m0, tb=tq)
        tok = lambda a: a.reshape(Bx, Tx, a.shape[-1])
        y = _merge_ffn(x, tok(att), tok(mb), *ffn_params, tm=tm, fc=1024)
        return y, k_last, v_last, c_new, n_new, m_new

    zeros_c = jnp.zeros((B, H, d, d), F32)
    zeros_n = jnp.zeros((B, H, d), F32)
    zeros_m = jnp.zeros((B, 1, H), F32)
    prev_prompt = (lambda b, i: (b, jnp.maximum(i - 1, 0), 1), lambda b, i: (b, jnp.maximum(i - 1, 0), 2))
    yp, kp, vp, cp, np_, mp = layer(x_prompt, None, None, prev_prompt, zeros_c, zeros_n, zeros_m,
                                    seqs=B, cps=2, mask_first=True)

    ck = cache_k[0].reshape(DB, ATT_REACH, ATT_WIDTH)
    cv = cache_v[0].reshape(DB, ATT_REACH, ATT_WIDTH)
    prev_sample = (lambda b, i: (b, 0, 0), lambda b, i: (b, 0, 0))
    ys, ks, vs, cs, ns, ms = layer(
        x_sample.reshape(1, DB * T, D), ck, cv, prev_sample,
        state_C[0].astype(F32), state_n[0].astype(F32), state_m[0].astype(F32).reshape(DB, 1, H),
        seqs=DB, cps=1, mask_first=False)

    sd = state_C.dtype
    heads = lambda a, nb, t: a.reshape(nb, t, ATT_HEADS, ATT_HEAD_DIM)[None]
    keep_p = min(ATT_REACH, S)
    return (yp, ys.reshape(DB, T, D),
            heads(kp, B, keep_p).astype(cache_k.dtype), heads(vp, B, keep_p).astype(cache_v.dtype),
            cp[None].astype(sd), np_[None].astype(sd), mp.reshape(1, B, H).astype(sd),
            heads(ks, DB, T).astype(cache_k.dtype), heads(vs, DB, T).astype(cache_v.dtype),
            cs[None].astype(sd), ns[None].astype(sd), ms.reshape(1, DB, H).astype(sd))
```

```python
import functools

import jax
import jax.numpy as jnp
from jax import lax
from jax.experimental import pallas as pl
from jax.experimental.pallas import tpu as pltpu

F32 = jnp.float32
BF16 = jnp.bfloat16

D_MODEL = 1024
CHUNK = 64
LEFT_CHUNKS = 8
ATT_REACH = LEFT_CHUNKS * CHUNK
ATT_WIDTH = 512
MLSTM_WIDTH = 512
ATT_HEADS = 8
ATT_HEAD_DIM = 64
MLSTM_HEADS = 4
MLSTM_HEAD_DIM = 128
MAX_REL = 128
D_FF = 4 * D_MODEL
DEPTH = 1
DEEPNORM_ALPHA = (2 * DEPTH) ** 0.25
LN_EPS = 1e-5
NEG_INF = -1e30
ATT_SCALE = ATT_HEAD_DIM ** -0.5
KB_SCALE = MLSTM_HEAD_DIM ** -0.5

LANES = 128
MAIN_WIDTH = 3 * ATT_WIDTH + 4 * MLSTM_WIDTH
GATE_PAD = LANES
MLSTM_CHUNK = 256
GATE_ROWS = 24
HEAD_PAIRS = ATT_HEADS // 2
VMEM_LIMIT = 56 * 1024 * 1024


def _cparams(sem):
    return pltpu.CompilerParams(dimension_semantics=sem, vmem_limit_bytes=VMEM_LIMIT)


def _in_proj_kernel(x_ref, w_ref, b_ref, qkva_ref, qkvm_ref, og_ref, gates_ref, klast_ref, vlast_ref,
                    *, tm, keep, lc):
    i = pl.program_id(1)
    last = pl.num_programs(1) - 1
    xb = x_ref[0].astype(BF16)

    def proj(c0, width):
        return (jnp.dot(xb, w_ref[:, c0:c0 + width], preferred_element_type=F32)
                + b_ref[:, c0:c0 + width])

    W = ATT_WIDTH
    qa = proj(0, W)
    qkva_ref[0, :, 0:W] = (qa * ATT_SCALE).astype(BF16)
    ka = proj(W, W)
    qkva_ref[0, :, W:2 * W] = ka.astype(BF16)
    va = proj(2 * W, W)
    qkva_ref[0, :, 2 * W:3 * W] = va.astype(BF16)

    @pl.when(i == last)
    def _():
        klast_ref[0] = ka[tm - keep:, :]
        vlast_ref[0] = va[tm - keep:, :]

    qkvm_ref[0, :, 0:W] = proj(3 * W, W).astype(BF16)
    qkvm_ref[0, :, W:2 * W] = (proj(4 * W, W) * KB_SCALE).astype(BF16)
    qkvm_ref[0, :, 2 * W:3 * W] = proj(5 * W, W).astype(BF16)
    ob = proj(6 * W, W)
    og_ref[0] = 1.0 / (1.0 + jnp.exp(-ob))
    g = proj(MAIN_WIDTH, GATE_PAD)
    log_sig = jnp.minimum(g, 0.0) - jnp.log1p(jnp.exp(-jnp.abs(g)))
    lane = lax.broadcasted_iota(jnp.int32, g.shape, 1)
    H = MLSTM_HEADS
    g8 = jnp.where(lane < H, g, log_sig).T[0:2 * H]
    pos = lax.broadcasted_iota(jnp.int32, g8.shape, 1) % lc
    head_row = lax.broadcasted_iota(jnp.int32, g8.shape, 0) < H
    csum = g8
    sh = 1
    while sh < lc:
        csum = csum + jnp.where(pos >= sh, pltpu.roll(csum, sh, 1), 0.0)
        sh *= 2
    b = pltpu.roll(csum, H, 0)
    a = g8 - b
    cm = a
    sh = 1
    while sh < lc:
        cm = jnp.maximum(cm, jnp.where(pos >= sh, pltpu.roll(cm, sh, 1), -jnp.inf))
        sh *= 2
    rows = [jnp.where(head_row, t, 0.0) for t in (b, a, cm)]
    for c in range(tm // lc):
        for j, t in enumerate(rows):
            gates_ref[c, 8 * j:8 * j + 8, :] = t[:, c * lc:(c + 1) * lc]


def _in_proj(x, w_cat, b_cat, *, tm, keep, lc):
    B, T, D = x.shape
    nt = T // tm
    W3 = 3 * ATT_WIDTH
    wtot = w_cat.shape[1]
    kern = functools.partial(_in_proj_kernel, tm=tm, keep=keep, lc=lc)
    return pl.pallas_call(
        kern,
        grid=(B, nt),
        in_specs=[
            pl.BlockSpec((1, tm, D), lambda b, i: (b, i, 0)),
            pl.BlockSpec((D, wtot), lambda b, i: (0, 0)),
            pl.BlockSpec((1, wtot), lambda b, i: (0, 0)),
        ],
        out_specs=[
            pl.BlockSpec((1, tm, W3), lambda b, i: (b, i, 0)),
            pl.BlockSpec((1, tm, W3), lambda b, i: (b, i, 0)),
            pl.BlockSpec((1, tm, MLSTM_WIDTH), lambda b, i: (b, i, 0)),
            pl.BlockSpec((tm // lc, GATE_ROWS, lc), lambda b, i: (b * nt + i, 0, 0)),
            pl.BlockSpec((1, keep, ATT_WIDTH), lambda b, i: (b, 0, 0)),
            pl.BlockSpec((1, keep, ATT_WIDTH), lambda b, i: (b, 0, 0)),
        ],
        out_shape=[
            jax.ShapeDtypeStruct((B, T, W3), BF16),
            jax.ShapeDtypeStruct((B, T, W3), BF16),
            jax.ShapeDtypeStruct((B, T, MLSTM_WIDTH), F32),
            jax.ShapeDtypeStruct((B * T // lc, GATE_ROWS, lc), F32),
            jax.ShapeDtypeStruct((B, keep, ATT_WIDTH), F32),
            jax.ShapeDtypeStruct((B, keep, ATT_WIDTH), F32),
        ],
        compiler_params=_cparams(("parallel", "arbitrary")),
        name="in_proj",
    )(x, w_cat, b_cat)


def _bias_table_kernel(g_ref, o_ref, *, rows, keys, cps):
    L = g_ref.shape[-1]
    r = lax.broadcasted_iota(jnp.int32, (rows, keys), 0)
    c = lax.broadcasted_iota(jnp.int32, (rows, keys), 1)
    band_start = (r // CHUNK) * CHUNK
    in_band = (c >= band_start) & (c < band_start + ATT_REACH + CHUNK)
    for h in range(ATT_HEADS):
        g = jnp.broadcast_to(g_ref[h], (rows, L))
        t = pltpu.roll(g, L - rows, 1, stride=1, stride_axis=0)
        tab = jnp.where(in_band, t[:, :keys], NEG_INF)
        o_ref[h // 2, (h % 2) * rows:(h % 2 + 1) * rows, :] = tab


def _bias_table(rel_bias, *, cps, keys):
    rows = cps * CHUNK
    L = ((rows + keys + LANES - 1) // LANES) * LANES
    n_const = rows + ATT_REACH - MAX_REL
    const = jnp.broadcast_to(rel_bias[:, 2 * MAX_REL:], (ATT_HEADS, n_const))
    ramp = rel_bias[:, ::-1]
    tail = jnp.broadcast_to(rel_bias[:, :1], (ATT_HEADS, LANES))
    g = jnp.concatenate([const, ramp, tail], axis=1)[:, :L].reshape(ATT_HEADS, 1, L)
    kern = functools.partial(_bias_table_kernel, rows=rows, keys=keys, cps=cps)
    return pl.pallas_call(
        kern,
        out_shape=jax.ShapeDtypeStruct((HEAD_PAIRS, 2 * rows, keys), F32),
        name="bias_table",
    )(g)


def _attn_kernel(q_ref, kc_ref, vc_ref, kp_ref, vp_ref, bias_ref, o_ref, kk_ref, vv_ref,
                 *, tq, cps, keys, mask_first):
    i = pl.program_id(1)
    rows = cps * CHUNK
    kk_ref[0:ATT_REACH, :] = kp_ref[0].astype(BF16)
    vv_ref[0:ATT_REACH, :] = vp_ref[0].astype(BF16)
    kk_ref[ATT_REACH:ATT_REACH + tq, :] = kc_ref[0]
    vv_ref[ATT_REACH:ATT_REACH + tq, :] = vc_ref[0]
    total = kk_ref.shape[0]
    if total > ATT_REACH + tq:
        kk_ref[ATT_REACH + tq:, :] = jnp.zeros((total - ATT_REACH - tq, ATT_WIDTH), BF16)
        vv_ref[ATT_REACH + tq:, :] = jnp.zeros((total - ATT_REACH - tq, ATT_WIDTH), BF16)

    lane = lax.broadcasted_iota(jnp.int32, (rows, LANES), 1)
    first_head = lane < ATT_HEAD_DIM
    col = lax.broadcasted_iota(jnp.int32, (2 * rows, keys), 1)
    for sub in range(tq // rows):
        off = sub * rows
        for pair in range(HEAD_PAIRS):
            ls = slice(pair * LANES, (pair + 1) * LANES)
            q2 = q_ref[0, off:off + rows, ls]
            zero = jnp.zeros_like(q2)
            qs = jnp.concatenate([jnp.where(first_head, q2, zero), jnp.where(first_head, zero, q2)], axis=0)
            s = lax.dot_general(qs, kk_ref[off:off + keys, ls], (((1,), (1,)), ((), ())),
                                preferred_element_type=F32)
            s = s + bias_ref[pair]
            if mask_first:
                first_valid = jnp.where(i == 0, ATT_REACH - off, 0)
                s = jnp.where(col >= first_valid, s, NEG_INF)
            m = jnp.max(s, axis=-1, keepdims=True)
            e = jnp.exp(s - m)
            p = (e / jnp.sum(e, axis=-1, keepdims=True)).astype(BF16)
            o2 = jnp.dot(p, vv_ref[off:off + keys, ls], preferred_element_type=F32)
            o = jnp.where(first_head, o2[:rows], o2[rows:])
            o_ref[0, off:off + rows, ls] = o.astype(BF16)


def _band_attn(qkva, k_prev, v_prev, prev_map, bias_tab, *, tq, cps, keys, mask_first):
    B, T, _ = qkva.shape
    nt = T // tq
    rows_total = max(ATT_REACH + tq, (tq // (cps * CHUNK) - 1) * cps * CHUNK + keys)
    kern = functools.partial(_attn_kernel, tq=tq, cps=cps, keys=keys, mask_first=mask_first)
    W = ATT_WIDTH
    return pl.pallas_call(
        kern,
        grid=(B, nt),
        in_specs=[
            pl.BlockSpec((1, tq, W), lambda b, i: (b, i, 0)),
            pl.BlockSpec((1, tq, W), lambda b, i: (b, i, 1)),
            pl.BlockSpec((1, tq, W), lambda b, i: (b, i, 2)),
            pl.BlockSpec((1, ATT_REACH, W), prev_map[0]),
            pl.BlockSpec((1, ATT_REACH, W), prev_map[1]),
            pl.BlockSpec(bias_tab.shape, lambda b, i: (0, 0, 0)),
        ],
        out_specs=pl.BlockSpec((1, tq, W), lambda b, i: (b, i, 0)),
        out_shape=jax.ShapeDtypeStruct((B, T, W), BF16),
        scratch_shapes=[pltpu.VMEM((rows_total, W), BF16), pltpu.VMEM((rows_total, W), BF16)],
        compiler_params=_cparams(("parallel", "arbitrary")),
        name="band_attn",
    )(qkva, qkva, qkva, k_prev, v_prev, bias_tab)


def _old_mlstm_kernel(q_ref, k_ref, v_ref, gates_ref, og_ref, ng_ref, c0_ref, n0_ref, m0_ref,
                  mb_ref, c_out_ref, n_out_ref, m_out_ref, c_s, n_s, m_s, *, tb):
    i = pl.program_id(1)
    last = pl.num_programs(1) - 1
    H, d = MLSTM_HEADS, MLSTM_HEAD_DIM

    @pl.when(i == 0)
    def _():
        c_s[...] = c0_ref[0]
        for h in range(H):
            n_s[h] = jnp.broadcast_to(n0_ref[0, h:h + 1, :], (8, d))
        m_s[...] = m0_ref[0]

    g = gates_ref[0]
    pos = lax.broadcasted_iota(jnp.int32, g.shape, 0) % CHUNK
    lane = lax.broadcasted_iota(jnp.int32, g.shape, 1)
    acc = g
    sh = 1
    while sh < CHUNK:
        shifted = pltpu.roll(acc, sh, 0)
        acc = acc + jnp.where((pos >= sh) & (lane >= H), shifted, 0.0)
        sh *= 2
    gt = acc.T

    ti = lax.broadcasted_iota(jnp.int32, (CHUNK, CHUNK), 0)
    si = lax.broadcasted_iota(jnp.int32, (CHUNK, CHUNK), 1)
    causal = ti >= si

    for c in range(tb // CHUNK):
        r0 = c * CHUNK
        rs = slice(r0, r0 + CHUNK)
        for h in range(H):
            hs = slice(h * d, (h + 1) * d)
            q = q_ref[0, rs, hs]
            k = k_ref[0, rs, hs]
            v = v_ref[0, rs, hs]
            bcol = acc[rs, H + h:H + h + 1]
            igcol = acc[rs, h:h + 1]
            brow = gt[H + h:H + h + 1, rs]
            igrow = gt[h:h + 1, rs]
            m_prev = m_s[0:1, h:h + 1]

            logw = jnp.where(causal, bcol - brow + igrow, -jnp.inf)
            inter = bcol + m_prev
            m_t = jnp.maximum(inter, jnp.max(logw, axis=-1, keepdims=True))
            qk = lax.dot_general(q, k, (((1,), (1,)), ((), ())), preferred_element_type=F32)
            w = jnp.exp(logw - m_t) * qk
            gdec = jnp.exp(inter - m_t)
            c_prev = c_s[h]
            n_prev = n_s[h]
            qc = lax.dot_general(q, c_prev.astype(BF16), (((1,), (1,)), ((), ())),
                                 preferred_element_type=F32)
            qn = lax.dot_general(q, n_prev.astype(BF16), (((1,), (1,)), ((), ())),
                                 preferred_element_type=F32)[:, 0:1]
            num = gdec * qc + jnp.dot(w.astype(BF16), v, preferred_element_type=F32)
            den = gdec * qn + jnp.sum(w, axis=-1, keepdims=True)
            hh = num / jnp.maximum(jnp.abs(den), jnp.exp(-m_t))

            b_last = bcol[CHUNK - 1:CHUNK, :]
            logw_end_row = b_last - brow + igrow
            m_new = jnp.maximum(b_last + m_prev, jnp.max(logw_end_row, axis=-1, keepdims=True))
            decay = jnp.exp(b_last + m_prev - m_new)
            w_end_col = jnp.exp(b_last - bcol + igcol - m_new)
            w_end_row = jnp.exp(logw_end_row - m_new)
            vw = (v.astype(F32) * w_end_col).astype(BF16)
            upd = lax.dot_general(vw, k, (((0,), (0,)), ((), ())), preferred_element_type=F32)
            c_s[h] = decay * c_prev + upd
            wr8 = jnp.broadcast_to(w_end_row, (8, CHUNK)).astype(BF16)
            n_s[h] = decay * n_prev + jnp.dot(wr8, k, preferred_element_type=F32)
            m_s[0:1, h:h + 1] = m_new

            mu = jnp.mean(hh, axis=-1, keepdims=True)
            xc = hh - mu
            var = jnp.mean(xc * xc, axis=-1, keepdims=True)
            hn = xc * lax.rsqrt(var + LN_EPS) * ng_ref[:, hs]
            mb_ref[0, rs, hs] = (og_ref[0, rs, hs] * hn).astype(BF16)

    @pl.when(i == last)
    def _():
        c_out_ref[0] = c_s[...]
        for h in range(H):
            n_out_ref[0, h:h + 1, :] = n_s[h, 0:1, :]
        m_out_ref[0] = m_s[...]


def _old_mlstm(qkvm, gates, og, norm_g, c0, n0, m0, *, tb):
    B, T, _ = qkvm.shape
    nt = T // tb
    W = MLSTM_WIDTH
    H, d = MLSTM_HEADS, MLSTM_HEAD_DIM
    kern = functools.partial(_old_mlstm_kernel, tb=tb)
    return pl.pallas_call(
        kern,
        grid=(B, nt),
        in_specs=[
            pl.BlockSpec((1, tb, W), lambda b, i: (b, i, 0)),
            pl.BlockSpec((1, tb, W), lambda b, i: (b, i, 1)),
            pl.BlockSpec((1, tb, W), lambda b, i: (b, i, 2)),
            pl.BlockSpec((1, tb, GATE_PAD), lambda b, i: (b, i, 0)),
            pl.BlockSpec((1, tb, W), lambda b, i: (b, i, 0)),
            pl.BlockSpec((1, W), lambda b, i: (0, 0)),
            pl.BlockSpec((1, H, d, d), lambda b, i: (b, 0, 0, 0)),
            pl.BlockSpec((1, H, d), lambda b, i: (b, 0, 0)),
            pl.BlockSpec((1, 1, H), lambda b, i: (b, 0, 0)),
        ],
        out_specs=[
            pl.BlockSpec((1, tb, W), lambda b, i: (b, i, 0)),
            pl.BlockSpec((1, H, d, d), lambda b, i: (b, 0, 0, 0)),
            pl.BlockSpec((1, H, d), lambda b, i: (b, 0, 0)),
            pl.BlockSpec((1, 1, H), lambda b, i: (b, 0, 0)),
        ],
        out_shape=[
            jax.ShapeDtypeStruct((B, T, W), BF16),
            jax.ShapeDtypeStruct((B, H, d, d), F32),
            jax.ShapeDtypeStruct((B, H, d), F32),
            jax.ShapeDtypeStruct((B, 1, H), F32),
        ],
        scratch_shapes=[
            pltpu.VMEM((H, d, d), F32),
            pltpu.VMEM((H, 8, d), F32),
            pltpu.VMEM((1, H), F32),
        ],
        compiler_params=_cparams(("parallel", "arbitrary")),
        name="mlstm",
    )(qkvm, qkvm, qkvm, gates, og, norm_g, c0, n0, m0)


def _mlstm_kernel(q_ref, k_ref, v_ref, gates_ref, og_ref, ng_ref, c0_ref, n0_ref, m0_ref,
                  mb_ref, c_out_ref, n_out_ref, m_out_ref, c_s, n_s, m_s, *, lc):
    i = pl.program_id(1)
    last = pl.num_programs(1) - 1
    H, d = MLSTM_HEADS, MLSTM_HEAD_DIM
    NT = (((1,), (1,)), ((), ()))

    @pl.when(i == 0)
    def _():
        c_s[...] = c0_ref[0]
        for h in range(H):
            n_s[h] = jnp.broadcast_to(n0_ref[0, h:h + 1, :], (8, d))
        m_s[...] = jnp.zeros(m_s.shape, F32)
        m_s[0:H, :] = jnp.broadcast_to(m0_ref[0], (H, LANES))

    b = gates_ref[0, 0:8, :]
    a = gates_ref[0, 8:16, :]
    cm = gates_ref[0, 16:24, :]
    m_prev = m_s[:, 0:1]
    inter = b + m_prev
    m_t = jnp.maximum(inter, b + cm)
    b_last = b[:, lc - 1:lc]
    m_new = jnp.maximum(b_last + m_prev, b_last + cm[:, lc - 1:lc])
    decay = jnp.exp(b_last + m_prev - m_new)
    w_end = jnp.exp(b_last + a - m_new)
    stacked = jnp.concatenate(
        [b - m_t, jnp.exp(inter - m_t), jnp.exp(-m_t), w_end, jnp.zeros((LANES - 32, lc), F32)], axis=0)
    cols = stacked.T
    m_s[...] = jnp.broadcast_to(m_new, m_s.shape)

    ti = lax.broadcasted_iota(jnp.int32, (lc, lc), 0)
    si = lax.broadcasted_iota(jnp.int32, (lc, lc), 1)
    causal = ti >= si

    for h in range(H):
        hs = slice(h * d, (h + 1) * d)
        q = q_ref[0, :, hs]
        k = k_ref[0, :, hs]
        v = v_ref[0, :, hs]
        ct_col = cols[:, h:h + 1]
        gdec = cols[:, 8 + h:9 + h]
        emt = cols[:, 16 + h:17 + h]
        w_end_col = cols[:, 24 + h:25 + h]

        qk = lax.dot_general(q, k, NT, preferred_element_type=F32)
        w = jnp.exp(jnp.where(causal, ct_col + a[h:h + 1, :], -jnp.inf)) * qk
        c_prev = c_s[h]
        n_prev = n_s[h]
        cn = jnp.concatenate([c_prev.astype(BF16), n_prev.astype(BF16)], axis=0)
        qcn = lax.dot_general(q, cn, NT, preferred_element_type=F32)
        num = gdec * qcn[:, 0:d] + jnp.dot(w.astype(BF16), v, preferred_element_type=F32)
        den = gdec * qcn[:, d:d + 1] + jnp.sum(w, axis=-1, keepdims=True)
        hh = num * (1.0 / jnp.maximum(jnp.abs(den), emt))

        vw = (v.astype(F32) * w_end_col).astype(BF16)
        upd = lax.dot_general(vw, k, (((0,), (0,)), ((), ())), preferred_element_type=F32)
        dec_h = decay[h:h + 1, :]
        c_s[h] = dec_h * c_prev + upd
        wr8 = jnp.broadcast_to(w_end[h:h + 1, :], (8, lc)).astype(BF16)
        n_s[h] = dec_h * n_prev + jnp.dot(wr8, k, preferred_element_type=F32)

        mu = jnp.mean(hh, axis=-1, keepdims=True)
        xc = hh - mu
        var = jnp.mean(xc * xc, axis=-1, keepdims=True)
        hn = xc * lax.rsqrt(var + LN_EPS) * ng_ref[:, hs]
        mb_ref[0, :, hs] = (og_ref[0, :, hs] * hn).astype(BF16)

    @pl.when(i == last)
    def _():
        c_out_ref[0] = c_s[...]
        for h in range(H):
            n_out_ref[0, h:h + 1, :] = n_s[h, 0:1, :]
        m_out_ref[0] = m_s[0:H, 0:1]


def _mlstm(qkvm, gates, og, norm_g, c0, n0, m0, *, lc):
    B, T, _ = qkvm.shape
    nt = T // lc
    W = MLSTM_WIDTH
    H, d = MLSTM_HEADS, MLSTM_HEAD_DIM
    kern = functools.partial(_mlstm_kernel, lc=lc)
    return pl.pallas_call(
        kern,
        grid=(B, nt),
        in_specs=[
            pl.BlockSpec((1, lc, W), lambda b, i: (b, i, 0)),
            pl.BlockSpec((1, lc, W), lambda b, i: (b, i, 1)),
            pl.BlockSpec((1, lc, W), lambda b, i: (b, i, 2)),
            pl.BlockSpec((1, GATE_ROWS, lc), lambda b, i: (b * nt + i, 0, 0)),
            pl.BlockSpec((1, lc, W), lambda b, i: (b, i, 0)),
            pl.BlockSpec((1, W), lambda b, i: (0, 0)),
            pl.BlockSpec((1, H, d, d), lambda b, i: (b, 0, 0, 0)),
            pl.BlockSpec((1, H, d), lambda b, i: (b, 0, 0)),
            pl.BlockSpec((1, H, 1), lambda b, i: (b, 0, 0)),
        ],
        out_specs=[
            pl.BlockSpec((1, lc, W), lambda b, i: (b, i, 0)),
            pl.BlockSpec((1, H, d, d), lambda b, i: (b, 0, 0, 0)),
            pl.BlockSpec((1, H, d), lambda b, i: (b, 0, 0)),
            pl.BlockSpec((1, H, 1), lambda b, i: (b, 0, 0)),
        ],
        out_shape=[
            jax.ShapeDtypeStruct((B, T, W), BF16),
            jax.ShapeDtypeStruct((B, H, d, d), F32),
            jax.ShapeDtypeStruct((B, H, d), F32),
            jax.ShapeDtypeStruct((B, H, 1), F32),
        ],
        scratch_shapes=[
            pltpu.VMEM((H, d, d), F32),
            pltpu.VMEM((H, 8, d), F32),
            pltpu.VMEM((8, LANES), F32),
        ],
        compiler_params=_cparams(("parallel", "arbitrary")),
        name="mlstm",
    )(qkvm, qkvm, qkvm, gates, og, norm_g, c0, n0, m0)


def _layer_norm(z, g, b):
    mu = jnp.mean(z, axis=-1, keepdims=True)
    zc = z - mu
    var = jnp.mean(zc * zc, axis=-1, keepdims=True)
    return zc * lax.rsqrt(var + LN_EPS) * g + b


def _merge_ffn_kernel(x_ref, att_ref, mb_ref, wo_ref, g1_ref, b1_ref, w1_ref, bf1_ref, w2_ref, bf2_ref,
                      g2_ref, b2_ref, y_ref, *, fc):
    W = ATT_WIDTH
    mix = (jnp.dot(att_ref[0], wo_ref[0:W, :], preferred_element_type=F32)
           + jnp.dot(mb_ref[0], wo_ref[W:2 * W, :], preferred_element_type=F32))
    h = _layer_norm(DEEPNORM_ALPHA * x_ref[0] + mix, g1_ref[...], b1_ref[...])
    hb = h.astype(BF16)
    f = jnp.zeros(h.shape, F32)
    for j in range(D_FF // fc):
        a = jnp.dot(hb, w1_ref[:, j * fc:(j + 1) * fc], preferred_element_type=F32) + bf1_ref[:, j * fc:(j + 1) * fc]
        a = jnp.square(jnp.maximum(a, 0.0)).astype(BF16)
        f = f + jnp.dot(a, w2_ref[j * fc:(j + 1) * fc, :], preferred_element_type=F32)
    f = f + bf2_ref[...]
    y_ref[0] = _layer_norm(DEEPNORM_ALPHA * h + f, g2_ref[...], b2_ref[...])


def _merge_ffn(x, att, mb, wo, g1, b1, w1, bf1, w2, bf2, g2, b2, *, tm, fc):
    B, T, D = x.shape
    nt = T // tm
    W = ATT_WIDTH
    const = lambda shape: pl.BlockSpec(shape, lambda b, i: (0, 0), pipeline_mode=pl.Buffered(1))
    kern = functools.partial(_merge_ffn_kernel, fc=fc)
    return pl.pallas_call(
        kern,
        grid=(B, nt),
        in_specs=[
            pl.BlockSpec((1, tm, D), lambda b, i: (b, i, 0)),
            pl.BlockSpec((1, tm, W), lambda b, i: (b, i, 0)),
            pl.BlockSpec((1, tm, W), lambda b, i: (b, i, 0)),
            const((D, D)), const((1, D)), const((1, D)),
            const((D, D_FF)), const((1, D_FF)), const((D_FF, D)), const((1, D)),
            const((1, D)), const((1, D)),
        ],
        out_specs=pl.BlockSpec((1, tm, D), lambda b, i: (b, i, 0)),
        out_shape=jax.ShapeDtypeStruct((B, T, D), F32),
        compiler_params=_cparams(("parallel", "parallel")),
        name="merge_ffn",
    )(x, att, mb, wo, g1, b1, w1, bf1, w2, bf2, g2, b2)


def _token_tile(t):
    return 512 if t % 512 == 0 else t


def kernel(x_prompt, x_sample, cache_k, cache_v, state_C, state_n, state_m, w_in, b_in, rel_bias,
           mlstm_norm_g, w_out, ln1_g, ln1_b, w_ff1, b_ff1, w_ff2, b_ff2, ln2_g, ln2_b):
    assert w_in.shape[0] == DEPTH == 1
    B, S, D = x_prompt.shape
    DB, T, _ = x_sample.shape
    H, d = MLSTM_HEADS, MLSTM_HEAD_DIM
    assert S % ATT_REACH == 0 and T == CHUNK and cache_k.shape[2] == ATT_REACH

    n_gate = 2 * H
    w_cat = jnp.concatenate(
        [w_in[0, :, :MAIN_WIDTH], jnp.pad(w_in[0, :, MAIN_WIDTH:], ((0, 0), (0, GATE_PAD - n_gate)))],
        axis=1).astype(BF16)
    b_cat = jnp.concatenate(
        [b_in[0, :MAIN_WIDTH], jnp.pad(b_in[0, MAIN_WIDTH:], (0, GATE_PAD - n_gate))])[None, :]
    wo = w_out[0].astype(BF16)
    w1 = w_ff1[0].astype(BF16)
    w2 = w_ff2[0].astype(BF16)
    row = lambda p: p[0][None, :]
    ffn_params = (wo, row(ln1_g), row(ln1_b), w1, row(b_ff1), w2, row(b_ff2), row(ln2_g), row(ln2_b))
    norm_g = row(mlstm_norm_g)

    def layer(x, k_prev, v_prev, prev_map, c0, n0, m0, *, seqs, cps, mask_first):
        Bx, Tx, _ = x.shape
        n_tok = Bx * Tx
        t_seq = n_tok // seqs
        tm = _token_tile(Tx)
        keep = min(ATT_REACH, Tx)
        lc = min(MLSTM_CHUNK, t_seq)
        qkva, qkvm, og, gates, k_last, v_last = _in_proj(x, w_cat, b_cat, tm=tm, keep=keep, lc=lc)
        seq = lambda a: a.reshape(seqs, t_seq, a.shape[-1])
        tq = _token_tile(t_seq)
        keys = -(-(ATT_REACH + cps * CHUNK) // LANES) * LANES
        bias_tab = _bias_table(rel_bias[0], cps=cps, keys=keys)
        if k_prev is None:
            k_prev = v_prev = seq(qkva)
        att = _band_attn(seq(qkva), k_prev, v_prev, prev_map, bias_tab,
                         tq=tq, cps=cps, keys=keys, mask_first=mask_first)
        mb, c_new, n_new, m_new = _mlstm(seq(qkvm), gates, seq(og), norm_g, c0, n0, m0, lc=lc)
        tok = lambda a: a.reshape(Bx, Tx, a.shape[-1])
        y = _merge_ffn(x, tok(att), tok(mb), *ffn_params, tm=tm, fc=1024)
        return y, k_last, v_last, c_new, n_new, m_new

    zeros_c = jnp.zeros((B, H, d, d), F32)
    zeros_n = jnp.zeros((B, H, d), F32)
    zeros_m = jnp.zeros((B, H, 1), F32)
    prev_prompt = (lambda b, i: (b, jnp.maximum(i - 1, 0), 1), lambda b, i: (b, jnp.maximum(i - 1, 0), 2))
    yp, kp, vp, cp, np_, mp = layer(x_prompt, None, None, prev_prompt, zeros_c, zeros_n, zeros_m,
                                    seqs=B, cps=2, mask_first=True)

    ck = cache_k[0].reshape(DB, ATT_REACH, ATT_WIDTH)
    cv = cache_v[0].reshape(DB, ATT_REACH, ATT_WIDTH)
    prev_sample = (lambda b, i: (b, 0, 0), lambda b, i: (b, 0, 0))
    ys, ks, vs, cs, ns, ms = layer(
        x_sample.reshape(1, DB * T, D), ck, cv, prev_sample,
        state_C[0].astype(F32), state_n[0].astype(F32), state_m[0].astype(F32).reshape(DB, H, 1),
        seqs=DB, cps=1, mask_first=False)

    sd = state_C.dtype
    heads = lambda a, nb, t: a.reshape(nb, t, ATT_HEADS, ATT_HEAD_DIM)[None]
    keep_p = min(ATT_REACH, S)
    return (yp, ys.reshape(DB, T, D),
            heads(kp, B, keep_p).astype(cache_k.dtype), heads(vp, B, keep_p).astype(cache_v.dtype),
            cp[None].astype(sd), np_[None].astype(sd), mp.reshape(1, B, H).astype(sd),
            heads(ks, DB, T).astype(cache_k.dtype), heads(vs, DB, T).astype(cache_v.dtype),
            cs[None].astype(sd), ns[None].astype(sd), ms.reshape(1, DB, H).astype(sd))
```

```python
import functools

import jax
import jax.numpy as jnp
from jax import lax
from jax.experimental import pallas as pl
from jax.experimental.pallas import tpu as pltpu

F32 = jnp.float32
BF16 = jnp.bfloat16

D_MODEL = 1024
CHUNK = 64
LEFT_CHUNKS = 8
ATT_REACH = LEFT_CHUNKS * CHUNK
ATT_WIDTH = 512
MLSTM_WIDTH = 512
ATT_HEADS = 8
ATT_HEAD_DIM = 64
MLSTM_HEADS = 4
MLSTM_HEAD_DIM = 128
MAX_REL = 128
D_FF = 4 * D_MODEL
DEPTH = 1
DEEPNORM_ALPHA = (2 * DEPTH) ** 0.25
LN_EPS = 1e-5
NEG_INF = -1e30
ATT_SCALE = ATT_HEAD_DIM ** -0.5
KB_SCALE = MLSTM_HEAD_DIM ** -0.5

LANES = 128
MAIN_WIDTH = 3 * ATT_WIDTH + 4 * MLSTM_WIDTH
GATE_PAD = LANES
MLSTM_CHUNK = 256
MLSTM_STREAMS = 2
GATE_ROWS = 24
HEAD_PAIRS = ATT_HEADS // 2
VMEM_LIMIT = 56 * 1024 * 1024


def _cparams(sem):
    return pltpu.CompilerParams(dimension_semantics=sem, vmem_limit_bytes=VMEM_LIMIT)


def _in_proj_kernel(x_ref, w_ref, b_ref, qkva_ref, qkvm_ref, og_ref, gates_ref, klast_ref, vlast_ref,
                    *, tm, keep, lc):
    i = pl.program_id(1)
    last = pl.num_programs(1) - 1
    xb = x_ref[0].astype(BF16)

    def proj(c0, width):
        return (jnp.dot(xb, w_ref[:, c0:c0 + width], preferred_element_type=F32)
                + b_ref[:, c0:c0 + width])

    W = ATT_WIDTH
    g = proj(MAIN_WIDTH, GATE_PAD)
    log_sig = jnp.minimum(g, 0.0) - jnp.log1p(jnp.exp(-jnp.abs(g)))
    lane = lax.broadcasted_iota(jnp.int32, g.shape, 1)
    H = MLSTM_HEADS
    g8 = jnp.where(lane < H, g, log_sig).T[0:2 * H]
    pos = lax.broadcasted_iota(jnp.int32, g8.shape, 1) % lc
    head_row = lax.broadcasted_iota(jnp.int32, g8.shape, 0) < H
    csum = g8
    sh = 1
    while sh < lc:
        csum = csum + jnp.where(pos >= sh, pltpu.roll(csum, sh, 1), 0.0)
        sh *= 2
    b = pltpu.roll(csum, H, 0)
    a = g8 - b
    cm = a
    sh = 1
    while sh < lc:
        cm = jnp.maximum(cm, jnp.where(pos >= sh, pltpu.roll(cm, sh, 1), -jnp.inf))
        sh *= 2
    rows = [jnp.where(head_row, t, 0.0) for t in (b, a, cm)]
    for c in range(tm // lc):
        for j, t in enumerate(rows):
            gates_ref[c, 0, 8 * j:8 * j + 8, :] = t[:, c * lc:(c + 1) * lc]

    qkvm_ref[0, :, 0:W] = proj(3 * W, W).astype(BF16)
    qkvm_ref[0, :, W:2 * W] = (proj(4 * W, W) * KB_SCALE).astype(BF16)
    qkvm_ref[0, :, 2 * W:3 * W] = proj(5 * W, W).astype(BF16)
    ob = proj(6 * W, W)
    og_ref[0] = 1.0 / (1.0 + jnp.exp(-ob))
    qa = proj(0, W)
    qkva_ref[0, :, 0:W] = (qa * ATT_SCALE).astype(BF16)
    ka = proj(W, W)
    qkva_ref[0, :, W:2 * W] = ka.astype(BF16)
    va = proj(2 * W, W)
    qkva_ref[0, :, 2 * W:3 * W] = va.astype(BF16)

    @pl.when(i == last)
    def _():
        klast_ref[0] = ka[tm - keep:, :]
        vlast_ref[0] = va[tm - keep:, :]


def _in_proj(x, w_cat, b_cat, *, tm, keep, lc):
    B, T, D = x.shape
    nt = T // tm
    W3 = 3 * ATT_WIDTH
    wtot = w_cat.shape[1]
    kern = functools.partial(_in_proj_kernel, tm=tm, keep=keep, lc=lc)
    return pl.pallas_call(
        kern,
        grid=(B, nt),
        in_specs=[
            pl.BlockSpec((1, tm, D), lambda b, i: (b, i, 0)),
            pl.BlockSpec((D, wtot), lambda b, i: (0, 0)),
            pl.BlockSpec((1, wtot), lambda b, i: (0, 0)),
        ],
        out_specs=[
            pl.BlockSpec((1, tm, W3), lambda b, i: (b, i, 0)),
            pl.BlockSpec((1, tm, W3), lambda b, i: (b, i, 0)),
            pl.BlockSpec((1, tm, MLSTM_WIDTH), lambda b, i: (b, i, 0)),
            pl.BlockSpec((tm // lc, 1, GATE_ROWS, lc), lambda b, i: (i, b, 0, 0)),
            pl.BlockSpec((1, keep, ATT_WIDTH), lambda b, i: (b, 0, 0)),
            pl.BlockSpec((1, keep, ATT_WIDTH), lambda b, i: (b, 0, 0)),
        ],
        out_shape=[
            jax.ShapeDtypeStruct((B, T, W3), BF16),
            jax.ShapeDtypeStruct((B, T, W3), BF16),
            jax.ShapeDtypeStruct((B, T, MLSTM_WIDTH), F32),
            jax.ShapeDtypeStruct((T // lc, B, GATE_ROWS, lc), F32),
            jax.ShapeDtypeStruct((B, keep, ATT_WIDTH), F32),
            jax.ShapeDtypeStruct((B, keep, ATT_WIDTH), F32),
        ],
        compiler_params=_cparams(("parallel", "arbitrary")),
        name="in_proj",
    )(x, w_cat, b_cat)


def _bias_table_kernel(g_ref, o_ref, *, rows, keys, cps):
    L = g_ref.shape[-1]
    r = lax.broadcasted_iota(jnp.int32, (rows, keys), 0)
    c = lax.broadcasted_iota(jnp.int32, (rows, keys), 1)
    band_start = (r // CHUNK) * CHUNK
    in_band = (c >= band_start) & (c < band_start + ATT_REACH + CHUNK)
    for h in range(ATT_HEADS):
        g = jnp.broadcast_to(g_ref[h], (rows, L))
        t = pltpu.roll(g, L - rows, 1, stride=1, stride_axis=0)
        tab = jnp.where(in_band, t[:, :keys], NEG_INF)
        o_ref[h // 2, (h % 2) * rows:(h % 2 + 1) * rows, :] = tab


def _bias_table(rel_bias, *, cps, keys):
    rows = cps * CHUNK
    L = ((rows + keys + LANES - 1) // LANES) * LANES
    n_const = rows + ATT_REACH - MAX_REL
    const = jnp.broadcast_to(rel_bias[:, 2 * MAX_REL:], (ATT_HEADS, n_const))
    ramp = rel_bias[:, ::-1]
    tail = jnp.broadcast_to(rel_bias[:, :1], (ATT_HEADS, LANES))
    g = jnp.concatenate([const, ramp, tail], axis=1)[:, :L].reshape(ATT_HEADS, 1, L)
    kern = functools.partial(_bias_table_kernel, rows=rows, keys=keys, cps=cps)
    return pl.pallas_call(
        kern,
        out_shape=jax.ShapeDtypeStruct((HEAD_PAIRS, 2 * rows, keys), F32),
        name="bias_table",
    )(g)


def _attn_kernel(q_ref, kc_ref, vc_ref, kp_ref, vp_ref, bias_ref, o_ref, kk_ref, vv_ref,
                 *, tq, cps, keys, mask_first):
    i = pl.program_id(1)
    rows = cps * CHUNK
    kk_ref[0:ATT_REACH, :] = kp_ref[0].astype(BF16)
    vv_ref[0:ATT_REACH, :] = vp_ref[0].astype(BF16)
    kk_ref[ATT_REACH:ATT_REACH + tq, :] = kc_ref[0]
    vv_ref[ATT_REACH:ATT_REACH + tq, :] = vc_ref[0]
    total = kk_ref.shape[0]
    if total > ATT_REACH + tq:
        kk_ref[ATT_REACH + tq:, :] = jnp.zeros((total - ATT_REACH - tq, ATT_WIDTH), BF16)
        vv_ref[ATT_REACH + tq:, :] = jnp.zeros((total - ATT_REACH - tq, ATT_WIDTH), BF16)

    lane = lax.broadcasted_iota(jnp.int32, (rows, LANES), 1)
    first_head = lane < ATT_HEAD_DIM

    def tile(no_past):
        for sub in range(tq // rows):
            off = sub * rows
            k0 = max(ATT_REACH - off, 0) if no_past else 0
            for pair in range(HEAD_PAIRS):
                ls = slice(pair * LANES, (pair + 1) * LANES)
                q2 = q_ref[0, off:off + rows, ls]
                zero = jnp.zeros_like(q2)
                qs = jnp.concatenate([jnp.where(first_head, q2, zero), jnp.where(first_head, zero, q2)], axis=0)
                s = lax.dot_general(qs, kk_ref[off + k0:off + keys, ls], (((1,), (1,)), ((), ())),
                                    preferred_element_type=F32)
                s = s + bias_ref[pair, :, k0:keys]
                m = jnp.max(s, axis=-1, keepdims=True)
                e = jnp.exp(s - m)
                r = 1.0 / jnp.sum(e, axis=-1, keepdims=True)
                o2 = jnp.dot(e.astype(BF16), vv_ref[off + k0:off + keys, ls],
                             preferred_element_type=F32) * r
                o = jnp.where(first_head, o2[:rows], o2[rows:])
                o_ref[0, off:off + rows, ls] = o.astype(BF16)

    if mask_first:
        pl.when(i == 0)(lambda: tile(True))
        pl.when(i > 0)(lambda: tile(False))
    else:
        tile(False)


def _band_attn(qkva, k_prev, v_prev, prev_map, bias_tab, *, tq, cps, keys, mask_first):
    B, T, _ = qkva.shape
    nt = T // tq
    rows_total = max(ATT_REACH + tq, (tq // (cps * CHUNK) - 1) * cps * CHUNK + keys)
    kern = functools.partial(_attn_kernel, tq=tq, cps=cps, keys=keys, mask_first=mask_first)
    W = ATT_WIDTH
    return pl.pallas_call(
        kern,
        grid=(B, nt),
        in_specs=[
            pl.BlockSpec((1, tq, W), lambda b, i: (b, i, 0)),
            pl.BlockSpec((1, tq, W), lambda b, i: (b, i, 1)),
            pl.BlockSpec((1, tq, W), lambda b, i: (b, i, 2)),
            pl.BlockSpec((1, ATT_REACH, W), prev_map[0]),
            pl.BlockSpec((1, ATT_REACH, W), prev_map[1]),
            pl.BlockSpec(bias_tab.shape, lambda b, i: (0, 0, 0)),
        ],
        out_specs=pl.BlockSpec((1, tq, W), lambda b, i: (b, i, 0)),
        out_shape=jax.ShapeDtypeStruct((B, T, W), BF16),
        scratch_shapes=[pltpu.VMEM((rows_total, W), BF16), pltpu.VMEM((rows_total, W), BF16)],
        compiler_params=_cparams(("parallel", "arbitrary")),
        name="band_attn",
    )(qkva, qkva, qkva, k_prev, v_prev, bias_tab)


def _old_mlstm_kernel(q_ref, k_ref, v_ref, gates_ref, og_ref, ng_ref, c0_ref, n0_ref, m0_ref,
                  mb_ref, c_out_ref, n_out_ref, m_out_ref, c_s, n_s, m_s, *, tb):
    i = pl.program_id(1)
    last = pl.num_programs(1) - 1
    H, d = MLSTM_HEADS, MLSTM_HEAD_DIM

    @pl.when(i == 0)
    def _():
        c_s[...] = c0_ref[0]
        for h in range(H):
            n_s[h] = jnp.broadcast_to(n0_ref[0, h:h + 1, :], (8, d))
        m_s[...] = m0_ref[0]

    g = gates_ref[0]
    pos = lax.broadcasted_iota(jnp.int32, g.shape, 0) % CHUNK
    lane = lax.broadcasted_iota(jnp.int32, g.shape, 1)
    acc = g
    sh = 1
    while sh < CHUNK:
        shifted = pltpu.roll(acc, sh, 0)
        acc = acc + jnp.where((pos >= sh) & (lane >= H), shifted, 0.0)
        sh *= 2
    gt = acc.T

    ti = lax.broadcasted_iota(jnp.int32, (CHUNK, CHUNK), 0)
    si = lax.broadcasted_iota(jnp.int32, (CHUNK, CHUNK), 1)
    causal = ti >= si

    for c in range(tb // CHUNK):
        r0 = c * CHUNK
        rs = slice(r0, r0 + CHUNK)
        for h in range(H):
            hs = slice(h * d, (h + 1) * d)
            q = q_ref[0, rs, hs]
            k = k_ref[0, rs, hs]
            v = v_ref[0, rs, hs]
            bcol = acc[rs, H + h:H + h + 1]
            igcol = acc[rs, h:h + 1]
            brow = gt[H + h:H + h + 1, rs]
            igrow = gt[h:h + 1, rs]
            m_prev = m_s[0:1, h:h + 1]

            logw = jnp.where(causal, bcol - brow + igrow, -jnp.inf)
            inter = bcol + m_prev
            m_t = jnp.maximum(inter, jnp.max(logw, axis=-1, keepdims=True))
            qk = lax.dot_general(q, k, (((1,), (1,)), ((), ())), preferred_element_type=F32)
            w = jnp.exp(logw - m_t) * qk
            gdec = jnp.exp(inter - m_t)
            c_prev = c_s[h]
            n_prev = n_s[h]
            qc = lax.dot_general(q, c_prev.astype(BF16), (((1,), (1,)), ((), ())),
                                 preferred_element_type=F32)
            qn = lax.dot_general(q, n_prev.astype(BF16), (((1,), (1,)), ((), ())),
                                 preferred_element_type=F32)[:, 0:1]
            num = gdec * qc + jnp.dot(w.astype(BF16), v, preferred_element_type=F32)
            den = gdec * qn + jnp.sum(w, axis=-1, keepdims=True)
            hh = num / jnp.maximum(jnp.abs(den), jnp.exp(-m_t))

            b_last = bcol[CHUNK - 1:CHUNK, :]
            logw_end_row = b_last - brow + igrow
            m_new = jnp.maximum(b_last + m_prev, jnp.max(logw_end_row, axis=-1, keepdims=True))
            decay = jnp.exp(b_last + m_prev - m_new)
            w_end_col = jnp.exp(b_last - bcol + igcol - m_new)
            w_end_row = jnp.exp(logw_end_row - m_new)
            vw = (v.astype(F32) * w_end_col).astype(BF16)
            upd = lax.dot_general(vw, k, (((0,), (0,)), ((), ())), preferred_element_type=F32)
            c_s[h] = decay * c_prev + upd
            wr8 = jnp.broadcast_to(w_end_row, (8, CHUNK)).astype(BF16)
            n_s[h] = decay * n_prev + jnp.dot(wr8, k, preferred_element_type=F32)
            m_s[0:1, h:h + 1] = m_new

            mu = jnp.mean(hh, axis=-1, keepdims=True)
            xc = hh - mu
            var = jnp.mean(xc * xc, axis=-1, keepdims=True)
            hn = xc * lax.rsqrt(var + LN_EPS) * ng_ref[:, hs]
            mb_ref[0, rs, hs] = (og_ref[0, rs, hs] * hn).astype(BF16)

    @pl.when(i == last)
    def _():
        c_out_ref[0] = c_s[...]
        for h in range(H):
            n_out_ref[0, h:h + 1, :] = n_s[h, 0:1, :]
        m_out_ref[0] = m_s[...]


def _old_mlstm(qkvm, gates, og, norm_g, c0, n0, m0, *, tb):
    B, T, _ = qkvm.shape
    nt = T // tb
    W = MLSTM_WIDTH
    H, d = MLSTM_HEADS, MLSTM_HEAD_DIM
    kern = functools.partial(_old_mlstm_kernel, tb=tb)
    return pl.pallas_call(
        kern,
        grid=(B, nt),
        in_specs=[
            pl.BlockSpec((1, tb, W), lambda b, i: (b, i, 0)),
            pl.BlockSpec((1, tb, W), lambda b, i: (b, i, 1)),
            pl.BlockSpec((1, tb, W), lambda b, i: (b, i, 2)),
            pl.BlockSpec((1, tb, GATE_PAD), lambda b, i: (b, i, 0)),
            pl.BlockSpec((1, tb, W), lambda b, i: (b, i, 0)),
            pl.BlockSpec((1, W), lambda b, i: (0, 0)),
            pl.BlockSpec((1, H, d, d), lambda b, i: (b, 0, 0, 0)),
            pl.BlockSpec((1, H, d), lambda b, i: (b, 0, 0)),
            pl.BlockSpec((1, 1, H), lambda b, i: (b, 0, 0)),
        ],
        out_specs=[
            pl.BlockSpec((1, tb, W), lambda b, i: (b, i, 0)),
            pl.BlockSpec((1, H, d, d), lambda b, i: (b, 0, 0, 0)),
            pl.BlockSpec((1, H, d), lambda b, i: (b, 0, 0)),
            pl.BlockSpec((1, 1, H), lambda b, i: (b, 0, 0)),
        ],
        out_shape=[
            jax.ShapeDtypeStruct((B, T, W), BF16),
            jax.ShapeDtypeStruct((B, H, d, d), F32),
            jax.ShapeDtypeStruct((B, H, d), F32),
            jax.ShapeDtypeStruct((B, 1, H), F32),
        ],
        scratch_shapes=[
            pltpu.VMEM((H, d, d), F32),
            pltpu.VMEM((H, 8, d), F32),
            pltpu.VMEM((1, H), F32),
        ],
        compiler_params=_cparams(("parallel", "arbitrary")),
        name="mlstm",
    )(qkvm, qkvm, qkvm, gates, og, norm_g, c0, n0, m0)


def _mlstm_kernel(q_ref, k_ref, v_ref, gates_ref, og_ref, ng_ref, c0_ref, n0_ref, m0_ref,
                  mb_ref, c_out_ref, n_out_ref, m_out_ref, c_s, n_s, m_s, *, lc, bb):
    i = pl.program_id(1)
    last = pl.num_programs(1) - 1
    H, d = MLSTM_HEADS, MLSTM_HEAD_DIM
    NT = (((1,), (1,)), ((), ()))

    @pl.when(i == 0)
    def _():
        m_s[...] = jnp.zeros(m_s.shape, F32)
        for bi in range(bb):
            c_s[bi * H:(bi + 1) * H] = c0_ref[bi]
            for h in range(H):
                n_s[bi * H + h] = jnp.broadcast_to(n0_ref[bi, h:h + 1, :], (8, d))
            m_s[bi, 0:H, :] = jnp.broadcast_to(m0_ref[bi], (H, LANES))

    ti = lax.broadcasted_iota(jnp.int32, (lc, lc), 0)
    si = lax.broadcasted_iota(jnp.int32, (lc, lc), 1)
    causal = ti >= si

    for bi in range(bb):
        b = gates_ref[0, bi, 0:8, :]
        a = gates_ref[0, bi, 8:16, :]
        cm = gates_ref[0, bi, 16:24, :]
        m_prev = m_s[bi, :, 0:1]
        inter = b + m_prev
        m_t = jnp.maximum(inter, b + cm)
        b_last = b[:, lc - 1:lc]
        m_new = jnp.maximum(b_last + m_prev, b_last + cm[:, lc - 1:lc])
        decay = jnp.exp(b_last + m_prev - m_new)
        w_end = jnp.exp(b_last + a - m_new)
        stacked = jnp.concatenate(
            [b - m_t, jnp.exp(inter - m_t), jnp.exp(-m_t), w_end, jnp.zeros((LANES - 32, lc), F32)], axis=0)
        cols = stacked.T
        m_s[bi] = jnp.broadcast_to(m_new, (8, LANES))

        for h in range(H):
            st = bi * H + h
            hs = slice(h * d, (h + 1) * d)
            q = q_ref[bi, :, hs]
            k = k_ref[bi, :, hs]
            v = v_ref[bi, :, hs]
            ct_col = cols[:, h:h + 1]
            gdec = cols[:, 8 + h:9 + h]
            emt = cols[:, 16 + h:17 + h]
            w_end_col = cols[:, 24 + h:25 + h]

            qk = lax.dot_general(q, k, NT, preferred_element_type=F32)
            w = jnp.exp(jnp.where(causal, ct_col + a[h:h + 1, :], -jnp.inf)) * qk
            c_prev = c_s[st]
            n_prev = n_s[st]
            cn = jnp.concatenate([c_prev.astype(BF16), n_prev.astype(BF16)], axis=0)
            qcn = lax.dot_general(q, cn, NT, preferred_element_type=F32)
            num = gdec * qcn[:, 0:d] + jnp.dot(w.astype(BF16), v, preferred_element_type=F32)
            den = gdec * qcn[:, d:d + 1] + jnp.sum(w, axis=-1, keepdims=True)
            hh = num * (1.0 / jnp.maximum(jnp.abs(den), emt))

            vw = (v.astype(F32) * w_end_col).astype(BF16)
            upd = lax.dot_general(vw, k, (((0,), (0,)), ((), ())), preferred_element_type=F32)
            dec_h = decay[h:h + 1, :]
            c_s[st] = dec_h * c_prev + upd
            wr8 = jnp.broadcast_to(w_end[h:h + 1, :], (8, lc)).astype(BF16)
            n_s[st] = dec_h * n_prev + jnp.dot(wr8, k, preferred_element_type=F32)

            mu = jnp.mean(hh, axis=-1, keepdims=True)
            xc = hh - mu
            var = jnp.mean(xc * xc, axis=-1, keepdims=True)
            hn = xc * lax.rsqrt(var + LN_EPS) * ng_ref[:, hs]
            mb_ref[bi, :, hs] = (og_ref[bi, :, hs] * hn).astype(BF16)

    @pl.when(i == last)
    def _():
        for bi in range(bb):
            c_out_ref[bi] = c_s[bi * H:(bi + 1) * H]
            for h in range(H):
                n_out_ref[bi, h:h + 1, :] = n_s[bi * H + h, 0:1, :]
            m_out_ref[bi] = m_s[bi, 0:H, 0:1]


def _mlstm(qkvm, gates, og, norm_g, c0, n0, m0, *, lc, bb):
    B, T, _ = qkvm.shape
    nt = T // lc
    W = MLSTM_WIDTH
    H, d = MLSTM_HEADS, MLSTM_HEAD_DIM
    kern = functools.partial(_mlstm_kernel, lc=lc, bb=bb)
    return pl.pallas_call(
        kern,
        grid=(B // bb, nt),
        in_specs=[
            pl.BlockSpec((bb, lc, W), lambda b, i: (b, i, 0)),
            pl.BlockSpec((bb, lc, W), lambda b, i: (b, i, 1)),
            pl.BlockSpec((bb, lc, W), lambda b, i: (b, i, 2)),
            pl.BlockSpec((1, bb, GATE_ROWS, lc), lambda b, i: (i, b, 0, 0)),
            pl.BlockSpec((bb, lc, W), lambda b, i: (b, i, 0)),
            pl.BlockSpec((1, W), lambda b, i: (0, 0)),
            pl.BlockSpec((bb, H, d, d), lambda b, i: (b, 0, 0, 0)),
            pl.BlockSpec((bb, H, d), lambda b, i: (b, 0, 0)),
            pl.BlockSpec((bb, H, 1), lambda b, i: (b, 0, 0)),
        ],
        out_specs=[
            pl.BlockSpec((bb, lc, W), lambda b, i: (b, i, 0)),
            pl.BlockSpec((bb, H, d, d), lambda b, i: (b, 0, 0, 0)),
            pl.BlockSpec((bb, H, d), lambda b, i: (b, 0, 0)),
            pl.BlockSpec((bb, H, 1), lambda b, i: (b, 0, 0)),
        ],
        out_shape=[
            jax.ShapeDtypeStruct((B, T, W), BF16),
            jax.ShapeDtypeStruct((B, H, d, d), F32),
            jax.ShapeDtypeStruct((B, H, d), F32),
            jax.ShapeDtypeStruct((B, H, 1), F32),
        ],
        scratch_shapes=[
            pltpu.VMEM((bb * H, d, d), F32),
            pltpu.VMEM((bb * H, 8, d), F32),
            pltpu.VMEM((bb, 8, LANES), F32),
        ],
        compiler_params=_cparams(("parallel", "arbitrary")),
        name="mlstm",
    )(qkvm, qkvm, qkvm, gates, og, norm_g, c0, n0, m0)


def _layer_norm(z, g, b):
    mu = jnp.mean(z, axis=-1, keepdims=True)
    zc = z - mu
    var = jnp.mean(zc * zc, axis=-1, keepdims=True)
    return zc * lax.rsqrt(var + LN_EPS) * g + b


def _merge_ffn_kernel(x_ref, att_ref, mb_ref, wo_ref, g1_ref, b1_ref, w1_ref, bf1_ref, w2_ref, bf2_ref,
                      g2_ref, b2_ref, y_ref, *, fc):
    W = ATT_WIDTH
    mix = (jnp.dot(att_ref[0], wo_ref[0:W, :], preferred_element_type=F32)
           + jnp.dot(mb_ref[0], wo_ref[W:2 * W, :], preferred_element_type=F32))
    h = _layer_norm(DEEPNORM_ALPHA * x_ref[0] + mix, g1_ref[...], b1_ref[...])
    hb = h.astype(BF16)
    f = jnp.zeros(h.shape, F32)
    for j in range(D_FF // fc):
        a = jnp.dot(hb, w1_ref[:, j * fc:(j + 1) * fc], preferred_element_type=F32) + bf1_ref[:, j * fc:(j + 1) * fc]
        a = jnp.square(jnp.maximum(a, 0.0)).astype(BF16)
        f = f + jnp.dot(a, w2_ref[j * fc:(j + 1) * fc, :], preferred_element_type=F32)
    f = f + bf2_ref[...]
    y_ref[0] = _layer_norm(DEEPNORM_ALPHA * h + f, g2_ref[...], b2_ref[...])


def _merge_ffn(x, att, mb, wo, g1, b1, w1, bf1, w2, bf2, g2, b2, *, tm, fc):
    B, T, D = x.shape
    nt = T // tm
    W = ATT_WIDTH
    const = lambda shape: pl.BlockSpec(shape, lambda b, i: (0, 0), pipeline_mode=pl.Buffered(1))
    kern = functools.partial(_merge_ffn_kernel, fc=fc)
    return pl.pallas_call(
        kern,
        grid=(B, nt),
        in_specs=[
            pl.BlockSpec((1, tm, D), lambda b, i: (b, i, 0)),
            pl.BlockSpec((1, tm, W), lambda b, i: (b, i, 0)),
            pl.BlockSpec((1, tm, W), lambda b, i: (b, i, 0)),
            const((D, D)), const((1, D)), const((1, D)),
            const((D, D_FF)), const((1, D_FF)), const((D_FF, D)), const((1, D)),
            const((1, D)), const((1, D)),
        ],
        out_specs=pl.BlockSpec((1, tm, D), lambda b, i: (b, i, 0)),
        out_shape=jax.ShapeDtypeStruct((B, T, D), F32),
        compiler_params=_cparams(("parallel", "parallel")),
        name="merge_ffn",
    )(x, att, mb, wo, g1, b1, w1, bf1, w2, bf2, g2, b2)


def _token_tile(t):
    return 512 if t % 512 == 0 else t


def kernel(x_prompt, x_sample, cache_k, cache_v, state_C, state_n, state_m, w_in, b_in, rel_bias,
           mlstm_norm_g, w_out, ln1_g, ln1_b, w_ff1, b_ff1, w_ff2, b_ff2, ln2_g, ln2_b):
    assert w_in.shape[0] == DEPTH == 1
    B, S, D = x_prompt.shape
    DB, T, _ = x_sample.shape
    H, d = MLSTM_HEADS, MLSTM_HEAD_DIM
    assert S % ATT_REACH == 0 and T == CHUNK and cache_k.shape[2] == ATT_REACH

    n_gate = 2 * H
    w_cat = jnp.concatenate(
        [w_in[0, :, :MAIN_WIDTH], jnp.pad(w_in[0, :, MAIN_WIDTH:], ((0, 0), (0, GATE_PAD - n_gate)))],
        axis=1).astype(BF16)
    b_cat = jnp.concatenate(
        [b_in[0, :MAIN_WIDTH], jnp.pad(b_in[0, MAIN_WIDTH:], (0, GATE_PAD - n_gate))])[None, :]
    wo = w_out[0].astype(BF16)
    w1 = w_ff1[0].astype(BF16)
    w2 = w_ff2[0].astype(BF16)
    row = lambda p: p[0][None, :]
    ffn_params = (wo, row(ln1_g), row(ln1_b), w1, row(b_ff1), w2, row(b_ff2), row(ln2_g), row(ln2_b))
    norm_g = row(mlstm_norm_g)

    def layer(x, k_prev, v_prev, prev_map, c0, n0, m0, *, seqs, cps, mask_first):
        Bx, Tx, _ = x.shape
        n_tok = Bx * Tx
        t_seq = n_tok // seqs
        tm = _token_tile(Tx)
        keep = min(ATT_REACH, Tx)
        lc = min(MLSTM_CHUNK, t_seq)
        qkva, qkvm, og, gates, k_last, v_last = _in_proj(x, w_cat, b_cat, tm=tm, keep=keep, lc=lc)
        seq = lambda a: a.reshape(seqs, t_seq, a.shape[-1])
        tq = _token_tile(t_seq)
        keys = -(-(ATT_REACH + cps * CHUNK) // LANES) * LANES
        bias_tab = _bias_table(rel_bias[0], cps=cps, keys=keys)
        if k_prev is None:
            k_prev = v_prev = seq(qkva)
        att = _band_attn(seq(qkva), k_prev, v_prev, prev_map, bias_tab,
                         tq=tq, cps=cps, keys=keys, mask_first=mask_first)
        gates = gates.reshape(t_seq // lc, seqs, GATE_ROWS, lc)
        mb, c_new, n_new, m_new = _mlstm(seq(qkvm), gates, seq(og), norm_g, c0, n0, m0,
                                         lc=lc, bb=min(seqs, MLSTM_STREAMS))
        tok = lambda a: a.reshape(Bx, Tx, a.shape[-1])
        y = _merge_ffn(x, tok(att), tok(mb), *ffn_params, tm=tm, fc=1024)
        return y, k_last, v_last, c_new, n_new, m_new

    zeros_c = jnp.zeros((B, H, d, d), F32)
    zeros_n = jnp.zeros((B, H, d), F32)
    zeros_m = jnp.zeros((B, H, 1), F32)
    prev_prompt = (lambda b, i: (b, jnp.maximum(i - 1, 0), 1), lambda b, i: (b, jnp.maximum(i - 1, 0), 2))
    yp, kp, vp, cp, np_, mp = layer(x_prompt, None, None, prev_prompt, zeros_c, zeros_n, zeros_m,
                                    seqs=B, cps=2, mask_first=True)

    ck = cache_k[0].reshape(DB, ATT_REACH, ATT_WIDTH)
    cv = cache_v[0].reshape(DB, ATT_REACH, ATT_WIDTH)
    prev_sample = (lambda b, i: (b, 0, 0), lambda b, i: (b, 0, 0))
    ys, ks, vs, cs, ns, ms = layer(
        x_sample.reshape(1, DB * T, D), ck, cv, prev_sample,
        state_C[0].astype(F32), state_n[0].astype(F32), state_m[0].astype(F32).reshape(DB, H, 1),
        seqs=DB, cps=1, mask_first=False)

    sd = state_C.dtype
    heads = lambda a, nb, t: a.reshape(nb, t, ATT_HEADS, ATT_HEAD_DIM)[None]
    keep_p = min(ATT_REACH, S)
    return (yp, ys.reshape(DB, T, D),
            heads(kp, B, keep_p).astype(cache_k.dtype), heads(vp, B, keep_p).astype(cache_v.dtype),
            cp[None].astype(sd), np_[None].astype(sd), mp.reshape(1, B, H).astype(sd),
            heads(ks, DB, T).astype(cache_k.dtype), heads(vs, DB, T).astype(cache_v.dtype),
            cs[None].astype(sd), ns[None].astype(sd), ms.reshape(1, DB, H).astype(sd))
```

```python
import functools

import jax
import jax.numpy as jnp
from jax import lax
from jax.experimental import pallas as pl
from jax.experimental.pallas import tpu as pltpu

F32 = jnp.float32
BF16 = jnp.bfloat16

D_MODEL = 1024
CHUNK = 64
LEFT_CHUNKS = 8
ATT_REACH = LEFT_CHUNKS * CHUNK
ATT_WIDTH = 512
MLSTM_WIDTH = 512
ATT_HEADS = 8
ATT_HEAD_DIM = 64
MLSTM_HEADS = 4
MLSTM_HEAD_DIM = 128
MAX_REL = 128
D_FF = 4 * D_MODEL
DEPTH = 1
DEEPNORM_ALPHA = (2 * DEPTH) ** 0.25
LN_EPS = 1e-5
NEG_INF = -1e30
ATT_SCALE = ATT_HEAD_DIM ** -0.5
KB_SCALE = MLSTM_HEAD_DIM ** -0.5

OFF_KB = 3 * ATT_WIDTH + MLSTM_WIDTH
OFF_VB = OFF_KB + MLSTM_WIDTH
OFF_GATES = 3 * ATT_WIDTH + 4 * MLSTM_WIDTH

LANES = 128
GATE_PAD = LANES
COL_QB = 3 * ATT_WIDTH
COL_VB = COL_QB + MLSTM_WIDTH
COL_OB = COL_VB + MLSTM_WIDTH
COL_GATES = COL_OB + MLSTM_WIDTH
MLSTM_CHUNK = 256
MLSTM_STREAMS = 2
GATE_ROWS = 24
FFN_TILE = 1024
FFN_GROUP = 256
HEAD_PAIRS = ATT_HEADS // 2
VMEM_LIMIT = 56 * 1024 * 1024
NT_DIMS = (((1,), (1,)), ((), ()))


def _cparams(sem):
    return pltpu.CompilerParams(dimension_semantics=sem, vmem_limit_bytes=VMEM_LIMIT)


def _in_proj_kernel(x_ref, w_ref, b_ref, wkt_ref, bkt_ref,
                    qkva_ref, qvm_ref, kt_ref, og_ref, gates_ref, klast_ref, vlast_ref, *, tm, keep, lc):
    i = pl.program_id(1)
    last = pl.num_programs(1) - 1
    xb = x_ref[0].astype(BF16)

    def proj(c0, width):
        return (jnp.dot(xb, w_ref[:, c0:c0 + width], preferred_element_type=F32)
                + b_ref[:, c0:c0 + width])

    W = ATT_WIDTH
    g = proj(COL_GATES, GATE_PAD)
    log_sig = jnp.minimum(g, 0.0) - jnp.log1p(jnp.exp(-jnp.abs(g)))
    lane = lax.broadcasted_iota(jnp.int32, g.shape, 1)
    H = MLSTM_HEADS
    g8 = jnp.where(lane < H, g, log_sig).T[0:2 * H]
    pos = lax.broadcasted_iota(jnp.int32, g8.shape, 1) % lc
    head_row = lax.broadcasted_iota(jnp.int32, g8.shape, 0) < H
    csum = g8
    sh = 1
    while sh < lc:
        csum = csum + jnp.where(pos >= sh, pltpu.roll(csum, sh, 1), 0.0)
        sh *= 2
    b = pltpu.roll(csum, H, 0)
    a = g8 - b
    cm = a
    sh = 1
    while sh < lc:
        cm = jnp.maximum(cm, jnp.where(pos >= sh, pltpu.roll(cm, sh, 1), -jnp.inf))
        sh *= 2
    rows = [jnp.where(head_row, t, 0.0) for t in (b, a, cm)]
    for c in range(tm // lc):
        for j, t in enumerate(rows):
            gates_ref[c, 0, 8 * j:8 * j + 8, :] = t[:, c * lc:(c + 1) * lc]

    kt = lax.dot_general(wkt_ref[...], xb, NT_DIMS, preferred_element_type=F32) + bkt_ref[...]
    kt_ref[0] = (kt * KB_SCALE).astype(BF16)
    qvm_ref[0, :, 0:W] = proj(COL_QB, W).astype(BF16)
    qvm_ref[0, :, W:2 * W] = proj(COL_VB, W).astype(BF16)
    ob = proj(COL_OB, W)
    og_ref[0] = 1.0 / (1.0 + jnp.exp(-ob))
    qa = proj(0, W)
    qkva_ref[0, :, 0:W] = (qa * ATT_SCALE).astype(BF16)
    ka = proj(W, W)
    qkva_ref[0, :, W:2 * W] = ka.astype(BF16)
    va = proj(2 * W, W)
    qkva_ref[0, :, 2 * W:3 * W] = va.astype(BF16)

    @pl.when(i == last)
    def _():
        klast_ref[0] = ka[tm - keep:, :]
        vlast_ref[0] = va[tm - keep:, :]


def _in_proj(x, w_cat, b_cat, wkt, bkt, *, tm, keep, lc):
    B, T, D = x.shape
    nt = T // tm
    W = ATT_WIDTH
    wtot = w_cat.shape[1]
    kern = functools.partial(_in_proj_kernel, tm=tm, keep=keep, lc=lc)
    return pl.pallas_call(
        kern,
        grid=(B, nt),
        in_specs=[
            pl.BlockSpec((1, tm, D), lambda b, i: (b, i, 0)),
            pl.BlockSpec((D, wtot), lambda b, i: (0, 0)),
            pl.BlockSpec((1, wtot), lambda b, i: (0, 0)),
            pl.BlockSpec((W, D), lambda b, i: (0, 0)),
            pl.BlockSpec((W, 1), lambda b, i: (0, 0)),
        ],
        out_specs=[
            pl.BlockSpec((1, tm, 3 * W), lambda b, i: (b, i, 0)),
            pl.BlockSpec((1, tm, 2 * W), lambda b, i: (b, i, 0)),
            pl.BlockSpec((1, W, tm), lambda b, i: (b, 0, i)),
            pl.BlockSpec((1, tm, W), lambda b, i: (b, i, 0)),
            pl.BlockSpec((tm // lc, 1, GATE_ROWS, lc), lambda b, i: (i, b, 0, 0)),
            pl.BlockSpec((1, keep, W), lambda b, i: (b, 0, 0)),
            pl.BlockSpec((1, keep, W), lambda b, i: (b, 0, 0)),
        ],
        out_shape=[
            jax.ShapeDtypeStruct((B, T, 3 * W), BF16),
            jax.ShapeDtypeStruct((B, T, 2 * W), BF16),
            jax.ShapeDtypeStruct((B, W, T), BF16),
            jax.ShapeDtypeStruct((B, T, W), F32),
            jax.ShapeDtypeStruct((T // lc, B, GATE_ROWS, lc), F32),
            jax.ShapeDtypeStruct((B, keep, W), F32),
            jax.ShapeDtypeStruct((B, keep, W), F32),
        ],
        compiler_params=_cparams(("parallel", "arbitrary")),
        name="in_proj",
    )(x, w_cat, b_cat, wkt, bkt)


def _bias_table_kernel(g_ref, o_ref, *, rows, keys):
    L = g_ref.shape[-1]
    r = lax.broadcasted_iota(jnp.int32, (rows, keys), 0)
    c = lax.broadcasted_iota(jnp.int32, (rows, keys), 1)
    band_start = (r // CHUNK) * CHUNK
    in_band = (c >= band_start) & (c < band_start + ATT_REACH + CHUNK)
    for h in range(ATT_HEADS):
        g = jnp.broadcast_to(g_ref[h], (rows, L))
        t = pltpu.roll(g, L - rows, 1, stride=1, stride_axis=0)
        tab = jnp.where(in_band, t[:, :keys], NEG_INF)
        o_ref[h // 2, (h % 2) * rows:(h % 2 + 1) * rows, :] = tab


def _bias_table(rel_bias, *, cps, keys):
    rows = cps * CHUNK
    L = ((rows + keys + LANES - 1) // LANES) * LANES
    n_const = rows + ATT_REACH - MAX_REL
    const = jnp.broadcast_to(rel_bias[:, 2 * MAX_REL:], (ATT_HEADS, n_const))
    ramp = rel_bias[:, ::-1]
    tail = jnp.broadcast_to(rel_bias[:, :1], (ATT_HEADS, LANES))
    g = jnp.concatenate([const, ramp, tail], axis=1)[:, :L].reshape(ATT_HEADS, 1, L)
    kern = functools.partial(_bias_table_kernel, rows=rows, keys=keys)
    return pl.pallas_call(
        kern,
        out_shape=jax.ShapeDtypeStruct((HEAD_PAIRS, 2 * rows, keys), F32),
        name="bias_table",
    )(g)


def _attn_kernel(q_ref, kc_ref, vc_ref, kp_ref, vp_ref, bias_ref, o_ref, kk_ref, vv_ref,
                 *, tq, cps, keys, mask_first):
    i = pl.program_id(1)
    rows = cps * CHUNK
    kk_ref[0:ATT_REACH, :] = kp_ref[0].astype(BF16)
    vv_ref[0:ATT_REACH, :] = vp_ref[0].astype(BF16)
    kk_ref[ATT_REACH:ATT_REACH + tq, :] = kc_ref[0]
    vv_ref[ATT_REACH:ATT_REACH + tq, :] = vc_ref[0]
    total = kk_ref.shape[0]
    if total > ATT_REACH + tq:
        kk_ref[ATT_REACH + tq:, :] = jnp.zeros((total - ATT_REACH - tq, ATT_WIDTH), BF16)
        vv_ref[ATT_REACH + tq:, :] = jnp.zeros((total - ATT_REACH - tq, ATT_WIDTH), BF16)

    lane = lax.broadcasted_iota(jnp.int32, (rows, LANES), 1)
    first_head = lane < ATT_HEAD_DIM

    def tile(no_past):
        for sub in range(tq // rows):
            off = sub * rows
            first_valid = max(ATT_REACH - off, 0) if no_past else 0
            k0 = first_valid // LANES * LANES
            ones_cols = jnp.ones((keys - k0, LANES), BF16)
            for pair in range(HEAD_PAIRS):
                ls = slice(pair * LANES, (pair + 1) * LANES)
                q2 = q_ref[0, off:off + rows, ls]
                zero = jnp.zeros_like(q2)
                qs = jnp.concatenate([jnp.where(first_head, q2, zero), jnp.where(first_head, zero, q2)], axis=0)
                s = lax.dot_general(qs, kk_ref[off + k0:off + keys, ls], NT_DIMS,
                                    preferred_element_type=F32)
                s = s + bias_ref[pair, :, k0:keys]
                if first_valid > k0:
                    col = lax.broadcasted_iota(jnp.int32, s.shape, 1)
                    s = jnp.where(col >= first_valid - k0, s, NEG_INF)
                m = jnp.max(s, axis=-1, keepdims=True)
                e = jnp.exp(s - m).astype(BF16)
                v_ext = jnp.concatenate([vv_ref[off + k0:off + keys, ls], ones_cols], axis=1)
                o2 = jnp.dot(e, v_ext, preferred_element_type=F32)
                o2 = o2[:, 0:LANES] * (1.0 / o2[:, LANES:2 * LANES])
                o = jnp.where(first_head, o2[:rows], o2[rows:])
                o_ref[0, off:off + rows, ls] = o.astype(BF16)

    if mask_first:
        pl.when(i == 0)(lambda: tile(True))
        pl.when(i > 0)(lambda: tile(False))
    else:
        tile(False)


def _band_attn(qkva, k_prev, v_prev, prev_map, bias_tab, *, tq, cps, keys, mask_first):
    B, T, _ = qkva.shape
    nt = T // tq
    rows_total = max(ATT_REACH + tq, (tq // (cps * CHUNK) - 1) * cps * CHUNK + keys)
    kern = functools.partial(_attn_kernel, tq=tq, cps=cps, keys=keys, mask_first=mask_first)
    W = ATT_WIDTH
    return pl.pallas_call(
        kern,
        grid=(B, nt),
        in_specs=[
            pl.BlockSpec((1, tq, W), lambda b, i: (b, i, 0)),
            pl.BlockSpec((1, tq, W), lambda b, i: (b, i, 1)),
            pl.BlockSpec((1, tq, W), lambda b, i: (b, i, 2)),
            pl.BlockSpec((1, ATT_REACH, W), prev_map[0]),
            pl.BlockSpec((1, ATT_REACH, W), prev_map[1]),
            pl.BlockSpec(bias_tab.shape, lambda b, i: (0, 0, 0)),
        ],
        out_specs=pl.BlockSpec((1, tq, W), lambda b, i: (b, i, 0)),
        out_shape=jax.ShapeDtypeStruct((B, T, W), BF16),
        scratch_shapes=[pltpu.VMEM((rows_total, W), BF16), pltpu.VMEM((rows_total, W), BF16)],
        compiler_params=_cparams(("parallel", "arbitrary")),
        name="band_attn",
    )(qkva, qkva, qkva, k_prev, v_prev, bias_tab)


def _lane_mean(x, mean_w):
    hi = x.astype(BF16)
    lo = (x - hi.astype(F32)).astype(BF16)
    return jnp.dot(jnp.concatenate([hi, lo], axis=1), mean_w, preferred_element_type=F32)


def _mlstm_kernel(q_ref, kt_ref, v_ref, gates_ref, og_ref, ng_ref, c0_ref, n0_ref, m0_ref,
                  mb_ref, c_out_ref, n_out_ref, m_out_ref, cn_s, m_s, *, lc, bb):
    i = pl.program_id(1)
    last = pl.num_programs(1) - 1
    H, d = MLSTM_HEADS, MLSTM_HEAD_DIM

    @pl.when(i == 0)
    def _():
        m_s[...] = jnp.zeros(m_s.shape, F32)
        for bi in range(bb):
            for h in range(H):
                cn_s[bi * H + h, :, 0:d] = c0_ref[bi, h].T
                cn_s[bi * H + h, :, d:2 * d] = jnp.broadcast_to(n0_ref[bi, h:h + 1, :], (d, d)).T
            m_s[bi, 0:H, :] = jnp.broadcast_to(m0_ref[bi], (H, LANES))

    ti = lax.broadcasted_iota(jnp.int32, (lc, lc), 0)
    si = lax.broadcasted_iota(jnp.int32, (lc, lc), 1)
    causal = ti >= si
    ones_cols = jnp.ones((lc, d), BF16)
    mean_w = jnp.full((2 * d, d), 1.0 / d, BF16)

    a_rows, decays, col_sets = [], [], []
    for bi in range(bb):
        b = gates_ref[0, bi, 0:8, :]
        a = gates_ref[0, bi, 8:16, :]
        cm = gates_ref[0, bi, 16:24, :]
        m_prev = m_s[bi, :, 0:1]
        inter = b + m_prev
        m_t = jnp.maximum(inter, b + cm)
        b_last = b[:, lc - 1:lc]
        m_new = jnp.maximum(b_last + m_prev, b_last + cm[:, lc - 1:lc])
        w_end = jnp.exp(b_last + a - m_new)
        stacked = jnp.concatenate(
            [b - m_t, jnp.exp(inter - m_t), jnp.exp(-m_t), w_end, jnp.zeros((LANES - 32, lc), F32)], axis=0)
        col_sets.append(stacked.T)
        a_rows.append(a)
        decays.append(jnp.exp(b_last + m_prev - m_new))
        m_s[bi] = jnp.broadcast_to(m_new, (8, LANES))

    inst = [(bi, h) for bi in range(bb) for h in range(H)]
    hsl = lambda h: slice(h * d, (h + 1) * d)
    col = lambda bi, h, j: col_sets[bi][:, 8 * j + h:8 * j + h + 1]
    qs = [q_ref[bi, :, hsl(h)] for bi, h in inst]
    kts = [kt_ref[bi, hsl(h), :] for bi, h in inst]
    vs = [v_ref[bi, :, hsl(h)] for bi, h in inst]
    cn_prev = [cn_s[bi * H + h] for bi, h in inst]

    qk = [jnp.dot(q, kt, preferred_element_type=F32) for q, kt in zip(qs, kts)]
    qcn = [jnp.dot(q, cn.astype(BF16), preferred_element_type=F32) for q, cn in zip(qs, cn_prev)]
    w = [jnp.exp(jnp.where(causal, col(bi, h, 0) + a_rows[bi][h:h + 1, :], -jnp.inf)) * qk_i
         for (bi, h), qk_i in zip(inst, qk)]
    wv = [jnp.dot(w_i.astype(BF16), jnp.concatenate([v, ones_cols], axis=1), preferred_element_type=F32)
          for w_i, v in zip(w, vs)]
    for j, (bi, h) in enumerate(inst):
        w_end_col = jnp.broadcast_to(col(bi, h, 3), (lc, d))
        vw = jnp.concatenate([(vs[j].astype(F32) * w_end_col).astype(BF16), w_end_col.astype(BF16)], axis=1)
        cn_s[bi * H + h] = decays[bi][h:h + 1, :] * cn_prev[j] + jnp.dot(kts[j], vw, preferred_element_type=F32)
    hh = []
    for j, (bi, h) in enumerate(inst):
        tot = col(bi, h, 1) * qcn[j] + wv[j]
        den = tot[:, d:2 * d]
        hh.append(tot[:, 0:d] * (1.0 / jnp.maximum(jnp.abs(den), col(bi, h, 2))))
    xc = [hh_i - _lane_mean(hh_i, mean_w) for hh_i in hh]
    var = [_lane_mean(xc_i * xc_i, mean_w) for xc_i in xc]
    for j, (bi, h) in enumerate(inst):
        hn = xc[j] * lax.rsqrt(var[j] + LN_EPS) * ng_ref[:, hsl(h)]
        mb_ref[bi, :, hsl(h)] = (og_ref[bi, :, hsl(h)] * hn).astype(BF16)

    @pl.when(i == last)
    def _():
        for bi in range(bb):
            for h in range(H):
                cn = cn_s[bi * H + h]
                c_out_ref[bi, h] = cn[:, 0:d].T
                n_out_ref[bi, h:h + 1, :] = cn[:, d:2 * d].T[0:1, :]
            m_out_ref[bi] = m_s[bi, 0:H, 0:1]


def _mlstm(qvm, kt, gates, og, norm_g, c0, n0, m0, *, lc, bb):
    B, T, _ = qvm.shape
    nt = T // lc
    W = MLSTM_WIDTH
    H, d = MLSTM_HEADS, MLSTM_HEAD_DIM
    kern = functools.partial(_mlstm_kernel, lc=lc, bb=bb)
    return pl.pallas_call(
        kern,
        grid=(B // bb, nt),
        in_specs=[
            pl.BlockSpec((bb, lc, W), lambda b, i: (b, i, 0)),
            pl.BlockSpec((bb, W, lc), lambda b, i: (b, 0, i)),
            pl.BlockSpec((bb, lc, W), lambda b, i: (b, i, 1)),
            pl.BlockSpec((1, bb, GATE_ROWS, lc), lambda b, i: (i, b, 0, 0)),
            pl.BlockSpec((bb, lc, W), lambda b, i: (b, i, 0)),
            pl.BlockSpec((1, W), lambda b, i: (0, 0)),
            pl.BlockSpec((bb, H, d, d), lambda b, i: (b, 0, 0, 0)),
            pl.BlockSpec((bb, H, d), lambda b, i: (b, 0, 0)),
            pl.BlockSpec((bb, H, 1), lambda b, i: (b, 0, 0)),
        ],
        out_specs=[
            pl.BlockSpec((bb, lc, W), lambda b, i: (b, i, 0)),
            pl.BlockSpec((bb, H, d, d), lambda b, i: (b, 0, 0, 0)),
            pl.BlockSpec((bb, H, d), lambda b, i: (b, 0, 0)),
            pl.BlockSpec((bb, H, 1), lambda b, i: (b, 0, 0)),
        ],
        out_shape=[
            jax.ShapeDtypeStruct((B, T, W), BF16),
            jax.ShapeDtypeStruct((B, H, d, d), F32),
            jax.ShapeDtypeStruct((B, H, d), F32),
            jax.ShapeDtypeStruct((B, H, 1), F32),
        ],
        scratch_shapes=[
            pltpu.VMEM((bb * H, d, 2 * d), F32),
            pltpu.VMEM((bb, 8, LANES), F32),
        ],
        compiler_params=_cparams(("parallel", "arbitrary")),
        name="mlstm",
    )(qvm, kt, qvm, gates, og, norm_g, c0, n0, m0)


def _layer_norm(z, g, b):
    mu = jnp.mean(z, axis=-1, keepdims=True)
    zc = z - mu
    var = jnp.mean(zc * zc, axis=-1, keepdims=True)
    return zc * lax.rsqrt(var + LN_EPS) * g + b


def _merge_ffn_kernel(x_ref, att_ref, mb_ref, wo_ref, g1_ref, b1_ref, w1_ref, bf1_ref, w2_ref, bf2_ref,
                      g2_ref, b2_ref, y_ref, *, fc, parts):
    W = ATT_WIDTH
    tm = x_ref.shape[1]
    rp = tm // parts
    groups = [slice(p * rp, (p + 1) * rp) for p in range(parts)]
    mixes = [jnp.dot(att_ref[0, rs, :], wo_ref[0:W, :], preferred_element_type=F32)
             + jnp.dot(mb_ref[0, rs, :], wo_ref[W:2 * W, :], preferred_element_type=F32) for rs in groups]
    for rs, mix in zip(groups, mixes):
        h = _layer_norm(DEEPNORM_ALPHA * x_ref[0, rs, :] + mix, g1_ref[...], b1_ref[...])
        hb = h.astype(BF16)
        f = jnp.zeros(h.shape, F32)
        for j in range(D_FF // fc):
            cs = slice(j * fc, (j + 1) * fc)
            a = jnp.dot(hb, w1_ref[:, cs], preferred_element_type=F32) + bf1_ref[:, cs]
            a = jnp.square(jnp.maximum(a, 0.0)).astype(BF16)
            f = f + jnp.dot(a, w2_ref[cs, :], preferred_element_type=F32)
        f = f + bf2_ref[...]
        y_ref[0, rs, :] = _layer_norm(DEEPNORM_ALPHA * h + f, g2_ref[...], b2_ref[...])


def _merge_ffn(x, att, mb, wo, g1, b1, w1, bf1, w2, bf2, g2, b2, *, tm, fc, parts):
    B, T, D = x.shape
    nt = T // tm
    W = ATT_WIDTH
    const = lambda shape: pl.BlockSpec(shape, lambda b, i: (0, 0), pipeline_mode=pl.Buffered(1))
    kern = functools.partial(_merge_ffn_kernel, fc=fc, parts=parts)
    return pl.pallas_call(
        kern,
        grid=(B, nt),
        in_specs=[
            pl.BlockSpec((1, tm, D), lambda b, i: (b, i, 0)),
            pl.BlockSpec((1, tm, W), lambda b, i: (b, i, 0)),
            pl.BlockSpec((1, tm, W), lambda b, i: (b, i, 0)),
            const((D, D)), const((1, D)), const((1, D)),
            const((D, D_FF)), const((1, D_FF)), const((D_FF, D)), const((1, D)),
            const((1, D)), const((1, D)),
        ],
        out_specs=pl.BlockSpec((1, tm, D), lambda b, i: (b, i, 0)),
        out_shape=jax.ShapeDtypeStruct((B, T, D), F32),
        compiler_params=_cparams(("parallel", "parallel")),
        name="merge_ffn",
    )(x, att, mb, wo, g1, b1, w1, bf1, w2, bf2, g2, b2)


def _token_tile(t):
    return 512 if t % 512 == 0 else t


def kernel(x_prompt, x_sample, cache_k, cache_v, state_C, state_n, state_m, w_in, b_in, rel_bias,
           mlstm_norm_g, w_out, ln1_g, ln1_b, w_ff1, b_ff1, w_ff2, b_ff2, ln2_g, ln2_b):
    assert w_in.shape[0] == DEPTH == 1
    B, S, D = x_prompt.shape
    DB, T, _ = x_sample.shape
    H, d = MLSTM_HEADS, MLSTM_HEAD_DIM
    W = MLSTM_WIDTH
    assert S % ATT_REACH == 0 and T == CHUNK and cache_k.shape[2] == ATT_REACH

    n_gate = 2 * H
    wi, bi_ = w_in[0], b_in[0]
    w_cat = jnp.concatenate(
        [wi[:, :OFF_KB], wi[:, OFF_VB:OFF_GATES], jnp.pad(wi[:, OFF_GATES:], ((0, 0), (0, GATE_PAD - n_gate)))],
        axis=1).astype(BF16)
    b_cat = jnp.concatenate(
        [bi_[:OFF_KB], bi_[OFF_VB:OFF_GATES], jnp.pad(bi_[OFF_GATES:], (0, GATE_PAD - n_gate))])[None, :]
    wkt = wi[:, OFF_KB:OFF_VB].T.astype(BF16)
    bkt = bi_[OFF_KB:OFF_VB][:, None]
    wo = w_out[0].astype(BF16)
    w1 = w_ff1[0].astype(BF16)
    w2 = w_ff2[0].astype(BF16)
    row = lambda p: p[0][None, :]
    ffn_params = (wo, row(ln1_g), row(ln1_b), w1, row(b_ff1), w2, row(b_ff2), row(ln2_g), row(ln2_b))
    norm_g = row(mlstm_norm_g)

    def layer(x, k_prev, v_prev, prev_map, c0, n0, m0, *, seqs, cps, mask_first):
        Bx, Tx, _ = x.shape
        n_tok = Bx * Tx
        t_seq = n_tok // seqs
        tm = _token_tile(Tx)
        keep = min(ATT_REACH, Tx)
        lc = min(MLSTM_CHUNK, t_seq)
        qkva, qvm, kt, og, gates, k_last, v_last = _in_proj(x, w_cat, b_cat, wkt, bkt, tm=tm, keep=keep, lc=lc)
        seq = lambda a: a.reshape(seqs, t_seq, a.shape[-1])
        tq = _token_tile(t_seq)
        keys = -(-(ATT_REACH + cps * CHUNK) // LANES) * LANES
        bias_tab = _bias_table(rel_bias[0], cps=cps, keys=keys)
        if k_prev is None:
            k_prev = v_prev = seq(qkva)
        att = _band_attn(seq(qkva), k_prev, v_prev, prev_map, bias_tab,
                         tq=tq, cps=cps, keys=keys, mask_first=mask_first)
        gates = gates.reshape(t_seq // lc, seqs, GATE_ROWS, lc)
        kt = kt.reshape(Bx, W, seqs // Bx, t_seq).transpose(0, 2, 1, 3).reshape(seqs, W, t_seq)
        mb, c_new, n_new, m_new = _mlstm(seq(qvm), kt, gates, seq(og), norm_g, c0, n0, m0,
                                         lc=lc, bb=min(seqs, MLSTM_STREAMS))
        tok = lambda a: a.reshape(Bx, Tx, a.shape[-1])
        tf = FFN_TILE if Tx % FFN_TILE == 0 else tm
        y = _merge_ffn(x, tok(att), tok(mb), *ffn_params, tm=tf, fc=2048, parts=max(tf // FFN_GROUP, 1))
        return y, k_last, v_last, c_new, n_new, m_new

    zeros_c = jnp.zeros((B, H, d, d), F32)
    zeros_n = jnp.zeros((B, H, d), F32)
    zeros_m = jnp.zeros((B, H, 1), F32)
    prev_prompt = (lambda b, i: (b, jnp.maximum(i - 1, 0), 1), lambda b, i: (b, jnp.maximum(i - 1, 0), 2))
    yp, kp, vp, cp, np_, mp = layer(x_prompt, None, None, prev_prompt, zeros_c, zeros_n, zeros_m,
                                    seqs=B, cps=1, mask_first=True)

    ck = cache_k[0].reshape(DB, ATT_REACH, ATT_WIDTH)
    cv = cache_v[0].reshape(DB, ATT_REACH, ATT_WIDTH)
    prev_sample = (lambda b, i: (b, 0, 0), lambda b, i: (b, 0, 0))
    ys, ks, vs, cs, ns, ms = layer(
        x_sample.reshape(1, DB * T, D), ck, cv, prev_sample,
        state_C[0].astype(F32), state_n[0].astype(F32), state_m[0].astype(F32).reshape(DB, H, 1),
        seqs=DB, cps=1, mask_first=False)

    sd = state_C.dtype
    heads = lambda a, nb, t: a.reshape(nb, t, ATT_HEADS, ATT_HEAD_DIM)[None]
    keep_p = min(ATT_REACH, S)
    return (yp, ys.reshape(DB, T, D),
            heads(kp, B, keep_p).astype(cache_k.dtype), heads(vp, B, keep_p).astype(cache_v.dtype),
            cp[None].astype(sd), np_[None].astype(sd), mp.reshape(1, B, H).astype(sd),
            heads(ks, DB, T).astype(cache_k.dtype), heads(vs, DB, T).astype(cache_v.dtype),
            cs[None].astype(sd), ns[None].astype(sd), ms.reshape(1, DB, H).astype(sd))
```

```python
import functools

import jax
import jax.numpy as jnp
from jax import lax
from jax.experimental import pallas as pl
from jax.experimental.pallas import tpu as pltpu

F32 = jnp.float32
BF16 = jnp.bfloat16

D_MODEL = 1024
CHUNK = 64
LEFT_CHUNKS = 8
ATT_REACH = LEFT_CHUNKS * CHUNK
ATT_WIDTH = 512
MLSTM_WIDTH = 512
ATT_HEADS = 8
ATT_HEAD_DIM = 64
MLSTM_HEADS = 4
MLSTM_HEAD_DIM = 128
MAX_REL = 128
D_FF = 4 * D_MODEL
DEPTH = 1
DEEPNORM_ALPHA = (2 * DEPTH) ** 0.25
LN_EPS = 1e-5
NEG_INF = -1e30
ATT_SCALE = ATT_HEAD_DIM ** -0.5
KB_SCALE = MLSTM_HEAD_DIM ** -0.5

OFF_QB = 3 * ATT_WIDTH
OFF_KB = OFF_QB + MLSTM_WIDTH
OFF_VB = OFF_KB + MLSTM_WIDTH
OFF_OB = OFF_VB + MLSTM_WIDTH
OFF_GATES = OFF_OB + MLSTM_WIDTH

LANES = 128
GATE_ROWS_PAD = 16
MLSTM_CHUNK = 256
MLSTM_CHUNKS_PER_STEP = 4
MLSTM_STREAMS = 2
GATE_ROWS = 24
FFN_TILE = 1024
FFN_GROUP = 256
HEAD_PAIRS = ATT_HEADS // 2
VMEM_LIMIT = 56 * 1024 * 1024
NT_DIMS = (((1,), (1,)), ((), ()))


def _cparams(sem):
    return pltpu.CompilerParams(dimension_semantics=sem, vmem_limit_bytes=VMEM_LIMIT)


def _in_proj_kernel(x_ref, w_ref, b_ref, wkt_ref, bkt_ref,
                    qkva_ref, qvm_ref, kt_ref, og_ref, gates_ref, klast_ref, vlast_ref, *, tm, keep, lc):
    i = pl.program_id(1)
    last = pl.num_programs(1) - 1
    xb = x_ref[0].astype(BF16)

    def proj(c0, width):
        return (jnp.dot(xb, w_ref[:, c0:c0 + width], preferred_element_type=F32)
                + b_ref[:, c0:c0 + width])

    W = ATT_WIDTH
    H = MLSTM_HEADS
    ktg = lax.dot_general(wkt_ref[...], xb, NT_DIMS, preferred_element_type=F32) + bkt_ref[...]
    kt_ref[0] = (ktg[0:W] * KB_SCALE).astype(BF16)
    g = ktg[W:W + 2 * H]
    log_sig = jnp.minimum(g, 0.0) - jnp.log1p(jnp.exp(-jnp.abs(g)))
    head_row = lax.broadcasted_iota(jnp.int32, g.shape, 0) < H
    g8 = jnp.where(head_row, g, log_sig)
    pos = lax.broadcasted_iota(jnp.int32, g8.shape, 1) % lc
    csum = g8
    sh = 1
    while sh < lc:
        csum = csum + jnp.where(pos >= sh, pltpu.roll(csum, sh, 1), 0.0)
        sh *= 2
    b = pltpu.roll(csum, H, 0)
    a = g8 - b
    cm = a
    sh = 1
    while sh < lc:
        cm = jnp.maximum(cm, jnp.where(pos >= sh, pltpu.roll(cm, sh, 1), -jnp.inf))
        sh *= 2
    rows = [jnp.where(head_row, t, 0.0) for t in (b, a, cm)]
    for c in range(tm // lc):
        for j, t in enumerate(rows):
            gates_ref[c, 0, 8 * j:8 * j + 8, :] = t[:, c * lc:(c + 1) * lc]

    qvm_ref[0, :, 0:W] = proj(OFF_QB, W).astype(BF16)
    qvm_ref[0, :, W:2 * W] = proj(OFF_VB, W).astype(BF16)
    ob = proj(OFF_OB, W)
    og_ref[0] = 1.0 / (1.0 + jnp.exp(-ob))
    qa = proj(0, W)
    qkva_ref[0, :, 0:W] = (qa * ATT_SCALE).astype(BF16)
    ka = proj(W, W)
    qkva_ref[0, :, W:2 * W] = ka.astype(BF16)
    va = proj(2 * W, W)
    qkva_ref[0, :, 2 * W:3 * W] = va.astype(BF16)

    @pl.when(i == last)
    def _():
        klast_ref[0] = ka[tm - keep:, :]
        vlast_ref[0] = va[tm - keep:, :]


def _in_proj(x, w, b, wkt, bkt, *, tm, keep, lc):
    B, T, D = x.shape
    nt = T // tm
    W = ATT_WIDTH
    kern = functools.partial(_in_proj_kernel, tm=tm, keep=keep, lc=lc)
    return pl.pallas_call(
        kern,
        grid=(B, nt),
        in_specs=[
            pl.BlockSpec((1, tm, D), lambda b, i: (b, i, 0)),
            pl.BlockSpec(w.shape, lambda b, i: (0, 0)),
            pl.BlockSpec(b.shape, lambda b, i: (0, 0)),
            pl.BlockSpec(wkt.shape, lambda b, i: (0, 0)),
            pl.BlockSpec(bkt.shape, lambda b, i: (0, 0)),
        ],
        out_specs=[
            pl.BlockSpec((1, tm, 3 * W), lambda b, i: (b, i, 0)),
            pl.BlockSpec((1, tm, 2 * W), lambda b, i: (b, i, 0)),
            pl.BlockSpec((1, W, tm), lambda b, i: (b, 0, i)),
            pl.BlockSpec((1, tm, W), lambda b, i: (b, i, 0)),
            pl.BlockSpec((tm // lc, 1, GATE_ROWS, lc), lambda b, i: (i, b, 0, 0)),
            pl.BlockSpec((1, keep, W), lambda b, i: (b, 0, 0)),
            pl.BlockSpec((1, keep, W), lambda b, i: (b, 0, 0)),
        ],
        out_shape=[
            jax.ShapeDtypeStruct((B, T, 3 * W), BF16),
            jax.ShapeDtypeStruct((B, T, 2 * W), BF16),
            jax.ShapeDtypeStruct((B, W, T), BF16),
            jax.ShapeDtypeStruct((B, T, W), F32),
            jax.ShapeDtypeStruct((T // lc, B, GATE_ROWS, lc), F32),
            jax.ShapeDtypeStruct((B, keep, W), F32),
            jax.ShapeDtypeStruct((B, keep, W), F32),
        ],
        compiler_params=_cparams(("parallel", "arbitrary")),
        name="in_proj",
    )(x, w, b, wkt, bkt)


def _bias_table_kernel(g_ref, o_ref, *, rows, keys):
    L = g_ref.shape[-1]
    r = lax.broadcasted_iota(jnp.int32, (rows, keys), 0)
    c = lax.broadcasted_iota(jnp.int32, (rows, keys), 1)
    band_start = (r // CHUNK) * CHUNK
    in_band = (c >= band_start) & (c < band_start + ATT_REACH + CHUNK)
    for h in range(ATT_HEADS):
        g = jnp.broadcast_to(g_ref[h], (rows, L))
        t = pltpu.roll(g, L - rows, 1, stride=1, stride_axis=0)
        tab = jnp.where(in_band, t[:, :keys], NEG_INF)
        o_ref[h // 2, (h % 2) * rows:(h % 2 + 1) * rows, :] = tab


def _bias_table(rel_bias, *, cps, keys):
    rows = cps * CHUNK
    L = ((rows + keys + LANES - 1) // LANES) * LANES
    n_const = rows + ATT_REACH - MAX_REL
    const = jnp.broadcast_to(rel_bias[:, 2 * MAX_REL:], (ATT_HEADS, n_const))
    ramp = rel_bias[:, ::-1]
    tail = jnp.broadcast_to(rel_bias[:, :1], (ATT_HEADS, LANES))
    g = jnp.concatenate([const, ramp, tail], axis=1)[:, :L].reshape(ATT_HEADS, 1, L)
    kern = functools.partial(_bias_table_kernel, rows=rows, keys=keys)
    return pl.pallas_call(
        kern,
        out_shape=jax.ShapeDtypeStruct((HEAD_PAIRS, 2 * rows, keys), F32),
        name="bias_table",
    )(g)


def _attn_kernel(q_ref, kc_ref, vc_ref, kp_ref, vp_ref, bias_ref, o_ref, kk_ref, vv_ref,
                 *, tq, cps, keys, mask_first):
    i = pl.program_id(1)
    rows = cps * CHUNK
    kk_ref[0:ATT_REACH, :] = kp_ref[0].astype(BF16)
    vv_ref[0:ATT_REACH, :] = vp_ref[0].astype(BF16)
    kk_ref[ATT_REACH:ATT_REACH + tq, :] = kc_ref[0]
    vv_ref[ATT_REACH:ATT_REACH + tq, :] = vc_ref[0]
    total = kk_ref.shape[0]
    if total > ATT_REACH + tq:
        kk_ref[ATT_REACH + tq:, :] = jnp.zeros((total - ATT_REACH - tq, ATT_WIDTH), BF16)
        vv_ref[ATT_REACH + tq:, :] = jnp.zeros((total - ATT_REACH - tq, ATT_WIDTH), BF16)

    lane = lax.broadcasted_iota(jnp.int32, (rows, LANES), 1)
    first_head = lane < ATT_HEAD_DIM

    def tile(no_past):
        for sub in range(tq // rows):
            off = sub * rows
            first_valid = max(ATT_REACH - off, 0) if no_past else 0
            k0 = first_valid // LANES * LANES
            ones_cols = jnp.ones((keys - k0, LANES), BF16)
            for pair in range(HEAD_PAIRS):
                ls = slice(pair * LANES, (pair + 1) * LANES)
                q2 = q_ref[0, off:off + rows, ls]
                zero = jnp.zeros_like(q2)
                qs = jnp.concatenate([jnp.where(first_head, q2, zero), jnp.where(first_head, zero, q2)], axis=0)
                s = lax.dot_general(qs, kk_ref[off + k0:off + keys, ls], NT_DIMS,
                                    preferred_element_type=F32)
                s = s + bias_ref[pair, :, k0:keys]
                if first_valid > k0:
                    col = lax.broadcasted_iota(jnp.int32, s.shape, 1)
                    s = jnp.where(col >= first_valid - k0, s, NEG_INF)
                m = jnp.max(s, axis=-1, keepdims=True)
                e = jnp.exp(s - m).astype(BF16)
                v_ext = jnp.concatenate([vv_ref[off + k0:off + keys, ls], ones_cols], axis=1)
                o2 = jnp.dot(e, v_ext, preferred_element_type=F32)
                o2 = o2[:, 0:LANES] * (1.0 / o2[:, LANES:2 * LANES])
                o = jnp.where(first_head, o2[:rows], o2[rows:])
                o_ref[0, off:off + rows, ls] = o.astype(BF16)

    if mask_first:
        pl.when(i == 0)(lambda: tile(True))
        pl.when(i > 0)(lambda: tile(False))
    else:
        tile(False)


def _band_attn(qkva, k_prev, v_prev, prev_map, bias_tab, *, tq, cps, keys, mask_first):
    B, T, _ = qkva.shape
    nt = T // tq
    rows_total = max(ATT_REACH + tq, (tq // (cps * CHUNK) - 1) * cps * CHUNK + keys)
    kern = functools.partial(_attn_kernel, tq=tq, cps=cps, keys=keys, mask_first=mask_first)
    W = ATT_WIDTH
    prev_block = (1, ATT_REACH, W)
    return pl.pallas_call(
        kern,
        grid=(B, nt),
        in_specs=[
            pl.BlockSpec((1, tq, W), lambda b, i: (b, i, 0)),
            pl.BlockSpec((1, tq, W), lambda b, i: (b, i, 1)),
            pl.BlockSpec((1, tq, W), lambda b, i: (b, i, 2)),
            pl.BlockSpec(prev_block, prev_map[0]),
            pl.BlockSpec(prev_block, prev_map[1]),
            pl.BlockSpec(bias_tab.shape, lambda b, i: (0, 0, 0)),
        ],
        out_specs=pl.BlockSpec((1, tq, W), lambda b, i: (b, i, 0)),
        out_shape=jax.ShapeDtypeStruct((B, T, W), BF16),
        scratch_shapes=[pltpu.VMEM((rows_total, W), BF16), pltpu.VMEM((rows_total, W), BF16)],
        compiler_params=_cparams(("parallel", "arbitrary")),
        name="band_attn",
    )(qkva, qkva, qkva, k_prev, v_prev, bias_tab)


def _lane_mean(x, mean_w):
    hi = x.astype(BF16)
    lo = (x - hi.astype(F32)).astype(BF16)
    return jnp.dot(jnp.concatenate([hi, lo], axis=1), mean_w, preferred_element_type=F32)


def _mlstm_kernel(q_ref, kt_ref, v_ref, gates_ref, og_ref, ng_ref, c0_ref, n0_ref, m0_ref,
                  mb_ref, c_out_ref, n_out_ref, m_out_ref, cn_s, m_s, *, lc, bb, nc):
    i = pl.program_id(1)
    last = pl.num_programs(1) - 1
    H, d = MLSTM_HEADS, MLSTM_HEAD_DIM

    @pl.when(i == 0)
    def _():
        m_s[...] = jnp.zeros(m_s.shape, F32)
        for bi in range(bb):
            for h in range(H):
                cn_s[bi * H + h, :, 0:d] = c0_ref[bi, h].T
                cn_s[bi * H + h, :, d:2 * d] = jnp.broadcast_to(n0_ref[bi, h:h + 1, :], (d, d)).T
            m_s[bi, 0:H, :] = jnp.broadcast_to(m0_ref[bi], (H, LANES))

    ti = lax.broadcasted_iota(jnp.int32, (lc, lc), 0)
    si = lax.broadcasted_iota(jnp.int32, (lc, lc), 1)
    causal = ti >= si
    ones_cols = jnp.ones((lc, d), BF16)
    mean_w = jnp.full((2 * d, d), 1.0 / d, BF16)

    a_rows, decays, col_sets = {}, {}, {}
    m_cur = [m_s[bi, :, 0:1] for bi in range(bb)]
    for c in range(nc):
        for bi in range(bb):
            b = gates_ref[c, bi, 0:8, :]
            a = gates_ref[c, bi, 8:16, :]
            cm = gates_ref[c, bi, 16:24, :]
            m_prev = m_cur[bi]
            inter = b + m_prev
            m_t = jnp.maximum(inter, b + cm)
            b_last = b[:, lc - 1:lc]
            m_new = jnp.maximum(b_last + m_prev, b_last + cm[:, lc - 1:lc])
            w_end = jnp.exp(b_last + a - m_new)
            stacked = jnp.concatenate(
                [b - m_t, jnp.exp(inter - m_t), jnp.exp(-m_t), w_end, jnp.zeros((LANES - 32, lc), F32)], axis=0)
            col_sets[c, bi] = stacked.T
            a_rows[c, bi] = a
            decays[c, bi] = jnp.exp(b_last + m_prev - m_new)
            m_cur[bi] = m_new
    for bi in range(bb):
        m_s[bi] = jnp.broadcast_to(m_cur[bi], (8, LANES))

    heads = [(bi, h) for bi in range(bb) for h in range(H)]
    inst = [(c, bi, h) for c in range(nc) for bi, h in heads]
    hsl = lambda h: slice(h * d, (h + 1) * d)
    rsl = lambda c: slice(c * lc, (c + 1) * lc)
    col = lambda c, bi, h, j: col_sets[c, bi][:, 8 * j + h:8 * j + h + 1]
    qs = {(c, bi, h): q_ref[bi, rsl(c), hsl(h)] for c, bi, h in inst}
    kts = {(c, bi, h): kt_ref[bi, hsl(h), rsl(c)] for c, bi, h in inst}
    vs = {(c, bi, h): v_ref[bi, rsl(c), hsl(h)] for c, bi, h in inst}

    qk = {k_: jnp.dot(qs[k_], kts[k_], preferred_element_type=F32) for k_ in inst}
    cn_cur = {(bi, h): cn_s[bi * H + h] for bi, h in heads}
    qcn = {}
    for c in range(nc):
        for bi, h in heads:
            qcn[c, bi, h] = jnp.dot(qs[c, bi, h], cn_cur[bi, h].astype(BF16), preferred_element_type=F32)
        for bi, h in heads:
            w_end_col = jnp.broadcast_to(col(c, bi, h, 3), (lc, d))
            vw = jnp.concatenate(
                [(vs[c, bi, h].astype(F32) * w_end_col).astype(BF16), w_end_col.astype(BF16)], axis=1)
            cn_cur[bi, h] = (decays[c, bi][h:h + 1, :] * cn_cur[bi, h]
                             + jnp.dot(kts[c, bi, h], vw, preferred_element_type=F32))
    for bi, h in heads:
        cn_s[bi * H + h] = cn_cur[bi, h]
    w = {(c, bi, h): jnp.exp(jnp.where(causal, col(c, bi, h, 0) + a_rows[c, bi][h:h + 1, :], -jnp.inf))
         * qk[c, bi, h] for c, bi, h in inst}
    wv = {k_: jnp.dot(w[k_].astype(BF16), jnp.concatenate([vs[k_], ones_cols], axis=1),
                      preferred_element_type=F32) for k_ in inst}
    hh = {}
    for c, bi, h in inst:
        tot = col(c, bi, h, 1) * qcn[c, bi, h] + wv[c, bi, h]
        den = tot[:, d:2 * d]
        hh[c, bi, h] = tot[:, 0:d] * (1.0 / jnp.maximum(jnp.abs(den), col(c, bi, h, 2)))
    xc = {k_: hh[k_] - _lane_mean(hh[k_], mean_w) for k_ in inst}
    var = {k_: _lane_mean(xc[k_] * xc[k_], mean_w) for k_ in inst}
    for c, bi, h in inst:
        hn = xc[c, bi, h] * lax.rsqrt(var[c, bi, h] + LN_EPS) * ng_ref[:, hsl(h)]
        mb_ref[bi, rsl(c), hsl(h)] = (og_ref[bi, rsl(c), hsl(h)] * hn).astype(BF16)

    @pl.when(i == last)
    def _():
        for bi in range(bb):
            for h in range(H):
                cn = cn_s[bi * H + h]
                c_out_ref[bi, h] = cn[:, 0:d].T
                n_out_ref[bi, h:h + 1, :] = cn[:, d:2 * d].T[0:1, :]
            m_out_ref[bi] = m_s[bi, 0:H, 0:1]


def _mlstm(qvm, kt, gates, og, norm_g, c0, n0, m0, *, lc, bb, nc):
    B, T, _ = qvm.shape
    ts = nc * lc
    nt = T // ts
    W = MLSTM_WIDTH
    H, d = MLSTM_HEADS, MLSTM_HEAD_DIM
    kern = functools.partial(_mlstm_kernel, lc=lc, bb=bb, nc=nc)
    return pl.pallas_call(
        kern,
        grid=(B // bb, nt),
        in_specs=[
            pl.BlockSpec((bb, ts, W), lambda b, i: (b, i, 0)),
            pl.BlockSpec((bb, W, ts), lambda b, i: (b, 0, i)),
            pl.BlockSpec((bb, ts, W), lambda b, i: (b, i, 1)),
            pl.BlockSpec((nc, bb, GATE_ROWS, lc), lambda b, i: (i, b, 0, 0)),
            pl.BlockSpec((bb, ts, W), lambda b, i: (b, i, 0)),
            pl.BlockSpec((1, W), lambda b, i: (0, 0)),
            pl.BlockSpec((bb, H, d, d), lambda b, i: (b, 0, 0, 0)),
            pl.BlockSpec((bb, H, d), lambda b, i: (b, 0, 0)),
            pl.BlockSpec((bb, H, 1), lambda b, i: (b, 0, 0)),
        ],
        out_specs=[
            pl.BlockSpec((bb, ts, W), lambda b, i: (b, i, 0)),
            pl.BlockSpec((bb, H, d, d), lambda b, i: (b, 0, 0, 0)),
            pl.BlockSpec((bb, H, d), lambda b, i: (b, 0, 0)),
            pl.BlockSpec((bb, H, 1), lambda b, i: (b, 0, 0)),
        ],
        out_shape=[
            jax.ShapeDtypeStruct((B, T, W), BF16),
            jax.ShapeDtypeStruct((B, H, d, d), F32),
            jax.ShapeDtypeStruct((B, H, d), F32),
            jax.ShapeDtypeStruct((B, H, 1), F32),
        ],
        scratch_shapes=[
            pltpu.VMEM((bb * H, d, 2 * d), F32),
            pltpu.VMEM((bb, 8, LANES), F32),
        ],
        compiler_params=_cparams(("parallel", "arbitrary")),
        name="mlstm",
    )(qvm, kt, qvm, gates, og, norm_g, c0, n0, m0)


def _layer_norm(z, g, b):
    mu = jnp.mean(z, axis=-1, keepdims=True)
    zc = z - mu
    var = jnp.mean(zc * zc, axis=-1, keepdims=True)
    return zc * lax.rsqrt(var + LN_EPS) * g + b


def _merge_ffn_kernel(x_ref, att_ref, mb_ref, wo_ref, g1_ref, b1_ref, w1_ref, bf1_ref, w2_ref, bf2_ref,
                      g2_ref, b2_ref, y_ref, *, fc, parts):
    W = ATT_WIDTH
    tm = x_ref.shape[1]
    rp = tm // parts
    groups = [slice(p * rp, (p + 1) * rp) for p in range(parts)]
    mixes = [jnp.dot(att_ref[0, rs, :], wo_ref[0:W, :], preferred_element_type=F32)
             + jnp.dot(mb_ref[0, rs, :], wo_ref[W:2 * W, :], preferred_element_type=F32) for rs in groups]
    for rs, mix in zip(groups, mixes):
        h = _layer_norm(DEEPNORM_ALPHA * x_ref[0, rs, :] + mix, g1_ref[...], b1_ref[...])
        hb = h.astype(BF16)
        f = jnp.zeros(h.shape, F32)
        for j in range(D_FF // fc):
            cs = slice(j * fc, (j + 1) * fc)
            a = jnp.dot(hb, w1_ref[:, cs], preferred_element_type=F32) + bf1_ref[:, cs]
            a = jnp.square(jnp.maximum(a, 0.0)).astype(BF16)
            f = f + jnp.dot(a, w2_ref[cs, :], preferred_element_type=F32)
        f = f + bf2_ref[...]
        y_ref[0, rs, :] = _layer_norm(DEEPNORM_ALPHA * h + f, g2_ref[...], b2_ref[...])


def _merge_ffn(x, att, mb, wo, g1, b1, w1, bf1, w2, bf2, g2, b2, *, tm, fc, parts):
    B, T, D = x.shape
    nt = T // tm
    W = ATT_WIDTH
    const = lambda shape: pl.BlockSpec(shape, lambda b, i: (0, 0), pipeline_mode=pl.Buffered(1))
    kern = functools.partial(_merge_ffn_kernel, fc=fc, parts=parts)
    return pl.pallas_call(
        kern,
        grid=(B, nt),
        in_specs=[
            pl.BlockSpec((1, tm, D), lambda b, i: (b, i, 0)),
            pl.BlockSpec((1, tm, W), lambda b, i: (b, i, 0)),
            pl.BlockSpec((1, tm, W), lambda b, i: (b, i, 0)),
            const((D, D)), const((1, D)), const((1, D)),
            const((D, D_FF)), const((1, D_FF)), const((D_FF, D)), const((1, D)),
            const((1, D)), const((1, D)),
        ],
        out_specs=pl.BlockSpec((1, tm, D), lambda b, i: (b, i, 0)),
        out_shape=jax.ShapeDtypeStruct((B, T, D), F32),
        compiler_params=_cparams(("parallel", "parallel")),
        name="merge_ffn",
    )(x, att, mb, wo, g1, b1, w1, bf1, w2, bf2, g2, b2)


def _token_tile(t):
    return 512 if t % 512 == 0 else t


def kernel(x_prompt, x_sample, cache_k, cache_v, state_C, state_n, state_m, w_in, b_in, rel_bias,
           mlstm_norm_g, w_out, ln1_g, ln1_b, w_ff1, b_ff1, w_ff2, b_ff2, ln2_g, ln2_b):
    assert w_in.shape[0] == DEPTH == 1
    B, S, D = x_prompt.shape
    DB, T, _ = x_sample.shape
    H, d = MLSTM_HEADS, MLSTM_HEAD_DIM
    W = MLSTM_WIDTH
    assert S % ATT_REACH == 0 and T == CHUNK and cache_k.shape[2] == ATT_REACH

    n_gate = 2 * H
    wi, bi_ = w_in[0], b_in[0]
    w_all = wi.astype(BF16)
    b_all = bi_[None, :]
    gate_pad = GATE_ROWS_PAD - n_gate
    wkt = jnp.pad(jnp.concatenate([wi[:, OFF_KB:OFF_VB], wi[:, OFF_GATES:]], axis=1).T,
                  ((0, gate_pad), (0, 0))).astype(BF16)
    bkt = jnp.pad(jnp.concatenate([bi_[OFF_KB:OFF_VB], bi_[OFF_GATES:]]), (0, gate_pad))[:, None]
    wo = w_out[0].astype(BF16)
    w1 = w_ff1[0].astype(BF16)
    w2 = w_ff2[0].astype(BF16)
    row = lambda p: p[0][None, :]
    ffn_params = (wo, row(ln1_g), row(ln1_b), w1, row(b_ff1), w2, row(b_ff2), row(ln2_g), row(ln2_b))
    norm_g = row(mlstm_norm_g)

    def layer(x, k_prev, v_prev, prev_map, c0, n0, m0, *, seqs, cps, mask_first):
        Bx, Tx, _ = x.shape
        n_tok = Bx * Tx
        t_seq = n_tok // seqs
        tm = _token_tile(Tx)
        keep = min(ATT_REACH, Tx)
        lc = min(MLSTM_CHUNK, t_seq)
        qkva, qvm, kt, og, gates, k_last, v_last = _in_proj(x, w_all, b_all, wkt, bkt, tm=tm, keep=keep, lc=lc)
        seq = lambda a: a.reshape(seqs, t_seq, a.shape[-1])
        tq = _token_tile(t_seq)
        keys = -(-(ATT_REACH + cps * CHUNK) // LANES) * LANES
        bias_tab = _bias_table(rel_bias[0], cps=cps, keys=keys)
        if k_prev is None:
            k_prev = v_prev = seq(qkva)
        att = _band_attn(seq(qkva), k_prev, v_prev, prev_map, bias_tab,
                         tq=tq, cps=cps, keys=keys, mask_first=mask_first)
        gates = gates.reshape(t_seq // lc, seqs, GATE_ROWS, lc)
        kt = kt.reshape(Bx, W, seqs // Bx, t_seq).transpose(0, 2, 1, 3).reshape(seqs, W, t_seq)
        mb, c_new, n_new, m_new = _mlstm(seq(qvm), kt, gates, seq(og), norm_g, c0, n0, m0,
                                         lc=lc, bb=min(seqs, MLSTM_STREAMS),
                                         nc=min(MLSTM_CHUNKS_PER_STEP, t_seq // lc))
        tok = lambda a: a.reshape(Bx, Tx, a.shape[-1])
        tf = FFN_TILE if Tx % FFN_TILE == 0 else tm
        y = _merge_ffn(x, tok(att), tok(mb), *ffn_params, tm=tf, fc=2048, parts=max(tf // FFN_GROUP, 1))
        return y, k_last, v_last, c_new, n_new, m_new

    zeros_c = jnp.zeros((B, H, d, d), F32)
    zeros_n = jnp.zeros((B, H, d), F32)
    zeros_m = jnp.zeros((B, H, 1), F32)
    prev_prompt = (lambda b, i: (b, jnp.maximum(i - 1, 0), 1), lambda b, i: (b, jnp.maximum(i - 1, 0), 2))
    yp, kp, vp, cp, np_, mp = layer(x_prompt, None, None, prev_prompt, zeros_c, zeros_n, zeros_m,
                                    seqs=B, cps=1, mask_first=True)

    ck = cache_k[0].reshape(DB, ATT_REACH, ATT_WIDTH)
    cv = cache_v[0].reshape(DB, ATT_REACH, ATT_WIDTH)
    prev_sample = (lambda b, i: (b, 0, 0), lambda b, i: (b, 0, 0))
    ys, ks, vs, cs, ns, ms = layer(
        x_sample.reshape(1, DB * T, D), ck, cv, prev_sample,
        state_C[0].astype(F32), state_n[0].astype(F32), state_m[0].astype(F32).reshape(DB, H, 1),
        seqs=DB, cps=1, mask_first=False)

    sd = state_C.dtype
    heads = lambda a, nb, t: a.reshape(nb, t, ATT_HEADS, ATT_HEAD_DIM)[None]
    keep_p = min(ATT_REACH, S)
    return (yp, ys.reshape(DB, T, D),
            heads(kp, B, keep_p).astype(cache_k.dtype), heads(vp, B, keep_p).astype(cache_v.dtype),
            cp[None].astype(sd), np_[None].astype(sd), mp.reshape(1, B, H).astype(sd),
            heads(ks, DB, T).astype(cache_k.dtype), heads(vs, DB, T).astype(cache_v.dtype),
            cs[None].astype(sd), ns[None].astype(sd), ms.reshape(1, DB, H).astype(sd))
```

```python
import functools

import jax
import jax.numpy as jnp
from jax import lax
from jax.experimental import pallas as pl
from jax.experimental.pallas import tpu as pltpu

F32 = jnp.float32
BF16 = jnp.bfloat16

D_MODEL = 1024
CHUNK = 64
LEFT_CHUNKS = 8
ATT_REACH = LEFT_CHUNKS * CHUNK
ATT_WIDTH = 512
MLSTM_WIDTH = 512
ATT_HEADS = 8
ATT_HEAD_DIM = 64
MLSTM_HEADS = 4
MLSTM_HEAD_DIM = 128
MAX_REL = 128
D_FF = 4 * D_MODEL
DEPTH = 1
DEEPNORM_ALPHA = (2 * DEPTH) ** 0.25
LN_EPS = 1e-5
NEG_INF = -1e30
ATT_SCALE = ATT_HEAD_DIM ** -0.5
KB_SCALE = MLSTM_HEAD_DIM ** -0.5

OFF_QB = 3 * ATT_WIDTH
OFF_KB = OFF_QB + MLSTM_WIDTH
OFF_VB = OFF_KB + MLSTM_WIDTH
OFF_OB = OFF_VB + MLSTM_WIDTH
OFF_GATES = OFF_OB + MLSTM_WIDTH

LANES = 128
GATE_ROWS_PAD = 16
MLSTM_CHUNK = 256
MLSTM_CHUNKS_PER_STEP = 4
MLSTM_STREAMS = 2
GATE_ROWS = 24
ATTN_TILE = 1024
FFN_TILE = 1024
FFN_GROUP = 256
HEAD_PAIRS = ATT_HEADS // 2
VMEM_LIMIT = 56 * 1024 * 1024
NT_DIMS = (((1,), (1,)), ((), ()))


def _cparams(sem):
    return pltpu.CompilerParams(dimension_semantics=sem, vmem_limit_bytes=VMEM_LIMIT)


def _in_proj_kernel(x_ref, w_ref, b_ref, wkt_ref, bkt_ref,
                    qkva_ref, qvm_ref, kt_ref, og_ref, gates_ref, klast_ref, vlast_ref, *, tm, keep, lc):
    i = pl.program_id(1)
    last = pl.num_programs(1) - 1
    xb = x_ref[0].astype(BF16)

    def proj(c0, width):
        return (jnp.dot(xb, w_ref[:, c0:c0 + width], preferred_element_type=F32)
                + b_ref[:, c0:c0 + width])

    W = ATT_WIDTH
    H = MLSTM_HEADS
    ktg = lax.dot_general(wkt_ref[...], xb, NT_DIMS, preferred_element_type=F32) + bkt_ref[...]
    kt_ref[0] = (ktg[0:W] * KB_SCALE).astype(BF16)
    g = ktg[W:W + 2 * H]
    log_sig = jnp.minimum(g, 0.0) - jnp.log1p(jnp.exp(-jnp.abs(g)))
    head_row = lax.broadcasted_iota(jnp.int32, g.shape, 0) < H
    g8 = jnp.where(head_row, g, log_sig)
    pos = lax.broadcasted_iota(jnp.int32, g8.shape, 1) % lc
    csum = g8
    sh = 1
    while sh < lc:
        csum = csum + jnp.where(pos >= sh, pltpu.roll(csum, sh, 1), 0.0)
        sh *= 2
    b = pltpu.roll(csum, H, 0)
    a = g8 - b
    cm = a
    sh = 1
    while sh < lc:
        cm = jnp.maximum(cm, jnp.where(pos >= sh, pltpu.roll(cm, sh, 1), -jnp.inf))
        sh *= 2
    rows = [jnp.where(head_row, t, 0.0) for t in (b, a, cm)]
    for c in range(tm // lc):
        for j, t in enumerate(rows):
            gates_ref[c, 0, 8 * j:8 * j + 8, :] = t[:, c * lc:(c + 1) * lc]

    qvm_ref[0, :, 0:W] = proj(OFF_QB, W).astype(BF16)
    qvm_ref[0, :, W:2 * W] = proj(OFF_VB, W).astype(BF16)
    ob = proj(OFF_OB, W)
    og_ref[0] = 1.0 / (1.0 + jnp.exp(-ob))
    qa = proj(0, W)
    qkva_ref[0, :, 0:W] = (qa * ATT_SCALE).astype(BF16)
    ka = proj(W, W)
    qkva_ref[0, :, W:2 * W] = ka.astype(BF16)
    va = proj(2 * W, W)
    qkva_ref[0, :, 2 * W:3 * W] = va.astype(BF16)

    @pl.when(i == last)
    def _():
        klast_ref[0] = ka[tm - keep:, :]
        vlast_ref[0] = va[tm - keep:, :]


def _in_proj(x, w, b, wkt, bkt, *, tm, keep, lc):
    B, T, D = x.shape
    nt = T // tm
    W = ATT_WIDTH
    kern = functools.partial(_in_proj_kernel, tm=tm, keep=keep, lc=lc)
    return pl.pallas_call(
        kern,
        grid=(B, nt),
        in_specs=[
            pl.BlockSpec((1, tm, D), lambda b, i: (b, i, 0)),
            pl.BlockSpec(w.shape, lambda b, i: (0, 0)),
            pl.BlockSpec(b.shape, lambda b, i: (0, 0)),
            pl.BlockSpec(wkt.shape, lambda b, i: (0, 0)),
            pl.BlockSpec(bkt.shape, lambda b, i: (0, 0)),
        ],
        out_specs=[
            pl.BlockSpec((1, tm, 3 * W), lambda b, i: (b, i, 0)),
            pl.BlockSpec((1, tm, 2 * W), lambda b, i: (b, i, 0)),
            pl.BlockSpec((1, W, tm), lambda b, i: (b, 0, i)),
            pl.BlockSpec((1, tm, W), lambda b, i: (b, i, 0)),
            pl.BlockSpec((tm // lc, 1, GATE_ROWS, lc), lambda b, i: (i, b, 0, 0)),
            pl.BlockSpec((1, keep, W), lambda b, i: (b, 0, 0)),
            pl.BlockSpec((1, keep, W), lambda b, i: (b, 0, 0)),
        ],
        out_shape=[
            jax.ShapeDtypeStruct((B, T, 3 * W), BF16),
            jax.ShapeDtypeStruct((B, T, 2 * W), BF16),
            jax.ShapeDtypeStruct((B, W, T), BF16),
            jax.ShapeDtypeStruct((B, T, W), F32),
            jax.ShapeDtypeStruct((T // lc, B, GATE_ROWS, lc), F32),
            jax.ShapeDtypeStruct((B, keep, W), F32),
            jax.ShapeDtypeStruct((B, keep, W), F32),
        ],
        compiler_params=_cparams(("parallel", "arbitrary")),
        name="in_proj",
    )(x, w, b, wkt, bkt)


def _key_gate_weights_kernel(wk_ref, wg_ref, o_ref):
    W = wk_ref.shape[1]
    n_gate = 2 * MLSTM_HEADS
    o_ref[0:W, :] = wk_ref[...].T.astype(BF16)
    gates_t = wg_ref[...].T[0:GATE_ROWS_PAD]
    keep = lax.broadcasted_iota(jnp.int32, gates_t.shape, 0) < n_gate
    o_ref[W:W + GATE_ROWS_PAD, :] = jnp.where(keep, gates_t, 0.0).astype(BF16)


def _key_gate_weights(w):
    D = w.shape[0]
    W = MLSTM_WIDTH
    return pl.pallas_call(
        _key_gate_weights_kernel,
        grid=(1,),
        in_specs=[pl.BlockSpec((D, W), lambda i: (0, OFF_KB // W)),
                  pl.BlockSpec((D, LANES), lambda i: (0, OFF_GATES // LANES))],
        out_specs=pl.BlockSpec((W + GATE_ROWS_PAD, D), lambda i: (0, 0)),
        out_shape=jax.ShapeDtypeStruct((W + GATE_ROWS_PAD, D), BF16),
        name="key_gate_weights",
    )(w, w)


def _bias_table_kernel(g_ref, o_ref, *, rows, keys):
    L = g_ref.shape[-1]
    r = lax.broadcasted_iota(jnp.int32, (rows, keys), 0)
    c = lax.broadcasted_iota(jnp.int32, (rows, keys), 1)
    band_start = (r // CHUNK) * CHUNK
    in_band = (c >= band_start) & (c < band_start + ATT_REACH + CHUNK)
    for h in range(ATT_HEADS):
        g = jnp.broadcast_to(g_ref[h], (rows, L))
        t = pltpu.roll(g, L - rows, 1, stride=1, stride_axis=0)
        tab = jnp.where(in_band, t[:, :keys], NEG_INF)
        o_ref[h // 2, (h % 2) * rows:(h % 2 + 1) * rows, :] = tab


def _bias_table(rel_bias, *, cps, keys):
    rows = cps * CHUNK
    L = ((rows + keys + LANES - 1) // LANES) * LANES
    n_const = rows + ATT_REACH - MAX_REL
    const = jnp.broadcast_to(rel_bias[:, 2 * MAX_REL:], (ATT_HEADS, n_const))
    ramp = rel_bias[:, ::-1]
    tail = jnp.broadcast_to(rel_bias[:, :1], (ATT_HEADS, LANES))
    g = jnp.concatenate([const, ramp, tail], axis=1)[:, :L].reshape(ATT_HEADS, 1, L)
    kern = functools.partial(_bias_table_kernel, rows=rows, keys=keys)
    return pl.pallas_call(
        kern,
        out_shape=jax.ShapeDtypeStruct((HEAD_PAIRS, 2 * rows, keys), F32),
        name="bias_table",
    )(g)


def _attn_kernel(q_ref, kc_ref, vc_ref, kp_ref, vp_ref, bias_ref, o_ref, kk_ref, vv_ref,
                 *, tq, cps, keys, mask_first):
    i = pl.program_id(1)
    rows = cps * CHUNK
    kk_ref[0:ATT_REACH, :] = kp_ref[0].astype(BF16)
    vv_ref[0:ATT_REACH, :] = vp_ref[0].astype(BF16)
    kk_ref[ATT_REACH:ATT_REACH + tq, :] = kc_ref[0]
    vv_ref[ATT_REACH:ATT_REACH + tq, :] = vc_ref[0]
    total = kk_ref.shape[0]
    if total > ATT_REACH + tq:
        kk_ref[ATT_REACH + tq:, :] = jnp.zeros((total - ATT_REACH - tq, ATT_WIDTH), BF16)
        vv_ref[ATT_REACH + tq:, :] = jnp.zeros((total - ATT_REACH - tq, ATT_WIDTH), BF16)

    lane = lax.broadcasted_iota(jnp.int32, (rows, LANES), 1)
    first_head = lane < ATT_HEAD_DIM

    def tile(no_past):
        for sub in range(tq // rows):
            off = sub * rows
            first_valid = max(ATT_REACH - off, 0) if no_past else 0
            k0 = first_valid // LANES * LANES
            ones_cols = jnp.ones((keys - k0, LANES), BF16)
            for pair in range(HEAD_PAIRS):
                ls = slice(pair * LANES, (pair + 1) * LANES)
                q2 = q_ref[0, off:off + rows, ls]
                zero = jnp.zeros_like(q2)
                qs = jnp.concatenate([jnp.where(first_head, q2, zero), jnp.where(first_head, zero, q2)], axis=0)
                s = lax.dot_general(qs, kk_ref[off + k0:off + keys, ls], NT_DIMS,
                                    preferred_element_type=F32)
                s = s + bias_ref[pair, :, k0:keys]
                if first_valid > k0:
                    col = lax.broadcasted_iota(jnp.int32, s.shape, 1)
                    s = jnp.where(col >= first_valid - k0, s, NEG_INF)
                m = jnp.max(s, axis=-1, keepdims=True)
                e = jnp.exp(s - m).astype(BF16)
                v_ext = jnp.concatenate([vv_ref[off + k0:off + keys, ls], ones_cols], axis=1)
                o2 = jnp.dot(e, v_ext, preferred_element_type=F32)
                o2 = o2[:, 0:LANES] * (1.0 / o2[:, LANES:2 * LANES])
                o = jnp.where(first_head, o2[:rows], o2[rows:])
                o_ref[0, off:off + rows, ls] = o.astype(BF16)

    if mask_first:
        pl.when(i == 0)(lambda: tile(True))
        pl.when(i > 0)(lambda: tile(False))
    else:
        tile(False)


def _band_attn(qkva, k_prev, v_prev, prev_map, bias_tab, *, tq, cps, keys, mask_first):
    B, T, _ = qkva.shape
    nt = T // tq
    rows_total = max(ATT_REACH + tq, (tq // (cps * CHUNK) - 1) * cps * CHUNK + keys)
    kern = functools.partial(_attn_kernel, tq=tq, cps=cps, keys=keys, mask_first=mask_first)
    W = ATT_WIDTH
    prev_block = (1, ATT_REACH, W)
    return pl.pallas_call(
        kern,
        grid=(B, nt),
        in_specs=[
            pl.BlockSpec((1, tq, W), lambda b, i: (b, i, 0)),
            pl.BlockSpec((1, tq, W), lambda b, i: (b, i, 1)),
            pl.BlockSpec((1, tq, W), lambda b, i: (b, i, 2)),
            pl.BlockSpec(prev_block, prev_map[0]),
            pl.BlockSpec(prev_block, prev_map[1]),
            pl.BlockSpec(bias_tab.shape, lambda b, i: (0, 0, 0)),
        ],
        out_specs=pl.BlockSpec((1, tq, W), lambda b, i: (b, i, 0)),
        out_shape=jax.ShapeDtypeStruct((B, T, W), BF16),
        scratch_shapes=[pltpu.VMEM((rows_total, W), BF16), pltpu.VMEM((rows_total, W), BF16)],
        compiler_params=_cparams(("parallel", "arbitrary")),
        name="band_attn",
    )(qkva, qkva, qkva, k_prev, v_prev, bias_tab)


def _lane_mean(x, mean_w):
    hi = x.astype(BF16)
    lo = (x - hi.astype(F32)).astype(BF16)
    return jnp.dot(jnp.concatenate([hi, lo], axis=1), mean_w, preferred_element_type=F32)


def _mlstm_kernel(q_ref, kt_ref, v_ref, gates_ref, og_ref, ng_ref, c0_ref, n0_ref, m0_ref,
                  mb_ref, c_out_ref, n_out_ref, m_out_ref, cn_s, m_s, *, lc, bb, nc):
    i = pl.program_id(1)
    last = pl.num_programs(1) - 1
    H, d = MLSTM_HEADS, MLSTM_HEAD_DIM

    @pl.when(i == 0)
    def _():
        m_s[...] = jnp.zeros(m_s.shape, F32)
        for bi in range(bb):
            for h in range(H):
                cn_s[bi * H + h, :, 0:d] = c0_ref[bi, h].T
                cn_s[bi * H + h, :, d:2 * d] = jnp.broadcast_to(n0_ref[bi, h:h + 1, :], (d, d)).T
            m_s[bi, 0:H, :] = jnp.broadcast_to(m0_ref[bi], (H, LANES))

    ti = lax.broadcasted_iota(jnp.int32, (lc, lc), 0)
    si = lax.broadcasted_iota(jnp.int32, (lc, lc), 1)
    causal = ti >= si
    ones_cols = jnp.ones((lc, d), BF16)
    mean_w = jnp.full((2 * d, d), 1.0 / d, BF16)

    a_rows, decays, col_sets = {}, {}, {}
    m_cur = [m_s[bi, :, 0:1] for bi in range(bb)]
    for c in range(nc):
        for bi in range(bb):
            b = gates_ref[c, bi, 0:8, :]
            a = gates_ref[c, bi, 8:16, :]
            cm = gates_ref[c, bi, 16:24, :]
            m_prev = m_cur[bi]
            inter = b + m_prev
            m_t = jnp.maximum(inter, b + cm)
            b_last = b[:, lc - 1:lc]
            m_new = jnp.maximum(b_last + m_prev, b_last + cm[:, lc - 1:lc])
            w_end = jnp.exp(b_last + a - m_new)
            stacked = jnp.concatenate(
                [b - m_t, jnp.exp(inter - m_t), jnp.exp(-m_t), w_end, jnp.zeros((LANES - 32, lc), F32)], axis=0)
            col_sets[c, bi] = stacked.T
            a_rows[c, bi] = a
            decays[c, bi] = jnp.exp(b_last + m_prev - m_new)
            m_cur[bi] = m_new
    for bi in range(bb):
        m_s[bi] = jnp.broadcast_to(m_cur[bi], (8, LANES))

    heads = [(bi, h) for bi in range(bb) for h in range(H)]
    inst = [(c, bi, h) for c in range(nc) for bi, h in heads]
    hsl = lambda h: slice(h * d, (h + 1) * d)
    rsl = lambda c: slice(c * lc, (c + 1) * lc)
    col = lambda c, bi, h, j: col_sets[c, bi][:, 8 * j + h:8 * j + h + 1]
    qs = {(c, bi, h): q_ref[bi, rsl(c), hsl(h)] for c, bi, h in inst}
    kts = {(c, bi, h): kt_ref[bi, hsl(h), rsl(c)] for c, bi, h in inst}
    vs = {(c, bi, h): v_ref[bi, rsl(c), hsl(h)] for c, bi, h in inst}

    qk = {k_: jnp.dot(qs[k_], kts[k_], preferred_element_type=F32) for k_ in inst}
    cn_cur = {(bi, h): cn_s[bi * H + h] for bi, h in heads}
    qcn = {}
    for c in range(nc):
        for bi, h in heads:
            qcn[c, bi, h] = jnp.dot(qs[c, bi, h], cn_cur[bi, h].astype(BF16), preferred_element_type=F32)
        for bi, h in heads:
            w_end_col = jnp.broadcast_to(col(c, bi, h, 3), (lc, d))
            vw = jnp.concatenate(
                [(vs[c, bi, h].astype(F32) * w_end_col).astype(BF16), w_end_col.astype(BF16)], axis=1)
            cn_cur[bi, h] = (decays[c, bi][h:h + 1, :] * cn_cur[bi, h]
                             + jnp.dot(kts[c, bi, h], vw, preferred_element_type=F32))
    for bi, h in heads:
        cn_s[bi * H + h] = cn_cur[bi, h]
    w = {(c, bi, h): jnp.exp(jnp.where(causal, col(c, bi, h, 0) + a_rows[c, bi][h:h + 1, :], -jnp.inf))
         * qk[c, bi, h] for c, bi, h in inst}
    wv = {k_: jnp.dot(w[k_].astype(BF16), jnp.concatenate([vs[k_], ones_cols], axis=1),
                      preferred_element_type=F32) for k_ in inst}
    hh = {}
    for c, bi, h in inst:
        tot = col(c, bi, h, 1) * qcn[c, bi, h] + wv[c, bi, h]
        den = tot[:, d:2 * d]
        hh[c, bi, h] = tot[:, 0:d] * (1.0 / jnp.maximum(jnp.abs(den), col(c, bi, h, 2)))
    xc = {k_: hh[k_] - _lane_mean(hh[k_], mean_w) for k_ in inst}
    var = {k_: _lane_mean(xc[k_] * xc[k_], mean_w) for k_ in inst}
    for c, bi, h in inst:
        hn = xc[c, bi, h] * lax.rsqrt(var[c, bi, h] + LN_EPS) * ng_ref[:, hsl(h)]
        mb_ref[bi, rsl(c), hsl(h)] = (og_ref[bi, rsl(c), hsl(h)] * hn).astype(BF16)

    @pl.when(i == last)
    def _():
        for bi in range(bb):
            for h in range(H):
                cn = cn_s[bi * H + h]
                c_out_ref[bi, h] = cn[:, 0:d].T
                n_out_ref[bi, h:h + 1, :] = cn[:, d:2 * d].T[0:1, :]
            m_out_ref[bi] = m_s[bi, 0:H, 0:1]


def _mlstm(qvm, kt, gates, og, norm_g, c0, n0, m0, *, lc, bb, nc):
    B, T, _ = qvm.shape
    ts = nc * lc
    nt = T // ts
    W = MLSTM_WIDTH
    H, d = MLSTM_HEADS, MLSTM_HEAD_DIM
    kern = functools.partial(_mlstm_kernel, lc=lc, bb=bb, nc=nc)
    return pl.pallas_call(
        kern,
        grid=(B // bb, nt),
        in_specs=[
            pl.BlockSpec((bb, ts, W), lambda b, i: (b, i, 0)),
            pl.BlockSpec((bb, W, ts), lambda b, i: (b, 0, i)),
            pl.BlockSpec((bb, ts, W), lambda b, i: (b, i, 1)),
            pl.BlockSpec((nc, bb, GATE_ROWS, lc), lambda b, i: (i, b, 0, 0)),
            pl.BlockSpec((bb, ts, W), lambda b, i: (b, i, 0)),
            pl.BlockSpec((1, W), lambda b, i: (0, 0)),
            pl.BlockSpec((bb, H, d, d), lambda b, i: (b, 0, 0, 0)),
            pl.BlockSpec((bb, H, d), lambda b, i: (b, 0, 0)),
            pl.BlockSpec((bb, H, 1), lambda b, i: (b, 0, 0)),
        ],
        out_specs=[
            pl.BlockSpec((bb, ts, W), lambda b, i: (b, i, 0)),
            pl.BlockSpec((bb, H, d, d), lambda b, i: (b, 0, 0, 0)),
            pl.BlockSpec((bb, H, d), lambda b, i: (b, 0, 0)),
            pl.BlockSpec((bb, H, 1), lambda b, i: (b, 0, 0)),
        ],
        out_shape=[
            jax.ShapeDtypeStruct((B, T, W), BF16),
            jax.ShapeDtypeStruct((B, H, d, d), F32),
            jax.ShapeDtypeStruct((B, H, d), F32),
            jax.ShapeDtypeStruct((B, H, 1), F32),
        ],
        scratch_shapes=[
            pltpu.VMEM((bb * H, d, 2 * d), F32),
            pltpu.VMEM((bb, 8, LANES), F32),
        ],
        compiler_params=_cparams(("parallel", "arbitrary")),
        name="mlstm",
    )(qvm, kt, qvm, gates, og, norm_g, c0, n0, m0)


def _layer_norm(z, g, b):
    mu = jnp.mean(z, axis=-1, keepdims=True)
    zc = z - mu
    var = jnp.mean(zc * zc, axis=-1, keepdims=True)
    return zc * lax.rsqrt(var + LN_EPS) * g + b


def _merge_ffn_kernel(x_ref, att_ref, mb_ref, wo_ref, g1_ref, b1_ref, w1_ref, bf1_ref, w2_ref, bf2_ref,
                      g2_ref, b2_ref, y_ref, *, fc, parts):
    W = ATT_WIDTH
    tm = x_ref.shape[1]
    rp = tm // parts
    groups = [slice(p * rp, (p + 1) * rp) for p in range(parts)]
    mixes = [jnp.dot(att_ref[0, rs, :], wo_ref[0:W, :], preferred_element_type=F32)
             + jnp.dot(mb_ref[0, rs, :], wo_ref[W:2 * W, :], preferred_element_type=F32) for rs in groups]
    for rs, mix in zip(groups, mixes):
        h = _layer_norm(DEEPNORM_ALPHA * x_ref[0, rs, :] + mix, g1_ref[...], b1_ref[...])
        hb = h.astype(BF16)
        f = jnp.zeros(h.shape, F32)
        for j in range(D_FF // fc):
            cs = slice(j * fc, (j + 1) * fc)
            a = jnp.dot(hb, w1_ref[:, cs], preferred_element_type=F32) + bf1_ref[:, cs]
            a = jnp.square(jnp.maximum(a, 0.0)).astype(BF16)
            f = f + jnp.dot(a, w2_ref[cs, :], preferred_element_type=F32)
        f = f + bf2_ref[...]
        y_ref[0, rs, :] = _layer_norm(DEEPNORM_ALPHA * h + f, g2_ref[...], b2_ref[...])


def _merge_ffn(x, att, mb, wo, g1, b1, w1, bf1, w2, bf2, g2, b2, *, tm, fc, parts):
    B, T, D = x.shape
    nt = T // tm
    W = ATT_WIDTH
    const = lambda shape: pl.BlockSpec(shape, lambda b, i: (0, 0), pipeline_mode=pl.Buffered(1))
    kern = functools.partial(_merge_ffn_kernel, fc=fc, parts=parts)
    return pl.pallas_call(
        kern,
        grid=(B, nt),
        in_specs=[
            pl.BlockSpec((1, tm, D), lambda b, i: (b, i, 0)),
            pl.BlockSpec((1, tm, W), lambda b, i: (b, i, 0)),
            pl.BlockSpec((1, tm, W), lambda b, i: (b, i, 0)),
            const((D, D)), const((1, D)), const((1, D)),
            const((D, D_FF)), const((1, D_FF)), const((D_FF, D)), const((1, D)),
            const((1, D)), const((1, D)),
        ],
        out_specs=pl.BlockSpec((1, tm, D), lambda b, i: (b, i, 0)),
        out_shape=jax.ShapeDtypeStruct((B, T, D), F32),
        compiler_params=_cparams(("parallel", "parallel")),
        name="merge_ffn",
    )(x, att, mb, wo, g1, b1, w1, bf1, w2, bf2, g2, b2)


def _token_tile(t):
    return 512 if t % 512 == 0 else t


def kernel(x_prompt, x_sample, cache_k, cache_v, state_C, state_n, state_m, w_in, b_in, rel_bias,
           mlstm_norm_g, w_out, ln1_g, ln1_b, w_ff1, b_ff1, w_ff2, b_ff2, ln2_g, ln2_b):
    assert w_in.shape[0] == DEPTH == 1
    B, S, D = x_prompt.shape
    DB, T, _ = x_sample.shape
    H, d = MLSTM_HEADS, MLSTM_HEAD_DIM
    W = MLSTM_WIDTH
    assert S % ATT_REACH == 0 and T == CHUNK and cache_k.shape[2] == ATT_REACH

    n_gate = 2 * H
    wi, bi_ = w_in[0], b_in[0]
    w_all = wi.astype(BF16)
    b_all = bi_[None, :]
    gate_pad = GATE_ROWS_PAD - n_gate
    wkt = _key_gate_weights(wi)
    bkt = jnp.pad(jnp.concatenate([bi_[OFF_KB:OFF_VB], bi_[OFF_GATES:]]), (0, gate_pad))[:, None]
    wo = w_out[0].astype(BF16)
    w1 = w_ff1[0].astype(BF16)
    w2 = w_ff2[0].astype(BF16)
    row = lambda p: p[0][None, :]
    ffn_params = (wo, row(ln1_g), row(ln1_b), w1, row(b_ff1), w2, row(b_ff2), row(ln2_g), row(ln2_b))
    norm_g = row(mlstm_norm_g)

    def layer(x, k_prev, v_prev, prev_map, c0, n0, m0, *, seqs, cps, mask_first):
        Bx, Tx, _ = x.shape
        n_tok = Bx * Tx
        t_seq = n_tok // seqs
        tm = _token_tile(Tx)
        keep = min(ATT_REACH, Tx)
        lc = min(MLSTM_CHUNK, t_seq)
        qkva, qvm, kt, og, gates, k_last, v_last = _in_proj(x, w_all, b_all, wkt, bkt, tm=tm, keep=keep, lc=lc)
        seq = lambda a: a.reshape(seqs, t_seq, a.shape[-1])
        tq = ATTN_TILE if t_seq % ATTN_TILE == 0 else _token_tile(t_seq)
        keys = -(-(ATT_REACH + cps * CHUNK) // LANES) * LANES
        bias_tab = _bias_table(rel_bias[0], cps=cps, keys=keys)
        if k_prev is None:
            k_prev = v_prev = seq(qkva)
            step = tq // ATT_REACH
            prev_map = (lambda b, i: (b, jnp.maximum(step * i - 1, 0), 1),
                        lambda b, i: (b, jnp.maximum(step * i - 1, 0), 2))
        att = _band_attn(seq(qkva), k_prev, v_prev, prev_map, bias_tab,
                         tq=tq, cps=cps, keys=keys, mask_first=mask_first)
        gates = gates.reshape(t_seq // lc, seqs, GATE_ROWS, lc)
        kt = kt.reshape(Bx, W, seqs // Bx, t_seq).transpose(0, 2, 1, 3).reshape(seqs, W, t_seq)
        mb, c_new, n_new, m_new = _mlstm(seq(qvm), kt, gates, seq(og), norm_g, c0, n0, m0,
                                         lc=lc, bb=min(seqs, MLSTM_STREAMS),
                                         nc=min(MLSTM_CHUNKS_PER_STEP, t_seq // lc))
        tok = lambda a: a.reshape(Bx, Tx, a.shape[-1])
        tf = FFN_TILE if Tx % FFN_TILE == 0 else tm
        y = _merge_ffn(x, tok(att), tok(mb), *ffn_params, tm=tf, fc=2048, parts=max(tf // FFN_GROUP, 1))
        return y, k_last, v_last, c_new, n_new, m_new

    zeros_c = jnp.zeros((B, H, d, d), F32)
    zeros_n = jnp.zeros((B, H, d), F32)
    zeros_m = jnp.zeros((B, H, 1), F32)
    yp, kp, vp, cp, np_, mp = layer(x_prompt, None, None, None, zeros_c, zeros_n, zeros_m,
                                    seqs=B, cps=1, mask_first=True)

    ck = cache_k[0].reshape(DB, ATT_REACH, ATT_WIDTH)
    cv = cache_v[0].reshape(DB, ATT_REACH, ATT_WIDTH)
    prev_sample = (lambda b, i: (b, 0, 0), lambda b, i: (b, 0, 0))
    ys, ks, vs, cs, ns, ms = layer(
        x_sample.reshape(1, DB * T, D), ck, cv, prev_sample,
        state_C[0].astype(F32), state_n[0].astype(F32), state_m[0].astype(F32).reshape(DB, H, 1),
        seqs=DB, cps=1, mask_first=False)

    sd = state_C.dtype
    heads = lambda a, nb, t: a.reshape(nb, t, ATT_HEADS, ATT_HEAD_DIM)[None]
    keep_p = min(ATT_REACH, S)
    return (yp, ys.reshape(DB, T, D),
            heads(kp, B, keep_p).astype(cache_k.dtype), heads(vp, B, keep_p).astype(cache_v.dtype),
            cp[None].astype(sd), np_[None].astype(sd), mp.reshape(1, B, H).astype(sd),
            heads(ks, DB, T).astype(cache_k.dtype), heads(vs, DB, T).astype(cache_v.dtype),
            cs[None].astype(sd), ns[None].astype(sd), ms.reshape(1, DB, H).astype(sd))
```

```python
import functools

import jax
import jax.numpy as jnp
from jax import lax
from jax.experimental import pallas as pl
from jax.experimental.pallas import tpu as pltpu

F32 = jnp.float32
BF16 = jnp.bfloat16

D_MODEL = 1024
CHUNK = 64
LEFT_CHUNKS = 8
ATT_REACH = LEFT_CHUNKS * CHUNK
ATT_WIDTH = 512
MLSTM_WIDTH = 512
ATT_HEADS = 8
ATT_HEAD_DIM = 64
MLSTM_HEADS = 4
MLSTM_HEAD_DIM = 128
MAX_REL = 128
D_FF = 4 * D_MODEL
DEPTH = 1
DEEPNORM_ALPHA = (2 * DEPTH) ** 0.25
LN_EPS = 1e-5
NEG_INF = -1e30
ATT_SCALE = ATT_HEAD_DIM ** -0.5
KB_SCALE = MLSTM_HEAD_DIM ** -0.5

OFF_QB = 3 * ATT_WIDTH
OFF_KB = OFF_QB + MLSTM_WIDTH
OFF_VB = OFF_KB + MLSTM_WIDTH
OFF_OB = OFF_VB + MLSTM_WIDTH
OFF_GATES = OFF_OB + MLSTM_WIDTH

LANES = 128
GATE_ROWS_PAD = 16
MLSTM_CHUNK = 256
MLSTM_CHUNKS_PER_STEP = 4
MLSTM_STREAMS = 2
GATE_ROWS = 24
ATTN_TILE = 1024
FFN_TILE = 1024
FFN_GROUP = 256
HEAD_PAIRS = ATT_HEADS // 2
VMEM_LIMIT = 56 * 1024 * 1024
NT_DIMS = (((1,), (1,)), ((), ()))


def _cparams(sem):
    return pltpu.CompilerParams(dimension_semantics=sem, vmem_limit_bytes=VMEM_LIMIT)


def _in_proj_kernel(x_ref, w_ref, b_ref, wkt_ref, bkt_ref,
                    qkva_ref, qvm_ref, kt_ref, og_ref, gates_ref, klast_ref, vlast_ref, *, tm, keep, lc):
    i = pl.program_id(1)
    last = pl.num_programs(1) - 1
    xb = x_ref[0].astype(BF16)

    def proj(c0, width):
        return (jnp.dot(xb, w_ref[:, c0:c0 + width], preferred_element_type=F32)
                + b_ref[:, c0:c0 + width])

    W = ATT_WIDTH
    H = MLSTM_HEADS
    ktg = lax.dot_general(wkt_ref[...], xb, NT_DIMS, preferred_element_type=F32) + bkt_ref[...]
    kt_ref[0] = (ktg[0:W] * KB_SCALE).astype(BF16)
    g = ktg[W:W + 2 * H]
    log_sig = jnp.minimum(g, 0.0) - jnp.log1p(jnp.exp(-jnp.abs(g)))
    head_row = lax.broadcasted_iota(jnp.int32, g.shape, 0) < H
    g8 = jnp.where(head_row, g, log_sig)
    pos = lax.broadcasted_iota(jnp.int32, g8.shape, 1) % lc
    csum = g8
    sh = 1
    while sh < lc:
        csum = csum + jnp.where(pos >= sh, pltpu.roll(csum, sh, 1), 0.0)
        sh *= 2
    b = pltpu.roll(csum, H, 0)
    a = g8 - b
    cm = a
    sh = 1
    while sh < lc:
        cm = jnp.maximum(cm, jnp.where(pos >= sh, pltpu.roll(cm, sh, 1), -jnp.inf))
        sh *= 2
    rows = [jnp.where(head_row, t, 0.0) for t in (b, a, cm)]
    for c in range(tm // lc):
        for j, t in enumerate(rows):
            gates_ref[c, 0, 8 * j:8 * j + 8, :] = t[:, c * lc:(c + 1) * lc]

    qvm_ref[0, :, 0:W] = proj(OFF_QB, W).astype(BF16)
    qvm_ref[0, :, W:2 * W] = proj(OFF_VB, W).astype(BF16)
    ob = proj(OFF_OB, W)
    og_ref[0] = 1.0 / (1.0 + jnp.exp(-ob))
    qa = proj(0, W)
    qkva_ref[0, :, 0:W] = (qa * ATT_SCALE).astype(BF16)
    ka = proj(W, W)
    qkva_ref[0, :, W:2 * W] = ka.astype(BF16)
    va = proj(2 * W, W)
    qkva_ref[0, :, 2 * W:3 * W] = va.astype(BF16)

    @pl.when(i == last)
    def _():
        klast_ref[0] = ka[tm - keep:, :].T
        vlast_ref[0] = va[tm - keep:, :].T


def _in_proj(x, w, b, wkt, bkt, *, tm, keep, lc):
    B, T, D = x.shape
    nt = T // tm
    W = ATT_WIDTH
    kern = functools.partial(_in_proj_kernel, tm=tm, keep=keep, lc=lc)
    return pl.pallas_call(
        kern,
        grid=(B, nt),
        in_specs=[
            pl.BlockSpec((1, tm, D), lambda b, i: (b, i, 0)),
            pl.BlockSpec(w.shape, lambda b, i: (0, 0)),
            pl.BlockSpec(b.shape, lambda b, i: (0, 0)),
            pl.BlockSpec(wkt.shape, lambda b, i: (0, 0)),
            pl.BlockSpec(bkt.shape, lambda b, i: (0, 0)),
        ],
        out_specs=[
            pl.BlockSpec((1, tm, 3 * W), lambda b, i: (b, i, 0)),
            pl.BlockSpec((1, tm, 2 * W), lambda b, i: (b, i, 0)),
            pl.BlockSpec((1, W, tm), lambda b, i: (b, 0, i)),
            pl.BlockSpec((1, tm, W), lambda b, i: (b, i, 0)),
            pl.BlockSpec((tm // lc, 1, GATE_ROWS, lc), lambda b, i: (i, b, 0, 0)),
            pl.BlockSpec((1, W, keep), lambda b, i: (b, 0, 0)),
            pl.BlockSpec((1, W, keep), lambda b, i: (b, 0, 0)),
        ],
        out_shape=[
            jax.ShapeDtypeStruct((B, T, 3 * W), BF16),
            jax.ShapeDtypeStruct((B, T, 2 * W), BF16),
            jax.ShapeDtypeStruct((B, W, T), BF16),
            jax.ShapeDtypeStruct((B, T, W), F32),
            jax.ShapeDtypeStruct((T // lc, B, GATE_ROWS, lc), F32),
            jax.ShapeDtypeStruct((B, W, keep), F32),
            jax.ShapeDtypeStruct((B, W, keep), F32),
        ],
        compiler_params=_cparams(("parallel", "arbitrary")),
        name="in_proj",
    )(x, w, b, wkt, bkt)


def _proj_weights_kernel(wt_ref, wg_ref, w_ref, wkt_ref):
    j = pl.program_id(0)
    W = wt_ref.shape[0]
    w_ref[...] = wt_ref[...].T.astype(BF16)

    @pl.when(j == OFF_KB // W)
    def _():
        wkt_ref[0:W, :] = wt_ref[...].astype(BF16)

    @pl.when(j == 0)
    def _():
        pad = jnp.zeros((GATE_ROWS_PAD - wg_ref.shape[0], wg_ref.shape[1]), F32)
        wkt_ref[W:W + GATE_ROWS_PAD, :] = jnp.concatenate([wg_ref[...], pad], axis=0).astype(BF16)


def _proj_weights(wt):
    n_in, D = wt.shape
    W = MLSTM_WIDTH
    n_gate = 2 * MLSTM_HEADS
    return pl.pallas_call(
        _proj_weights_kernel,
        grid=(OFF_GATES // W,),
        in_specs=[pl.BlockSpec((W, D), lambda j: (j, 0)),
                  pl.BlockSpec((n_gate, D), lambda j: (OFF_GATES // n_gate, 0))],
        out_specs=[pl.BlockSpec((D, W), lambda j: (0, j)),
                   pl.BlockSpec((W + GATE_ROWS_PAD, D), lambda j: (0, 0))],
        out_shape=[jax.ShapeDtypeStruct((D, OFF_GATES), BF16),
                   jax.ShapeDtypeStruct((W + GATE_ROWS_PAD, D), BF16)],
        compiler_params=_cparams(("arbitrary",)),
        name="proj_weights",
    )(wt, wt)


def _bias_table_kernel(g_ref, o_ref, *, rows, keys):
    L = g_ref.shape[-1]
    r = lax.broadcasted_iota(jnp.int32, (rows, keys), 0)
    c = lax.broadcasted_iota(jnp.int32, (rows, keys), 1)
    band_start = (r // CHUNK) * CHUNK
    in_band = (c >= band_start) & (c < band_start + ATT_REACH + CHUNK)
    for h in range(ATT_HEADS):
        g = jnp.broadcast_to(g_ref[h], (rows, L))
        t = pltpu.roll(g, L - rows, 1, stride=1, stride_axis=0)
        tab = jnp.where(in_band, t[:, :keys], NEG_INF)
        o_ref[h // 2, (h % 2) * rows:(h % 2 + 1) * rows, :] = tab


def _bias_table(rel_bias, *, cps, keys):
    rows = cps * CHUNK
    L = ((rows + keys + LANES - 1) // LANES) * LANES
    n_const = rows + ATT_REACH - MAX_REL
    const = jnp.broadcast_to(rel_bias[:, 2 * MAX_REL:], (ATT_HEADS, n_const))
    ramp = rel_bias[:, ::-1]
    tail = jnp.broadcast_to(rel_bias[:, :1], (ATT_HEADS, LANES))
    g = jnp.concatenate([const, ramp, tail], axis=1)[:, :L].reshape(ATT_HEADS, 1, L)
    kern = functools.partial(_bias_table_kernel, rows=rows, keys=keys)
    return pl.pallas_call(
        kern,
        out_shape=jax.ShapeDtypeStruct((HEAD_PAIRS, 2 * rows, keys), F32),
        name="bias_table",
    )(g)


def _attn_kernel(q_ref, kc_ref, vc_ref, kp_ref, vp_ref, bias_ref, o_ref, kk_ref, vv_ref,
                 *, tq, cps, keys, mask_first, past_feature_major):
    i = pl.program_id(1)
    rows = cps * CHUNK
    if past_feature_major:
        kk_ref[0:ATT_REACH, :] = kp_ref[0].T.astype(BF16)
        vv_ref[0:ATT_REACH, :] = vp_ref[0].T.astype(BF16)
    else:
        kk_ref[0:ATT_REACH, :] = kp_ref[0].astype(BF16)
        vv_ref[0:ATT_REACH, :] = vp_ref[0].astype(BF16)
    kk_ref[ATT_REACH:ATT_REACH + tq, :] = kc_ref[0]
    vv_ref[ATT_REACH:ATT_REACH + tq, :] = vc_ref[0]
    total = kk_ref.shape[0]
    if total > ATT_REACH + tq:
        kk_ref[ATT_REACH + tq:, :] = jnp.zeros((total - ATT_REACH - tq, ATT_WIDTH), BF16)
        vv_ref[ATT_REACH + tq:, :] = jnp.zeros((total - ATT_REACH - tq, ATT_WIDTH), BF16)

    lane = lax.broadcasted_iota(jnp.int32, (rows, LANES), 1)
    first_head = lane < ATT_HEAD_DIM

    def tile(no_past):
        for sub in range(tq // rows):
            off = sub * rows
            first_valid = max(ATT_REACH - off, 0) if no_past else 0
            k0 = first_valid // LANES * LANES
            ones_cols = jnp.ones((keys - k0, LANES), BF16)
            for pair in range(HEAD_PAIRS):
                ls = slice(pair * LANES, (pair + 1) * LANES)
                q2 = q_ref[0, off:off + rows, ls]
                zero = jnp.zeros_like(q2)
                qs = jnp.concatenate([jnp.where(first_head, q2, zero), jnp.where(first_head, zero, q2)], axis=0)
                s = lax.dot_general(qs, kk_ref[off + k0:off + keys, ls], NT_DIMS,
                                    preferred_element_type=F32)
                s = s + bias_ref[pair, :, k0:keys]
                if first_valid > k0:
                    col = lax.broadcasted_iota(jnp.int32, s.shape, 1)
                    s = jnp.where(col >= first_valid - k0, s, NEG_INF)
                m = jnp.max(s, axis=-1, keepdims=True)
                e = jnp.exp(s - m).astype(BF16)
                v_ext = jnp.concatenate([vv_ref[off + k0:off + keys, ls], ones_cols], axis=1)
                o2 = jnp.dot(e, v_ext, preferred_element_type=F32)
                o2 = o2[:, 0:LANES] * (1.0 / o2[:, LANES:2 * LANES])
                o = jnp.where(first_head, o2[:rows], o2[rows:])
                o_ref[0, off:off + rows, ls] = o.astype(BF16)

    if mask_first:
        pl.when(i == 0)(lambda: tile(True))
        pl.when(i > 0)(lambda: tile(False))
    else:
        tile(False)


def _band_attn(qkva, k_prev, v_prev, prev_map, bias_tab, *, tq, cps, keys, mask_first, past_feature_major):
    B, T, _ = qkva.shape
    nt = T // tq
    rows_total = max(ATT_REACH + tq, (tq // (cps * CHUNK) - 1) * cps * CHUNK + keys)
    kern = functools.partial(_attn_kernel, tq=tq, cps=cps, keys=keys, mask_first=mask_first,
                             past_feature_major=past_feature_major)
    W = ATT_WIDTH
    prev_block = (1, ATT_REACH, W)
    return pl.pallas_call(
        kern,
        grid=(B, nt),
        in_specs=[
            pl.BlockSpec((1, tq, W), lambda b, i: (b, i, 0)),
            pl.BlockSpec((1, tq, W), lambda b, i: (b, i, 1)),
            pl.BlockSpec((1, tq, W), lambda b, i: (b, i, 2)),
            pl.BlockSpec(prev_block, prev_map[0]),
            pl.BlockSpec(prev_block, prev_map[1]),
            pl.BlockSpec(bias_tab.shape, lambda b, i: (0, 0, 0)),
        ],
        out_specs=pl.BlockSpec((1, tq, W), lambda b, i: (b, i, 0)),
        out_shape=jax.ShapeDtypeStruct((B, T, W), BF16),
        scratch_shapes=[pltpu.VMEM((rows_total, W), BF16), pltpu.VMEM((rows_total, W), BF16)],
        compiler_params=_cparams(("parallel", "arbitrary")),
        name="band_attn",
    )(qkva, qkva, qkva, k_prev, v_prev, bias_tab)


def _lane_mean(x, mean_w):
    hi = x.astype(BF16)
    lo = (x - hi.astype(F32)).astype(BF16)
    return jnp.dot(jnp.concatenate([hi, lo], axis=1), mean_w, preferred_element_type=F32)


def _mlstm_kernel(q_ref, kt_ref, v_ref, gates_ref, og_ref, ng_ref, c0_ref, n0_ref, m0_ref,
                  mb_ref, c_out_ref, n_out_ref, m_out_ref, cn_s, m_s, *, lc, bb, nc):
    i = pl.program_id(1)
    last = pl.num_programs(1) - 1
    H, d = MLSTM_HEADS, MLSTM_HEAD_DIM

    @pl.when(i == 0)
    def _():
        m_s[...] = jnp.zeros(m_s.shape, F32)
        for bi in range(bb):
            for h in range(H):
                cn_s[bi * H + h, :, 0:d] = c0_ref[bi, h].T
                cn_s[bi * H + h, :, d:2 * d] = jnp.broadcast_to(n0_ref[bi, h:h + 1, :], (d, d)).T
            m_s[bi, 0:H, :] = jnp.broadcast_to(m0_ref[bi], (H, LANES))

    ti = lax.broadcasted_iota(jnp.int32, (lc, lc), 0)
    si = lax.broadcasted_iota(jnp.int32, (lc, lc), 1)
    causal = ti >= si
    ones_cols = jnp.ones((lc, d), BF16)
    mean_w = jnp.full((2 * d, d), 1.0 / d, BF16)

    a_rows, decays, col_sets = {}, {}, {}
    m_cur = [m_s[bi, :, 0:1] for bi in range(bb)]
    for c in range(nc):
        for bi in range(bb):
            b = gates_ref[c, bi, 0:8, :]
            a = gates_ref[c, bi, 8:16, :]
            cm = gates_ref[c, bi, 16:24, :]
            m_prev = m_cur[bi]
            inter = b + m_prev
            m_t = jnp.maximum(inter, b + cm)
            b_last = b[:, lc - 1:lc]
            m_new = jnp.maximum(b_last + m_prev, b_last + cm[:, lc - 1:lc])
            w_end = jnp.exp(b_last + a - m_new)
            stacked = jnp.concatenate(
                [b - m_t, jnp.exp(inter - m_t), jnp.exp(-m_t), w_end, jnp.zeros((LANES - 32, lc), F32)], axis=0)
            col_sets[c, bi] = stacked.T
            a_rows[c, bi] = a
            decays[c, bi] = jnp.exp(b_last + m_prev - m_new)
            m_cur[bi] = m_new
    for bi in range(bb):
        m_s[bi] = jnp.broadcast_to(m_cur[bi], (8, LANES))

    heads = [(bi, h) for bi in range(bb) for h in range(H)]
    inst = [(c, bi, h) for c in range(nc) for bi, h in heads]
    hsl = lambda h: slice(h * d, (h + 1) * d)
    rsl = lambda c: slice(c * lc, (c + 1) * lc)
    col = lambda c, bi, h, j: col_sets[c, bi][:, 8 * j + h:8 * j + h + 1]
    qs = {(c, bi, h): q_ref[bi, rsl(c), hsl(h)] for c, bi, h in inst}
    kts = {(c, bi, h): kt_ref[bi, hsl(h), rsl(c)] for c, bi, h in inst}
    vs = {(c, bi, h): v_ref[bi, rsl(c), hsl(h)] for c, bi, h in inst}

    qk = {k_: jnp.dot(qs[k_], kts[k_], preferred_element_type=F32) for k_ in inst}
    cn_cur = {(bi, h): cn_s[bi * H + h] for bi, h in heads}
    qcn = {}
    for c in range(nc):
        for bi, h in heads:
            qcn[c, bi, h] = jnp.dot(qs[c, bi, h], cn_cur[bi, h].astype(BF16), preferred_element_type=F32)
        for bi, h in heads:
            w_end_col = jnp.broadcast_to(col(c, bi, h, 3), (lc, d))
            vw = jnp.concatenate(
                [(vs[c, bi, h].astype(F32) * w_end_col).astype(BF16), w_end_col.astype(BF16)], axis=1)
            cn_cur[bi, h] = (decays[c, bi][h:h + 1, :] * cn_cur[bi, h]
                             + jnp.dot(kts[c, bi, h], vw, preferred_element_type=F32))
    for bi, h in heads:
        cn_s[bi * H + h] = cn_cur[bi, h]
    w = {(c, bi, h): jnp.exp(jnp.where(causal, col(c, bi, h, 0) + a_rows[c, bi][h:h + 1, :], -jnp.inf))
         * qk[c, bi, h] for c, bi, h in inst}
    wv = {k_: jnp.dot(w[k_].astype(BF16), jnp.concatenate([vs[k_], ones_cols], axis=1),
                      preferred_element_type=F32) for k_ in inst}
    hh = {}
    for c, bi, h in inst:
        tot = col(c, bi, h, 1) * qcn[c, bi, h] + wv[c, bi, h]
        den = tot[:, d:2 * d]
        hh[c, bi, h] = tot[:, 0:d] * (1.0 / jnp.maximum(jnp.abs(den), col(c, bi, h, 2)))
    xc = {k_: hh[k_] - _lane_mean(hh[k_], mean_w) for k_ in inst}
    var = {k_: _lane_mean(xc[k_] * xc[k_], mean_w) for k_ in inst}
    for c, bi, h in inst:
        hn = xc[c, bi, h] * lax.rsqrt(var[c, bi, h] + LN_EPS) * ng_ref[:, hsl(h)]
        mb_ref[bi, rsl(c), hsl(h)] = (og_ref[bi, rsl(c), hsl(h)] * hn).astype(BF16)

    @pl.when(i == last)
    def _():
        for bi in range(bb):
            for h in range(H):
                cn = cn_s[bi * H + h]
                c_out_ref[bi, h] = cn[:, 0:d].T
                n_out_ref[bi, h:h + 1, :] = cn[:, d:2 * d].T[0:1, :]
            m_out_ref[bi] = m_s[bi, 0:H, 0:1]


def _mlstm(qvm, kt, gates, og, norm_g, c0, n0, m0, *, lc, bb, nc):
    B, T, _ = qvm.shape
    ts = nc * lc
    nt = T // ts
    W = MLSTM_WIDTH
    H, d = MLSTM_HEADS, MLSTM_HEAD_DIM
    kern = functools.partial(_mlstm_kernel, lc=lc, bb=bb, nc=nc)
    return pl.pallas_call(
        kern,
        grid=(B // bb, nt),
        in_specs=[
            pl.BlockSpec((bb, ts, W), lambda b, i: (b, i, 0)),
            pl.BlockSpec((bb, W, ts), lambda b, i: (b, 0, i)),
            pl.BlockSpec((bb, ts, W), lambda b, i: (b, i, 1)),
            pl.BlockSpec((nc, bb, GATE_ROWS, lc), lambda b, i: (i, b, 0, 0)),
            pl.BlockSpec((bb, ts, W), lambda b, i: (b, i, 0)),
            pl.BlockSpec((1, W), lambda b, i: (0, 0)),
            pl.BlockSpec((bb, H, d, d), lambda b, i: (b, 0, 0, 0)),
            pl.BlockSpec((bb, H, d), lambda b, i: (b, 0, 0)),
            pl.BlockSpec((bb, H, 1), lambda b, i: (b, 0, 0)),
        ],
        out_specs=[
            pl.BlockSpec((bb, ts, W), lambda b, i: (b, i, 0)),
            pl.BlockSpec((bb, H, d, d), lambda b, i: (b, 0, 0, 0)),
            pl.BlockSpec((bb, H, d), lambda b, i: (b, 0, 0)),
            pl.BlockSpec((bb, H, 1), lambda b, i: (b, 0, 0)),
        ],
        out_shape=[
            jax.ShapeDtypeStruct((B, T, W), BF16),
            jax.ShapeDtypeStruct((B, H, d, d), F32),
            jax.ShapeDtypeStruct((B, H, d), F32),
            jax.ShapeDtypeStruct((B, H, 1), F32),
        ],
        scratch_shapes=[
            pltpu.VMEM((bb * H, d, 2 * d), F32),
            pltpu.VMEM((bb, 8, LANES), F32),
        ],
        compiler_params=_cparams(("parallel", "arbitrary")),
        name="mlstm",
    )(qvm, kt, qvm, gates, og, norm_g, c0, n0, m0)


def _layer_norm(z, g, b):
    mu = jnp.mean(z, axis=-1, keepdims=True)
    zc = z - mu
    var = jnp.mean(zc * zc, axis=-1, keepdims=True)
    return zc * lax.rsqrt(var + LN_EPS) * g + b


def _merge_ffn_kernel(x_ref, att_ref, mb_ref, wo_ref, g1_ref, b1_ref, w1_ref, bf1_ref, w2_ref, bf2_ref,
                      g2_ref, b2_ref, y_ref, *, fc, parts):
    W = ATT_WIDTH
    tm = x_ref.shape[1]
    rp = tm // parts
    groups = [slice(p * rp, (p + 1) * rp) for p in range(parts)]
    mixes = [jnp.dot(att_ref[0, rs, :], wo_ref[0:W, :], preferred_element_type=F32)
             + jnp.dot(mb_ref[0, rs, :], wo_ref[W:2 * W, :], preferred_element_type=F32) for rs in groups]
    for rs, mix in zip(groups, mixes):
        h = _layer_norm(DEEPNORM_ALPHA * x_ref[0, rs, :] + mix, g1_ref[...], b1_ref[...])
        hb = h.astype(BF16)
        f = jnp.zeros(h.shape, F32)
        for j in range(D_FF // fc):
            cs = slice(j * fc, (j + 1) * fc)
            a = jnp.dot(hb, w1_ref[:, cs], preferred_element_type=F32) + bf1_ref[:, cs]
            a = jnp.square(jnp.maximum(a, 0.0)).astype(BF16)
            f = f + jnp.dot(a, w2_ref[cs, :], preferred_element_type=F32)
        f = f + bf2_ref[...]
        y_ref[0, rs, :] = _layer_norm(DEEPNORM_ALPHA * h + f, g2_ref[...], b2_ref[...])


def _merge_ffn(x, att, mb, wo, g1, b1, w1, bf1, w2, bf2, g2, b2, *, tm, fc, parts):
    B, T, D = x.shape
    nt = T // tm
    W = ATT_WIDTH
    const = lambda shape: pl.BlockSpec(shape, lambda b, i: (0, 0), pipeline_mode=pl.Buffered(1))
    kern = functools.partial(_merge_ffn_kernel, fc=fc, parts=parts)
    return pl.pallas_call(
        kern,
        grid=(B, nt),
        in_specs=[
            pl.BlockSpec((1, tm, D), lambda b, i: (b, i, 0)),
            pl.BlockSpec((1, tm, W), lambda b, i: (b, i, 0)),
            pl.BlockSpec((1, tm, W), lambda b, i: (b, i, 0)),
            const((D, D)), const((1, D)), const((1, D)),
            const((D, D_FF)), const((1, D_FF)), const((D_FF, D)), const((1, D)),
            const((1, D)), const((1, D)),
        ],
        out_specs=pl.BlockSpec((1, tm, D), lambda b, i: (b, i, 0)),
        out_shape=jax.ShapeDtypeStruct((B, T, D), F32),
        compiler_params=_cparams(("parallel", "parallel")),
        name="merge_ffn",
    )(x, att, mb, wo, g1, b1, w1, bf1, w2, bf2, g2, b2)


def _token_tile(t):
    return 512 if t % 512 == 0 else t


def kernel(x_prompt, x_sample, cache_k, cache_v, state_C, state_n, state_m, w_in, b_in, rel_bias,
           mlstm_norm_g, w_out, ln1_g, ln1_b, w_ff1, b_ff1, w_ff2, b_ff2, ln2_g, ln2_b):
    assert w_in.shape[0] == DEPTH == 1
    B, S, D = x_prompt.shape
    DB, T, _ = x_sample.shape
    H, d = MLSTM_HEADS, MLSTM_HEAD_DIM
    W = MLSTM_WIDTH
    assert S % ATT_REACH == 0 and T == CHUNK and cache_k.shape[2] == ATT_REACH

    n_gate = 2 * H
    wi, bi_ = w_in[0], b_in[0]
    gate_pad = GATE_ROWS_PAD - n_gate
    w_all, wkt = _proj_weights(wi.T)
    b_all = bi_[None, :]
    bkt = jnp.pad(jnp.concatenate([bi_[OFF_KB:OFF_VB], bi_[OFF_GATES:]]), (0, gate_pad))[:, None]
    wo = w_out[0].astype(BF16)
    w1 = w_ff1[0].astype(BF16)
    w2 = w_ff2[0].astype(BF16)
    row = lambda p: p[0][None, :]
    ffn_params = (wo, row(ln1_g), row(ln1_b), w1, row(b_ff1), w2, row(b_ff2), row(ln2_g), row(ln2_b))
    norm_g = row(mlstm_norm_g)

    def layer(x, k_prev, v_prev, prev_map, c0, n0, m0, *, seqs, cps, mask_first):
        Bx, Tx, _ = x.shape
        n_tok = Bx * Tx
        t_seq = n_tok // seqs
        tm = _token_tile(Tx)
        keep = min(ATT_REACH, Tx)
        lc = min(MLSTM_CHUNK, t_seq)
        qkva, qvm, kt, og, gates, k_last, v_last = _in_proj(x, w_all, b_all, wkt, bkt, tm=tm, keep=keep, lc=lc)
        seq = lambda a: a.reshape(seqs, t_seq, a.shape[-1])
        tq = ATTN_TILE if t_seq % ATTN_TILE == 0 else _token_tile(t_seq)
        keys = -(-(ATT_REACH + cps * CHUNK) // LANES) * LANES
        bias_tab = _bias_table(rel_bias[0], cps=cps, keys=keys)
        past_feature_major = k_prev is not None
        if k_prev is None:
            k_prev = v_prev = seq(qkva)
            step = tq // ATT_REACH
            prev_map = (lambda b, i: (b, jnp.maximum(step * i - 1, 0), 1),
                        lambda b, i: (b, jnp.maximum(step * i - 1, 0), 2))
        att = _band_attn(seq(qkva), k_prev, v_prev, prev_map, bias_tab,
                         tq=tq, cps=cps, keys=keys, mask_first=mask_first,
                         past_feature_major=past_feature_major)
        gates = gates.reshape(t_seq // lc, seqs, GATE_ROWS, lc)
        kt = kt.reshape(Bx, W, seqs // Bx, t_seq).transpose(0, 2, 1, 3).reshape(seqs, W, t_seq)
        mb, c_new, n_new, m_new = _mlstm(seq(qvm), kt, gates, seq(og), norm_g, c0, n0, m0,
                                         lc=lc, bb=min(seqs, MLSTM_STREAMS),
                                         nc=min(MLSTM_CHUNKS_PER_STEP, t_seq // lc))
        tok = lambda a: a.reshape(Bx, Tx, a.shape[-1])
        tf = FFN_TILE if Tx % FFN_TILE == 0 else tm
        y = _merge_ffn(x, tok(att), tok(mb), *ffn_params, tm=tf, fc=2048, parts=max(tf // FFN_GROUP, 1))
        return y, k_last, v_last, c_new, n_new, m_new

    zeros_c = jnp.zeros((B, H, d, d), F32)
    zeros_n = jnp.zeros((B, H, d), F32)
    zeros_m = jnp.zeros((B, H, 1), F32)
    yp, kp, vp, cp, np_, mp = layer(x_prompt, None, None, None, zeros_c, zeros_n, zeros_m,
                                    seqs=B, cps=1, mask_first=True)

    ck = cache_k[0].transpose(0, 2, 3, 1).reshape(DB, ATT_WIDTH, ATT_REACH)
    cv = cache_v[0].transpose(0, 2, 3, 1).reshape(DB, ATT_WIDTH, ATT_REACH)
    prev_sample = (lambda b, i: (b, 0, 0), lambda b, i: (b, 0, 0))
    ys, ks, vs, cs, ns, ms = layer(
        x_sample.reshape(1, DB * T, D), ck, cv, prev_sample,
        state_C[0].astype(F32), state_n[0].astype(F32), state_m[0].astype(F32).reshape(DB, H, 1),
        seqs=DB, cps=1, mask_first=False)

    sd = state_C.dtype
    def heads(a, nb, t):
        bx = a.shape[0]
        a = a.reshape(bx, ATT_HEADS, ATT_HEAD_DIM, nb // bx, t)
        return a.transpose(0, 3, 4, 1, 2).reshape(nb, t, ATT_HEADS, ATT_HEAD_DIM)[None]
    keep_p = min(ATT_REACH, S)
    return (yp, ys.reshape(DB, T, D),
            heads(kp, B, keep_p).astype(cache_k.dtype), heads(vp, B, keep_p).astype(cache_v.dtype),
            cp[None].astype(sd), np_[None].astype(sd), mp.reshape(1, B, H).astype(sd),
            heads(ks, DB, T).astype(cache_k.dtype), heads(vs, DB, T).astype(cache_v.dtype),
            cs[None].astype(sd), ns[None].astype(sd), ms.reshape(1, DB, H).astype(sd))
```

```python
import functools

import jax
import jax.numpy as jnp
from jax import lax
from jax.experimental import pallas as pl
from jax.experimental.pallas import tpu as pltpu

F32 = jnp.float32
BF16 = jnp.bfloat16

D_MODEL = 1024
CHUNK = 64
LEFT_CHUNKS = 8
ATT_REACH = LEFT_CHUNKS * CHUNK
ATT_WIDTH = 512
MLSTM_WIDTH = 512
ATT_HEADS = 8
ATT_HEAD_DIM = 64
MLSTM_HEADS = 4
MLSTM_HEAD_DIM = 128
MAX_REL = 128
D_FF = 4 * D_MODEL
DEPTH = 1
DEEPNORM_ALPHA = (2 * DEPTH) ** 0.25
LN_EPS = 1e-5
NEG_INF = -1e30
ATT_SCALE = ATT_HEAD_DIM ** -0.5
KB_SCALE = MLSTM_HEAD_DIM ** -0.5

OFF_QB = 3 * ATT_WIDTH
OFF_KB = OFF_QB + MLSTM_WIDTH
OFF_VB = OFF_KB + MLSTM_WIDTH
OFF_OB = OFF_VB + MLSTM_WIDTH
OFF_GATES = OFF_OB + MLSTM_WIDTH

LANES = 128
GATE_ROWS_PAD = 16
MLSTM_CHUNK = 256
MLSTM_CHUNKS_PER_STEP = 4
MLSTM_STREAMS = 2
GATE_ROWS = 24
IN_PROJ_TILE = 1024
ATTN_TILE = 1024
FFN_TILE = 1024
FFN_GROUP = 256
HEAD_PAIRS = ATT_HEADS // 2
VMEM_LIMIT = 56 * 1024 * 1024
NT_DIMS = (((1,), (1,)), ((), ()))


def _cparams(sem):
    return pltpu.CompilerParams(dimension_semantics=sem, vmem_limit_bytes=VMEM_LIMIT)


def _in_proj_kernel(x_ref, w_ref, b_ref, wkt_ref, bkt_ref,
                    qkva_ref, qvm_ref, kt_ref, og_ref, gates_ref, klast_ref, vlast_ref, *, tm, keep, lc):
    i = pl.program_id(1)
    last = pl.num_programs(1) - 1
    xb = x_ref[0].astype(BF16)

    def proj(c0, width):
        return (jnp.dot(xb, w_ref[:, c0:c0 + width], preferred_element_type=F32)
                + b_ref[:, c0:c0 + width])

    W = ATT_WIDTH
    H = MLSTM_HEADS
    ktg = lax.dot_general(wkt_ref[...], xb, NT_DIMS, preferred_element_type=F32) + bkt_ref[...]
    kt_ref[0] = (ktg[0:W] * KB_SCALE).astype(BF16)
    g = ktg[W:W + 2 * H]
    log_sig = jnp.minimum(g, 0.0) - jnp.log1p(jnp.exp(-jnp.abs(g)))
    head_row = lax.broadcasted_iota(jnp.int32, g.shape, 0) < H
    g8 = jnp.where(head_row, g, log_sig)
    pos = lax.broadcasted_iota(jnp.int32, g8.shape, 1) % lc
    csum = g8
    sh = 1
    while sh < lc:
        csum = csum + jnp.where(pos >= sh, pltpu.roll(csum, sh, 1), 0.0)
        sh *= 2
    b = pltpu.roll(csum, H, 0)
    a = g8 - b
    cm = a
    sh = 1
    while sh < lc:
        cm = jnp.maximum(cm, jnp.where(pos >= sh, pltpu.roll(cm, sh, 1), -jnp.inf))
        sh *= 2
    rows = [jnp.where(head_row, t, 0.0) for t in (b, a, cm)]
    for c in range(tm // lc):
        for j, t in enumerate(rows):
            gates_ref[c, 0, 8 * j:8 * j + 8, :] = t[:, c * lc:(c + 1) * lc]

    qvm_ref[0, :, 0:W] = proj(OFF_QB, W).astype(BF16)
    qvm_ref[0, :, W:2 * W] = proj(OFF_VB, W).astype(BF16)
    ob = proj(OFF_OB, W)
    og_ref[0] = 1.0 / (1.0 + jnp.exp(-ob))
    qa = proj(0, W)
    qkva_ref[0, :, 0:W] = (qa * ATT_SCALE).astype(BF16)
    ka = proj(W, W)
    qkva_ref[0, :, W:2 * W] = ka.astype(BF16)
    va = proj(2 * W, W)
    qkva_ref[0, :, 2 * W:3 * W] = va.astype(BF16)

    @pl.when(i == last)
    def _():
        klast_ref[0] = ka[tm - keep:, :].T
        vlast_ref[0] = va[tm - keep:, :].T


def _in_proj(x, w, b, wkt, bkt, *, tm, keep, lc):
    B, T, D = x.shape
    nt = T // tm
    W = ATT_WIDTH
    kern = functools.partial(_in_proj_kernel, tm=tm, keep=keep, lc=lc)
    return pl.pallas_call(
        kern,
        grid=(B, nt),
        in_specs=[
            pl.BlockSpec((1, tm, D), lambda b, i: (b, i, 0)),
            pl.BlockSpec(w.shape, lambda b, i: (0, 0), pipeline_mode=pl.Buffered(1)),
            pl.BlockSpec(b.shape, lambda b, i: (0, 0), pipeline_mode=pl.Buffered(1)),
            pl.BlockSpec(wkt.shape, lambda b, i: (0, 0), pipeline_mode=pl.Buffered(1)),
            pl.BlockSpec(bkt.shape, lambda b, i: (0, 0), pipeline_mode=pl.Buffered(1)),
        ],
        out_specs=[
            pl.BlockSpec((1, tm, 3 * W), lambda b, i: (b, i, 0)),
            pl.BlockSpec((1, tm, 2 * W), lambda b, i: (b, i, 0)),
            pl.BlockSpec((1, W, tm), lambda b, i: (b, 0, i)),
            pl.BlockSpec((1, tm, W), lambda b, i: (b, i, 0)),
            pl.BlockSpec((tm // lc, 1, GATE_ROWS, lc), lambda b, i: (i, b, 0, 0)),
            pl.BlockSpec((1, W, keep), lambda b, i: (b, 0, 0)),
            pl.BlockSpec((1, W, keep), lambda b, i: (b, 0, 0)),
        ],
        out_shape=[
            jax.ShapeDtypeStruct((B, T, 3 * W), BF16),
            jax.ShapeDtypeStruct((B, T, 2 * W), BF16),
            jax.ShapeDtypeStruct((B, W, T), BF16),
            jax.ShapeDtypeStruct((B, T, W), F32),
            jax.ShapeDtypeStruct((T // lc, B, GATE_ROWS, lc), F32),
            jax.ShapeDtypeStruct((B, W, keep), F32),
            jax.ShapeDtypeStruct((B, W, keep), F32),
        ],
        compiler_params=_cparams(("parallel", "arbitrary")),
        name="in_proj",
    )(x, w, b, wkt, bkt)


def _proj_weights_kernel(wt_ref, wg_ref, w_ref, wkt_ref):
    j = pl.program_id(0)
    W = wt_ref.shape[0]
    w_ref[...] = wt_ref[...].T.astype(BF16)

    @pl.when(j == OFF_KB // W)
    def _():
        wkt_ref[0:W, :] = wt_ref[...].astype(BF16)

    @pl.when(j == 0)
    def _():
        pad = jnp.zeros((GATE_ROWS_PAD - wg_ref.shape[0], wg_ref.shape[1]), F32)
        wkt_ref[W:W + GATE_ROWS_PAD, :] = jnp.concatenate([wg_ref[...], pad], axis=0).astype(BF16)


def _proj_weights(wt):
    n_in, D = wt.shape
    W = MLSTM_WIDTH
    n_gate = 2 * MLSTM_HEADS
    return pl.pallas_call(
        _proj_weights_kernel,
        grid=(OFF_GATES // W,),
        in_specs=[pl.BlockSpec((W, D), lambda j: (j, 0)),
                  pl.BlockSpec((n_gate, D), lambda j: (OFF_GATES // n_gate, 0))],
        out_specs=[pl.BlockSpec((D, W), lambda j: (0, j)),
                   pl.BlockSpec((W + GATE_ROWS_PAD, D), lambda j: (0, 0))],
        out_shape=[jax.ShapeDtypeStruct((D, OFF_GATES), BF16),
                   jax.ShapeDtypeStruct((W + GATE_ROWS_PAD, D), BF16)],
        compiler_params=_cparams(("arbitrary",)),
        name="proj_weights",
    )(wt, wt)


def _bias_table_kernel(g_ref, o_ref, *, rows, keys):
    L = g_ref.shape[-1]
    r = lax.broadcasted_iota(jnp.int32, (rows, keys), 0)
    c = lax.broadcasted_iota(jnp.int32, (rows, keys), 1)
    band_start = (r // CHUNK) * CHUNK
    in_band = (c >= band_start) & (c < band_start + ATT_REACH + CHUNK)
    for h in range(ATT_HEADS):
        g = jnp.broadcast_to(g_ref[h], (rows, L))
        t = pltpu.roll(g, L - rows, 1, stride=1, stride_axis=0)
        tab = jnp.where(in_band, t[:, :keys], NEG_INF)
        o_ref[h // 2, (h % 2) * rows:(h % 2 + 1) * rows, :] = tab


def _bias_table(rel_bias, *, cps, keys):
    rows = cps * CHUNK
    L = ((rows + keys + LANES - 1) // LANES) * LANES
    n_const = rows + ATT_REACH - MAX_REL
    const = jnp.broadcast_to(rel_bias[:, 2 * MAX_REL:], (ATT_HEADS, n_const))
    ramp = rel_bias[:, ::-1]
    tail = jnp.broadcast_to(rel_bias[:, :1], (ATT_HEADS, LANES))
    g = jnp.concatenate([const, ramp, tail], axis=1)[:, :L].reshape(ATT_HEADS, 1, L)
    kern = functools.partial(_bias_table_kernel, rows=rows, keys=keys)
    return pl.pallas_call(
        kern,
        out_shape=jax.ShapeDtypeStruct((HEAD_PAIRS, 2 * rows, keys), F32),
        name="bias_table",
    )(g)


def _attn_kernel(q_ref, kc_ref, vc_ref, kp_ref, vp_ref, bias_ref, o_ref, kk_ref, vv_ref,
                 *, tq, cps, keys, mask_first, past_feature_major):
    i = pl.program_id(1)
    rows = cps * CHUNK
    if past_feature_major:
        kk_ref[0:ATT_REACH, :] = kp_ref[0].T.astype(BF16)
        vv_ref[0:ATT_REACH, :] = vp_ref[0].T.astype(BF16)
    else:
        kk_ref[0:ATT_REACH, :] = kp_ref[0].astype(BF16)
        vv_ref[0:ATT_REACH, :] = vp_ref[0].astype(BF16)
    kk_ref[ATT_REACH:ATT_REACH + tq, :] = kc_ref[0]
    vv_ref[ATT_REACH:ATT_REACH + tq, :] = vc_ref[0]
    total = kk_ref.shape[0]
    if total > ATT_REACH + tq:
        kk_ref[ATT_REACH + tq:, :] = jnp.zeros((total - ATT_REACH - tq, ATT_WIDTH), BF16)
        vv_ref[ATT_REACH + tq:, :] = jnp.zeros((total - ATT_REACH - tq, ATT_WIDTH), BF16)

    lane = lax.broadcasted_iota(jnp.int32, (rows, LANES), 1)
    first_head = lane < ATT_HEAD_DIM

    def tile(no_past):
        for sub in range(tq // rows):
            off = sub * rows
            first_valid = max(ATT_REACH - off, 0) if no_past else 0
            k0 = first_valid // LANES * LANES
            ones_cols = jnp.ones((keys - k0, LANES), BF16)
            for pair in range(HEAD_PAIRS):
                ls = slice(pair * LANES, (pair + 1) * LANES)
                q2 = q_ref[0, off:off + rows, ls]
                zero = jnp.zeros_like(q2)
                qs = jnp.concatenate([jnp.where(first_head, q2, zero), jnp.where(first_head, zero, q2)], axis=0)
                s = lax.dot_general(qs, kk_ref[off + k0:off + keys, ls], NT_DIMS,
                                    preferred_element_type=F32)
                s = s + bias_ref[pair, :, k0:keys]
                if first_valid > k0:
                    col = lax.broadcasted_iota(jnp.int32, s.shape, 1)
                    s = jnp.where(col >= first_valid - k0, s, NEG_INF)
                m = jnp.max(s, axis=-1, keepdims=True)
                e = jnp.exp(s - m).astype(BF16)
                v_ext = jnp.concatenate([vv_ref[off + k0:off + keys, ls], ones_cols], axis=1)
                o2 = jnp.dot(e, v_ext, preferred_element_type=F32)
                o2 = o2[:, 0:LANES] * (1.0 / o2[:, LANES:2 * LANES])
                o = jnp.where(first_head, o2[:rows], o2[rows:])
                o_ref[0, off:off + rows, ls] = o.astype(BF16)

    if mask_first:
        pl.when(i == 0)(lambda: tile(True))
        pl.when(i > 0)(lambda: tile(False))
    else:
        tile(False)


def _band_attn(qkva, k_prev, v_prev, prev_map, bias_tab, *, tq, cps, keys, mask_first, past_feature_major):
    B, T, _ = qkva.shape
    nt = T // tq
    rows_total = max(ATT_REACH + tq, (tq // (cps * CHUNK) - 1) * cps * CHUNK + keys)
    kern = functools.partial(_attn_kernel, tq=tq, cps=cps, keys=keys, mask_first=mask_first,
                             past_feature_major=past_feature_major)
    W = ATT_WIDTH
    prev_block = (1, ATT_REACH, W)
    return pl.pallas_call(
        kern,
        grid=(B, nt),
        in_specs=[
            pl.BlockSpec((1, tq, W), lambda b, i: (b, i, 0)),
            pl.BlockSpec((1, tq, W), lambda b, i: (b, i, 1)),
            pl.BlockSpec((1, tq, W), lambda b, i: (b, i, 2)),
            pl.BlockSpec(prev_block, prev_map[0]),
            pl.BlockSpec(prev_block, prev_map[1]),
            pl.BlockSpec(bias_tab.shape, lambda b, i: (0, 0, 0)),
        ],
        out_specs=pl.BlockSpec((1, tq, W), lambda b, i: (b, i, 0)),
        out_shape=jax.ShapeDtypeStruct((B, T, W), BF16),
        scratch_shapes=[pltpu.VMEM((rows_total, W), BF16), pltpu.VMEM((rows_total, W), BF16)],
        compiler_params=_cparams(("parallel", "arbitrary")),
        name="band_attn",
    )(qkva, qkva, qkva, k_prev, v_prev, bias_tab)


def _lane_mean(x, mean_w):
    hi = x.astype(BF16)
    lo = (x - hi.astype(F32)).astype(BF16)
    return jnp.dot(jnp.concatenate([hi, lo], axis=1), mean_w, preferred_element_type=F32)


def _mlstm_kernel(q_ref, kt_ref, v_ref, gates_ref, og_ref, ng_ref, c0_ref, n0_ref, m0_ref,
                  mb_ref, c_out_ref, n_out_ref, m_out_ref, cn_s, m_s, *, lc, bb, nc):
    i = pl.program_id(1)
    last = pl.num_programs(1) - 1
    H, d = MLSTM_HEADS, MLSTM_HEAD_DIM

    @pl.when(i == 0)
    def _():
        m_s[...] = jnp.zeros(m_s.shape, F32)
        for bi in range(bb):
            for h in range(H):
                cn_s[bi * H + h, :, 0:d] = c0_ref[bi, h].T
                cn_s[bi * H + h, :, d:2 * d] = jnp.broadcast_to(n0_ref[bi, h:h + 1, :], (d, d)).T
            m_s[bi, 0:H, :] = jnp.broadcast_to(m0_ref[bi], (H, LANES))

    ti = lax.broadcasted_iota(jnp.int32, (lc, lc), 0)
    si = lax.broadcasted_iota(jnp.int32, (lc, lc), 1)
    causal = ti >= si
    ones_cols = jnp.ones((lc, d), BF16)
    mean_w = jnp.full((2 * d, d), 1.0 / d, BF16)

    a_rows, decays, col_sets = {}, {}, {}
    m_cur = [m_s[bi, :, 0:1] for bi in range(bb)]
    for c in range(nc):
        for bi in range(bb):
            b = gates_ref[c, bi, 0:8, :]
            a = gates_ref[c, bi, 8:16, :]
            cm = gates_ref[c, bi, 16:24, :]
            m_prev = m_cur[bi]
            inter = b + m_prev
            m_t = jnp.maximum(inter, b + cm)
            b_last = b[:, lc - 1:lc]
            m_new = jnp.maximum(b_last + m_prev, b_last + cm[:, lc - 1:lc])
            w_end = jnp.exp(b_last + a - m_new)
            stacked = jnp.concatenate(
                [b - m_t, jnp.exp(inter - m_t), jnp.exp(-m_t), w_end, jnp.zeros((LANES - 32, lc), F32)], axis=0)
            col_sets[c, bi] = stacked.T
            a_rows[c, bi] = a
            decays[c, bi] = jnp.exp(b_last + m_prev - m_new)
            m_cur[bi] = m_new
    for bi in range(bb):
        m_s[bi] = jnp.broadcast_to(m_cur[bi], (8, LANES))

    heads = [(bi, h) for bi in range(bb) for h in range(H)]
    inst = [(c, bi, h) for c in range(nc) for bi, h in heads]
    hsl = lambda h: slice(h * d, (h + 1) * d)
    rsl = lambda c: slice(c * lc, (c + 1) * lc)
    col = lambda c, bi, h, j: col_sets[c, bi][:, 8 * j + h:8 * j + h + 1]
    qs = {(c, bi, h): q_ref[bi, rsl(c), hsl(h)] for c, bi, h in inst}
    kts = {(c, bi, h): kt_ref[bi, hsl(h), rsl(c)] for c, bi, h in inst}
    vs = {(c, bi, h): v_ref[bi, rsl(c), hsl(h)] for c, bi, h in inst}

    qk = {k_: jnp.dot(qs[k_], kts[k_], preferred_element_type=F32) for k_ in inst}
    cn_cur = {(bi, h): cn_s[bi * H + h] for bi, h in heads}
    qcn = {}
    for c in range(nc):
        for bi, h in heads:
            qcn[c, bi, h] = jnp.dot(qs[c, bi, h], cn_cur[bi, h].astype(BF16), preferred_element_type=F32)
        for bi, h in heads:
            w_end_col = jnp.broadcast_to(col(c, bi, h, 3), (lc, d))
            vw = jnp.concatenate(
                [(vs[c, bi, h].astype(F32) * w_end_col).astype(BF16), w_end_col.astype(BF16)], axis=1)
            cn_cur[bi, h] = (decays[c, bi][h:h + 1, :] * cn_cur[bi, h]
                             + jnp.dot(kts[c, bi, h], vw, preferred_element_type=F32))
    for bi, h in heads:
        cn_s[bi * H + h] = cn_cur[bi, h]
    w = {(c, bi, h): jnp.exp(jnp.where(causal, col(c, bi, h, 0) + a_rows[c, bi][h:h + 1, :], -jnp.inf))
         * qk[c, bi, h] for c, bi, h in inst}
    wv = {k_: jnp.dot(w[k_].astype(BF16), jnp.concatenate([vs[k_], ones_cols], axis=1),
                      preferred_element_type=F32) for k_ in inst}
    hh = {}
    for c, bi, h in inst:
        tot = col(c, bi, h, 1) * qcn[c, bi, h] + wv[c, bi, h]
        den = tot[:, d:2 * d]
        hh[c, bi, h] = tot[:, 0:d] * (1.0 / jnp.maximum(jnp.abs(den), col(c, bi, h, 2)))
    xc = {k_: hh[k_] - _lane_mean(hh[k_], mean_w) for k_ in inst}
    var = {k_: _lane_mean(xc[k_] * xc[k_], mean_w) for k_ in inst}
    for c, bi, h in inst:
        hn = xc[c, bi, h] * lax.rsqrt(var[c, bi, h] + LN_EPS) * ng_ref[:, hsl(h)]
        mb_ref[bi, rsl(c), hsl(h)] = (og_ref[bi, rsl(c), hsl(h)] * hn).astype(BF16)

    @pl.when(i == last)
    def _():
        for bi in range(bb):
            for h in range(H):
                cn = cn_s[bi * H + h]
                c_out_ref[bi, h] = cn[:, 0:d].T
                n_out_ref[bi, h:h + 1, :] = cn[:, d:2 * d].T[0:1, :]
            m_out_ref[bi] = m_s[bi, 0:H, 0:1]


def _mlstm(qvm, kt, gates, og, norm_g, c0, n0, m0, *, lc, bb, nc):
    B, T, _ = qvm.shape
    ts = nc * lc
    nt = T // ts
    W = MLSTM_WIDTH
    H, d = MLSTM_HEADS, MLSTM_HEAD_DIM
    kern = functools.partial(_mlstm_kernel, lc=lc, bb=bb, nc=nc)
    return pl.pallas_call(
        kern,
        grid=(B // bb, nt),
        in_specs=[
            pl.BlockSpec((bb, ts, W), lambda b, i: (b, i, 0)),
            pl.BlockSpec((bb, W, ts), lambda b, i: (b, 0, i)),
            pl.BlockSpec((bb, ts, W), lambda b, i: (b, i, 1)),
            pl.BlockSpec((nc, bb, GATE_ROWS, lc), lambda b, i: (i, b, 0, 0)),
            pl.BlockSpec((bb, ts, W), lambda b, i: (b, i, 0)),
            pl.BlockSpec((1, W), lambda b, i: (0, 0)),
            pl.BlockSpec((bb, H, d, d), lambda b, i: (b, 0, 0, 0)),
            pl.BlockSpec((bb, H, d), lambda b, i: (b, 0, 0)),
            pl.BlockSpec((bb, H, 1), lambda b, i: (b, 0, 0)),
        ],
        out_specs=[
            pl.BlockSpec((bb, ts, W), lambda b, i: (b, i, 0)),
            pl.BlockSpec((bb, H, d, d), lambda b, i: (b, 0, 0, 0)),
            pl.BlockSpec((bb, H, d), lambda b, i: (b, 0, 0)),
            pl.BlockSpec((bb, H, 1), lambda b, i: (b, 0, 0)),
        ],
        out_shape=[
            jax.ShapeDtypeStruct((B, T, W), BF16),
            jax.ShapeDtypeStruct((B, H, d, d), F32),
            jax.ShapeDtypeStruct((B, H, d), F32),
            jax.ShapeDtypeStruct((B, H, 1), F32),
        ],
        scratch_shapes=[
            pltpu.VMEM((bb * H, d, 2 * d), F32),
            pltpu.VMEM((bb, 8, LANES), F32),
        ],
        compiler_params=_cparams(("parallel", "arbitrary")),
        name="mlstm",
    )(qvm, kt, qvm, gates, og, norm_g, c0, n0, m0)


def _layer_norm(z, g, b):
    mu = jnp.mean(z, axis=-1, keepdims=True)
    zc = z - mu
    var = jnp.mean(zc * zc, axis=-1, keepdims=True)
    return zc * lax.rsqrt(var + LN_EPS) * g + b


def _merge_ffn_kernel(x_ref, att_ref, mb_ref, wo_ref, g1_ref, b1_ref, w1_ref, bf1_ref, w2_ref, bf2_ref,
                      g2_ref, b2_ref, y_ref, *, fc, parts):
    W = ATT_WIDTH
    tm = x_ref.shape[1]
    rp = tm // parts
    groups = [slice(p * rp, (p + 1) * rp) for p in range(parts)]
    mixes = [jnp.dot(att_ref[0, rs, :], wo_ref[0:W, :], preferred_element_type=F32)
             + jnp.dot(mb_ref[0, rs, :], wo_ref[W:2 * W, :], preferred_element_type=F32) for rs in groups]
    for rs, mix in zip(groups, mixes):
        h = _layer_norm(DEEPNORM_ALPHA * x_ref[0, rs, :] + mix, g1_ref[...], b1_ref[...])
        hb = h.astype(BF16)
        f = jnp.zeros(h.shape, F32)
        for j in range(D_FF // fc):
            cs = slice(j * fc, (j + 1) * fc)
            a = jnp.dot(hb, w1_ref[:, cs], preferred_element_type=F32) + bf1_ref[:, cs]
            a = jnp.square(jnp.maximum(a, 0.0)).astype(BF16)
            f = f + jnp.dot(a, w2_ref[cs, :], preferred_element_type=F32)
        f = f + bf2_ref[...]
        y_ref[0, rs, :] = _layer_norm(DEEPNORM_ALPHA * h + f, g2_ref[...], b2_ref[...])


def _merge_ffn(x, att, mb, wo, g1, b1, w1, bf1, w2, bf2, g2, b2, *, tm, fc, parts):
    B, T, D = x.shape
    nt = T // tm
    W = ATT_WIDTH
    const = lambda shape: pl.BlockSpec(shape, lambda b, i: (0, 0), pipeline_mode=pl.Buffered(1))
    kern = functools.partial(_merge_ffn_kernel, fc=fc, parts=parts)
    return pl.pallas_call(
        kern,
        grid=(B, nt),
        in_specs=[
            pl.BlockSpec((1, tm, D), lambda b, i: (b, i, 0)),
            pl.BlockSpec((1, tm, W), lambda b, i: (b, i, 0)),
            pl.BlockSpec((1, tm, W), lambda b, i: (b, i, 0)),
            const((D, D)), const((1, D)), const((1, D)),
            const((D, D_FF)), const((1, D_FF)), const((D_FF, D)), const((1, D)),
            const((1, D)), const((1, D)),
        ],
        out_specs=pl.BlockSpec((1, tm, D), lambda b, i: (b, i, 0)),
        out_shape=jax.ShapeDtypeStruct((B, T, D), F32),
        compiler_params=_cparams(("parallel", "parallel")),
        name="merge_ffn",
    )(x, att, mb, wo, g1, b1, w1, bf1, w2, bf2, g2, b2)


def _token_tile(t):
    return 512 if t % 512 == 0 else t


def kernel(x_prompt, x_sample, cache_k, cache_v, state_C, state_n, state_m, w_in, b_in, rel_bias,
           mlstm_norm_g, w_out, ln1_g, ln1_b, w_ff1, b_ff1, w_ff2, b_ff2, ln2_g, ln2_b):
    assert w_in.shape[0] == DEPTH == 1
    B, S, D = x_prompt.shape
    DB, T, _ = x_sample.shape
    H, d = MLSTM_HEADS, MLSTM_HEAD_DIM
    W = MLSTM_WIDTH
    assert S % ATT_REACH == 0 and T == CHUNK and cache_k.shape[2] == ATT_REACH

    n_gate = 2 * H
    wi, bi_ = w_in[0], b_in[0]
    gate_pad = GATE_ROWS_PAD - n_gate
    w_all, wkt = _proj_weights(wi.T)
    b_all = bi_[None, :]
    bkt = jnp.pad(jnp.concatenate([bi_[OFF_KB:OFF_VB], bi_[OFF_GATES:]]), (0, gate_pad))[:, None]
    wo = w_out[0].astype(BF16)
    w1 = w_ff1[0].astype(BF16)
    w2 = w_ff2[0].astype(BF16)
    row = lambda p: p[0][None, :]
    ffn_params = (wo, row(ln1_g), row(ln1_b), w1, row(b_ff1), w2, row(b_ff2), row(ln2_g), row(ln2_b))
    norm_g = row(mlstm_norm_g)

    def layer(x, k_prev, v_prev, prev_map, c0, n0, m0, *, seqs, cps, mask_first):
        Bx, Tx, _ = x.shape
        n_tok = Bx * Tx
        t_seq = n_tok // seqs
        tm = _token_tile(Tx)
        tp = IN_PROJ_TILE if Tx % IN_PROJ_TILE == 0 else tm
        keep = min(ATT_REACH, Tx)
        lc = min(MLSTM_CHUNK, t_seq)
        qkva, qvm, kt, og, gates, k_last, v_last = _in_proj(x, w_all, b_all, wkt, bkt, tm=tp, keep=keep, lc=lc)
        seq = lambda a: a.reshape(seqs, t_seq, a.shape[-1])
        tq = ATTN_TILE if t_seq % ATTN_TILE == 0 else _token_tile(t_seq)
        keys = -(-(ATT_REACH + cps * CHUNK) // LANES) * LANES
        bias_tab = _bias_table(rel_bias[0], cps=cps, keys=keys)
        past_feature_major = k_prev is not None
        if k_prev is None:
            k_prev = v_prev = seq(qkva)
            step = tq // ATT_REACH
            prev_map = (lambda b, i: (b, jnp.maximum(step * i - 1, 0), 1),
                        lambda b, i: (b, jnp.maximum(step * i - 1, 0), 2))
        att = _band_attn(seq(qkva), k_prev, v_prev, prev_map, bias_tab,
                         tq=tq, cps=cps, keys=keys, mask_first=mask_first,
                         past_feature_major=past_feature_major)
        gates = gates.reshape(t_seq // lc, seqs, GATE_ROWS, lc)
        kt = kt.reshape(Bx, W, seqs // Bx, t_seq).transpose(0, 2, 1, 3).reshape(seqs, W, t_seq)
        mb, c_new, n_new, m_new = _mlstm(seq(qvm), kt, gates, seq(og), norm_g, c0, n0, m0,
                                         lc=lc, bb=min(seqs, MLSTM_STREAMS),
                                         nc=min(MLSTM_CHUNKS_PER_STEP, t_seq // lc))
        tok = lambda a: a.reshape(Bx, Tx, a.shape[-1])
        tf = FFN_TILE if Tx % FFN_TILE == 0 else tm
        y = _merge_ffn(x, tok(att), tok(mb), *ffn_params, tm=tf, fc=2048, parts=max(tf // FFN_GROUP, 1))
        return y, k_last, v_last, c_new, n_new, m_new

    zeros_c = jnp.zeros((B, H, d, d), F32)
    zeros_n = jnp.zeros((B, H, d), F32)
    zeros_m = jnp.zeros((B, H, 1), F32)
    yp, kp, vp, cp, np_, mp = layer(x_prompt, None, None, None, zeros_c, zeros_n, zeros_m,
                                    seqs=B, cps=1, mask_first=True)

    ck = cache_k[0].transpose(0, 2, 3, 1).reshape(DB, ATT_WIDTH, ATT_REACH)
    cv = cache_v[0].transpose(0, 2, 3, 1).reshape(DB, ATT_WIDTH, ATT_REACH)
    prev_sample = (lambda b, i: (b, 0, 0), lambda b, i: (b, 0, 0))
    ys, ks, vs, cs, ns, ms = layer(
        x_sample.reshape(1, DB * T, D), ck, cv, prev_sample,
        state_C[0].astype(F32), state_n[0].astype(F32), state_m[0].astype(F32).reshape(DB, H, 1),
        seqs=DB, cps=1, mask_first=False)

    sd = state_C.dtype
    def heads(a, nb, t):
        bx = a.shape[0]
        a = a.reshape(bx, ATT_HEADS, ATT_HEAD_DIM, nb // bx, t)
        return a.transpose(0, 3, 4, 1, 2).reshape(nb, t, ATT_HEADS, ATT_HEAD_DIM)[None]
    keep_p = min(ATT_REACH, S)
    return (yp, ys.reshape(DB, T, D),
            heads(kp, B, keep_p).astype(cache_k.dtype), heads(vp, B, keep_p).astype(cache_v.dtype),
            cp[None].astype(sd), np_[None].astype(sd), mp.reshape(1, B, H).astype(sd),
            heads(ks, DB, T).astype(cache_k.dtype), heads(vs, DB, T).astype(cache_v.dtype),
            cs[None].astype(sd), ns[None].astype(sd), ms.reshape(1, DB, H).astype(sd))
```

```python
import functools

import jax
import jax.numpy as jnp
from jax import lax
from jax.experimental import pallas as pl
from jax.experimental.pallas import tpu as pltpu

F32 = jnp.float32
BF16 = jnp.bfloat16

D_MODEL = 1024
CHUNK = 64
LEFT_CHUNKS = 8
ATT_REACH = LEFT_CHUNKS * CHUNK
ATT_WIDTH = 512
MLSTM_WIDTH = 512
ATT_HEADS = 8
ATT_HEAD_DIM = 64
MLSTM_HEADS = 4
MLSTM_HEAD_DIM = 128
MAX_REL = 128
D_FF = 4 * D_MODEL
DEPTH = 1
DEEPNORM_ALPHA = (2 * DEPTH) ** 0.25
LN_EPS = 1e-5
NEG_INF = -1e30
ATT_SCALE = ATT_HEAD_DIM ** -0.5
KB_SCALE = MLSTM_HEAD_DIM ** -0.5

OFF_QB = 3 * ATT_WIDTH
OFF_KB = OFF_QB + MLSTM_WIDTH
OFF_VB = OFF_KB + MLSTM_WIDTH
OFF_OB = OFF_VB + MLSTM_WIDTH
OFF_GATES = OFF_OB + MLSTM_WIDTH

LANES = 128
GATE_ROWS_PAD = 16
MLSTM_CHUNK = 256
MLSTM_CHUNKS_PER_STEP = 4
MLSTM_STREAMS = 2
GATE_ROWS = 24
IN_PROJ_TILE = 1024
ATTN_TILE = 2048
ATTN_CHUNKS_PER_WINDOW = 1
FFN_TILE = 1024
FFN_GROUP = 256
FFN_CHUNK = 2048
HEAD_PAIRS = ATT_HEADS // 2
VMEM_LIMIT = 56 * 1024 * 1024
NT_DIMS = (((1,), (1,)), ((), ()))


def _cparams(sem):
    return pltpu.CompilerParams(dimension_semantics=sem, vmem_limit_bytes=VMEM_LIMIT)


def _in_proj_kernel(x_ref, w_ref, b_ref, wkt_ref, bkt_ref,
                    qkva_ref, qvm_ref, kt_ref, og_ref, gates_ref, klast_ref, vlast_ref, *, tm, keep, lc):
    i = pl.program_id(1)
    last = pl.num_programs(1) - 1
    xb = x_ref[0].astype(BF16)

    def proj(c0, width):
        return (jnp.dot(xb, w_ref[:, c0:c0 + width], preferred_element_type=F32)
                + b_ref[:, c0:c0 + width])

    W = ATT_WIDTH
    H = MLSTM_HEADS
    ktg = lax.dot_general(wkt_ref[...], xb, NT_DIMS, preferred_element_type=F32) + bkt_ref[...]
    kt_ref[0] = (ktg[0:W] * KB_SCALE).astype(BF16)
    g = ktg[W:W + 2 * H]
    log_sig = jnp.minimum(g, 0.0) - jnp.log1p(jnp.exp(-jnp.abs(g)))
    head_row = lax.broadcasted_iota(jnp.int32, g.shape, 0) < H
    g8 = jnp.where(head_row, g, log_sig)
    pos = lax.broadcasted_iota(jnp.int32, g8.shape, 1) % lc
    csum = g8
    sh = 1
    while sh < lc:
        csum = csum + jnp.where(pos >= sh, pltpu.roll(csum, sh, 1), 0.0)
        sh *= 2
    b = pltpu.roll(csum, H, 0)
    a = g8 - b
    cm = a
    sh = 1
    while sh < lc:
        cm = jnp.maximum(cm, jnp.where(pos >= sh, pltpu.roll(cm, sh, 1), -jnp.inf))
        sh *= 2
    rows = [jnp.where(head_row, t, 0.0) for t in (b, a, cm)]
    for c in range(tm // lc):
        for j, t in enumerate(rows):
            gates_ref[c, 0, 8 * j:8 * j + 8, :] = t[:, c * lc:(c + 1) * lc]

    qvm_ref[0, :, 0:W] = proj(OFF_QB, W).astype(BF16)
    qvm_ref[0, :, W:2 * W] = proj(OFF_VB, W).astype(BF16)
    ob = proj(OFF_OB, W)
    og_ref[0] = 1.0 / (1.0 + jnp.exp(-ob))
    qa = proj(0, W)
    qkva_ref[0, :, 0:W] = (qa * ATT_SCALE).astype(BF16)
    ka = proj(W, W)
    qkva_ref[0, :, W:2 * W] = ka.astype(BF16)
    va = proj(2 * W, W)
    qkva_ref[0, :, 2 * W:3 * W] = va.astype(BF16)

    @pl.when(i == last)
    def _():
        klast_ref[0] = ka[tm - keep:, :].T
        vlast_ref[0] = va[tm - keep:, :].T


def _in_proj(x, w, b, wkt, bkt, *, tm, keep, lc):
    B, T, D = x.shape
    nt = T // tm
    W = ATT_WIDTH
    kern = functools.partial(_in_proj_kernel, tm=tm, keep=keep, lc=lc)
    return pl.pallas_call(
        kern,
        grid=(B, nt),
        in_specs=[
            pl.BlockSpec((1, tm, D), lambda b, i: (b, i, 0)),
            pl.BlockSpec(w.shape, lambda b, i: (0, 0), pipeline_mode=pl.Buffered(1)),
            pl.BlockSpec(b.shape, lambda b, i: (0, 0), pipeline_mode=pl.Buffered(1)),
            pl.BlockSpec(wkt.shape, lambda b, i: (0, 0), pipeline_mode=pl.Buffered(1)),
            pl.BlockSpec(bkt.shape, lambda b, i: (0, 0), pipeline_mode=pl.Buffered(1)),
        ],
        out_specs=[
            pl.BlockSpec((1, tm, 3 * W), lambda b, i: (b, i, 0)),
            pl.BlockSpec((1, tm, 2 * W), lambda b, i: (b, i, 0)),
            pl.BlockSpec((1, W, tm), lambda b, i: (b, 0, i)),
            pl.BlockSpec((1, tm, W), lambda b, i: (b, i, 0)),
            pl.BlockSpec((tm // lc, 1, GATE_ROWS, lc), lambda b, i: (i, b, 0, 0)),
            pl.BlockSpec((1, W, keep), lambda b, i: (b, 0, 0)),
            pl.BlockSpec((1, W, keep), lambda b, i: (b, 0, 0)),
        ],
        out_shape=[
            jax.ShapeDtypeStruct((B, T, 3 * W), BF16),
            jax.ShapeDtypeStruct((B, T, 2 * W), BF16),
            jax.ShapeDtypeStruct((B, W, T), BF16),
            jax.ShapeDtypeStruct((B, T, W), F32),
            jax.ShapeDtypeStruct((T // lc, B, GATE_ROWS, lc), F32),
            jax.ShapeDtypeStruct((B, W, keep), F32),
            jax.ShapeDtypeStruct((B, W, keep), F32),
        ],
        compiler_params=_cparams(("parallel", "arbitrary")),
        name="in_proj",
    )(x, w, b, wkt, bkt)


def _proj_weights_kernel(wt_ref, wg_ref, w_ref, wkt_ref):
    j = pl.program_id(0)
    W = wt_ref.shape[0]
    w_ref[...] = wt_ref[...].T.astype(BF16)

    @pl.when(j == OFF_KB // W)
    def _():
        wkt_ref[0:W, :] = wt_ref[...].astype(BF16)

    @pl.when(j == 0)
    def _():
        pad = jnp.zeros((GATE_ROWS_PAD - wg_ref.shape[0], wg_ref.shape[1]), F32)
        wkt_ref[W:W + GATE_ROWS_PAD, :] = jnp.concatenate([wg_ref[...], pad], axis=0).astype(BF16)


def _proj_weights(wt):
    n_in, D = wt.shape
    W = MLSTM_WIDTH
    n_gate = 2 * MLSTM_HEADS
    return pl.pallas_call(
        _proj_weights_kernel,
        grid=(OFF_GATES // W,),
        in_specs=[pl.BlockSpec((W, D), lambda j: (j, 0)),
                  pl.BlockSpec((n_gate, D), lambda j: (OFF_GATES // n_gate, 0))],
        out_specs=[pl.BlockSpec((D, W), lambda j: (0, j)),
                   pl.BlockSpec((W + GATE_ROWS_PAD, D), lambda j: (0, 0))],
        out_shape=[jax.ShapeDtypeStruct((D, OFF_GATES), BF16),
                   jax.ShapeDtypeStruct((W + GATE_ROWS_PAD, D), BF16)],
        compiler_params=_cparams(("arbitrary",)),
        name="proj_weights",
    )(wt, wt)


def _bias_table_kernel(g_ref, o_ref, *, rows, keys):
    L = g_ref.shape[-1]
    r = lax.broadcasted_iota(jnp.int32, (rows, keys), 0)
    c = lax.broadcasted_iota(jnp.int32, (rows, keys), 1)
    band_start = (r // CHUNK) * CHUNK
    in_band = (c >= band_start) & (c < band_start + ATT_REACH + CHUNK)
    for h in range(ATT_HEADS):
        g = jnp.broadcast_to(g_ref[h], (rows, L))
        t = pltpu.roll(g, L - rows, 1, stride=1, stride_axis=0)
        tab = jnp.where(in_band, t[:, :keys], NEG_INF)
        o_ref[h // 2, (h % 2) * rows:(h % 2 + 1) * rows, :] = tab


def _bias_table(rel_bias, *, cps, keys):
    rows = cps * CHUNK
    L = ((rows + keys + LANES - 1) // LANES) * LANES
    n_const = rows + ATT_REACH - MAX_REL
    const = jnp.broadcast_to(rel_bias[:, 2 * MAX_REL:], (ATT_HEADS, n_const))
    ramp = rel_bias[:, ::-1]
    tail = jnp.broadcast_to(rel_bias[:, :1], (ATT_HEADS, LANES))
    g = jnp.concatenate([const, ramp, tail], axis=1)[:, :L].reshape(ATT_HEADS, 1, L)
    kern = functools.partial(_bias_table_kernel, rows=rows, keys=keys)
    return pl.pallas_call(
        kern,
        out_shape=jax.ShapeDtypeStruct((HEAD_PAIRS, 2 * rows, keys), F32),
        name="bias_table",
    )(g)


def _attn_kernel(q_ref, kc_ref, vc_ref, kp_ref, vp_ref, bias_ref, o_ref, kk_ref, vv_ref,
                 *, tq, cps, keys, mask_first, past_feature_major):
    i = pl.program_id(1)
    rows = cps * CHUNK
    if past_feature_major:
        kk_ref[0:ATT_REACH, :] = kp_ref[0].T.astype(BF16)
        vv_ref[0:ATT_REACH, :] = vp_ref[0].T.astype(BF16)
    else:
        kk_ref[0:ATT_REACH, :] = kp_ref[0].astype(BF16)
        vv_ref[0:ATT_REACH, :] = vp_ref[0].astype(BF16)
    kk_ref[ATT_REACH:ATT_REACH + tq, :] = kc_ref[0]
    vv_ref[ATT_REACH:ATT_REACH + tq, :] = vc_ref[0]
    total = kk_ref.shape[0]
    if total > ATT_REACH + tq:
        kk_ref[ATT_REACH + tq:, :] = jnp.zeros((total - ATT_REACH - tq, ATT_WIDTH), BF16)
        vv_ref[ATT_REACH + tq:, :] = jnp.zeros((total - ATT_REACH - tq, ATT_WIDTH), BF16)

    lane = lax.broadcasted_iota(jnp.int32, (rows, LANES), 1)
    first_head = lane < ATT_HEAD_DIM

    def tile(no_past):
        for sub in range(tq // rows):
            off = sub * rows
            first_valid = max(ATT_REACH - off, 0) if no_past else 0
            k0 = first_valid // LANES * LANES
            ones_cols = jnp.ones((keys - k0, LANES), BF16)
            for pair in range(HEAD_PAIRS):
                ls = slice(pair * LANES, (pair + 1) * LANES)
                q2 = q_ref[0, off:off + rows, ls]
                zero = jnp.zeros_like(q2)
                qs = jnp.concatenate([jnp.where(first_head, q2, zero), jnp.where(first_head, zero, q2)], axis=0)
                s = lax.dot_general(qs, kk_ref[off + k0:off + keys, ls], NT_DIMS,
                                    preferred_element_type=F32)
                s = s + bias_ref[pair, :, k0:keys]
                if first_valid > k0:
                    col = lax.broadcasted_iota(jnp.int32, s.shape, 1)
                    s = jnp.where(col >= first_valid - k0, s, NEG_INF)
                m = jnp.max(s, axis=-1, keepdims=True)
                e = jnp.exp(s - m).astype(BF16)
                v_ext = jnp.concatenate([vv_ref[off + k0:off + keys, ls], ones_cols], axis=1)
                o2 = jnp.dot(e, v_ext, preferred_element_type=F32)
                o2 = o2[:, 0:LANES] * (1.0 / o2[:, LANES:2 * LANES])
                o = jnp.where(first_head, o2[:rows], o2[rows:])
                o_ref[0, off:off + rows, ls] = o.astype(BF16)

    if mask_first:
        pl.when(i == 0)(lambda: tile(True))
        pl.when(i > 0)(lambda: tile(False))
    else:
        tile(False)


def _band_attn(qkva, k_prev, v_prev, prev_map, bias_tab, *, tq, cps, keys, mask_first, past_feature_major):
    B, T, _ = qkva.shape
    nt = T // tq
    rows_total = max(ATT_REACH + tq, (tq // (cps * CHUNK) - 1) * cps * CHUNK + keys)
    kern = functools.partial(_attn_kernel, tq=tq, cps=cps, keys=keys, mask_first=mask_first,
                             past_feature_major=past_feature_major)
    W = ATT_WIDTH
    prev_block = (1, ATT_REACH, W)
    return pl.pallas_call(
        kern,
        grid=(B, nt),
        in_specs=[
            pl.BlockSpec((1, tq, W), lambda b, i: (b, i, 0)),
            pl.BlockSpec((1, tq, W), lambda b, i: (b, i, 1)),
            pl.BlockSpec((1, tq, W), lambda b, i: (b, i, 2)),
            pl.BlockSpec(prev_block, prev_map[0]),
            pl.BlockSpec(prev_block, prev_map[1]),
            pl.BlockSpec(bias_tab.shape, lambda b, i: (0, 0, 0)),
        ],
        out_specs=pl.BlockSpec((1, tq, W), lambda b, i: (b, i, 0)),
        out_shape=jax.ShapeDtypeStruct((B, T, W), BF16),
        scratch_shapes=[pltpu.VMEM((rows_total, W), BF16), pltpu.VMEM((rows_total, W), BF16)],
        compiler_params=_cparams(("parallel", "arbitrary")),
        name="band_attn",
    )(qkva, qkva, qkva, k_prev, v_prev, bias_tab)


def _lane_mean(x, mean_w):
    hi = x.astype(BF16)
    lo = (x - hi.astype(F32)).astype(BF16)
    return jnp.dot(jnp.concatenate([hi, lo], axis=1), mean_w, preferred_element_type=F32)


def _mlstm_kernel(q_ref, kt_ref, v_ref, gates_ref, og_ref, ng_ref, c0_ref, n0_ref, m0_ref,
                  mb_ref, c_out_ref, n_out_ref, m_out_ref, cn_s, m_s, *, lc, bb, nc):
    i = pl.program_id(1)
    last = pl.num_programs(1) - 1
    H, d = MLSTM_HEADS, MLSTM_HEAD_DIM

    @pl.when(i == 0)
    def _():
        m_s[...] = jnp.zeros(m_s.shape, F32)
        for bi in range(bb):
            for h in range(H):
                cn_s[bi * H + h, :, 0:d] = c0_ref[bi, h].T
                cn_s[bi * H + h, :, d:2 * d] = jnp.broadcast_to(n0_ref[bi, h:h + 1, :], (d, d)).T
            m_s[bi, 0:H, :] = jnp.broadcast_to(m0_ref[bi], (H, LANES))

    ti = lax.broadcasted_iota(jnp.int32, (lc, lc), 0)
    si = lax.broadcasted_iota(jnp.int32, (lc, lc), 1)
    causal = ti >= si
    ones_cols = jnp.ones((lc, d), BF16)
    mean_w = jnp.full((2 * d, d), 1.0 / d, BF16)

    a_rows, decays, col_sets = {}, {}, {}
    m_cur = [m_s[bi, :, 0:1] for bi in range(bb)]
    for c in range(nc):
        for bi in range(bb):
            b = gates_ref[c, bi, 0:8, :]
            a = gates_ref[c, bi, 8:16, :]
            cm = gates_ref[c, bi, 16:24, :]
            m_prev = m_cur[bi]
            inter = b + m_prev
            m_t = jnp.maximum(inter, b + cm)
            b_last = b[:, lc - 1:lc]
            m_new = jnp.maximum(b_last + m_prev, b_last + cm[:, lc - 1:lc])
            w_end = jnp.exp(b_last + a - m_new)
            per_frame = [b - m_t, jnp.exp(inter - m_t), jnp.exp(-m_t), w_end]
            stacked = jnp.concatenate(per_frame + [jnp.zeros((LANES - 8 * len(per_frame), lc), F32)], axis=0)
            col_sets[c, bi] = stacked.T
            a_rows[c, bi] = a
            decays[c, bi] = jnp.exp(b_last + m_prev - m_new)
            m_cur[bi] = m_new
    for bi in range(bb):
        m_s[bi] = jnp.broadcast_to(m_cur[bi], (8, LANES))

    heads = [(bi, h) for bi in range(bb) for h in range(H)]
    inst = [(c, bi, h) for c in range(nc) for bi, h in heads]
    hsl = lambda h: slice(h * d, (h + 1) * d)
    rsl = lambda c: slice(c * lc, (c + 1) * lc)
    col = lambda c, bi, h, j: col_sets[c, bi][:, 8 * j + h:8 * j + h + 1]
    qs = {(c, bi, h): q_ref[bi, rsl(c), hsl(h)] for c, bi, h in inst}
    kts = {(c, bi, h): kt_ref[bi, hsl(h), rsl(c)] for c, bi, h in inst}
    vs = {(c, bi, h): v_ref[bi, rsl(c), hsl(h)] for c, bi, h in inst}

    qk = {k_: jnp.dot(qs[k_], kts[k_], preferred_element_type=F32) for k_ in inst}
    cn_cur = {(bi, h): cn_s[bi * H + h] for bi, h in heads}
    qcn = {}
    for c in range(nc):
        for bi, h in heads:
            qcn[c, bi, h] = jnp.dot(qs[c, bi, h], cn_cur[bi, h].astype(BF16), preferred_element_type=F32)
        for bi, h in heads:
            w_end_col = jnp.broadcast_to(col(c, bi, h, 3), (lc, d))
            vw = jnp.concatenate(
                [(vs[c, bi, h].astype(F32) * w_end_col).astype(BF16), w_end_col.astype(BF16)], axis=1)
            cn_cur[bi, h] = (decays[c, bi][h:h + 1, :] * cn_cur[bi, h]
                             + jnp.dot(kts[c, bi, h], vw, preferred_element_type=F32))
    for bi, h in heads:
        cn_s[bi * H + h] = cn_cur[bi, h]
    w = {(c, bi, h): jnp.exp(jnp.where(causal, col(c, bi, h, 0) + a_rows[c, bi][h:h + 1, :], -jnp.inf))
         * qk[c, bi, h] for c, bi, h in inst}
    wv = {k_: jnp.dot(w[k_].astype(BF16), jnp.concatenate([vs[k_], ones_cols], axis=1),
                      preferred_element_type=F32) for k_ in inst}
    hh = {}
    for c, bi, h in inst:
        tot = col(c, bi, h, 1) * qcn[c, bi, h] + wv[c, bi, h]
        den = tot[:, d:2 * d]
        hh[c, bi, h] = tot[:, 0:d] * (1.0 / jnp.maximum(jnp.abs(den), col(c, bi, h, 2)))
    xc = {k_: hh[k_] - _lane_mean(hh[k_], mean_w) for k_ in inst}
    var = {k_: _lane_mean(xc[k_] * xc[k_], mean_w) for k_ in inst}
    for c, bi, h in inst:
        hn = xc[c, bi, h] * lax.rsqrt(var[c, bi, h] + LN_EPS) * ng_ref[:, hsl(h)]
        mb_ref[bi, rsl(c), hsl(h)] = (og_ref[bi, rsl(c), hsl(h)] * hn).astype(BF16)

    @pl.when(i == last)
    def _():
        for bi in range(bb):
            for h in range(H):
                cn = cn_s[bi * H + h]
                c_out_ref[bi, h] = cn[:, 0:d].T
                n_out_ref[bi, h:h + 1, :] = cn[:, d:2 * d].T[0:1, :]
            m_out_ref[bi] = m_s[bi, 0:H, 0:1]


def _mlstm(qvm, kt, gates, og, norm_g, c0, n0, m0, *, lc, bb, nc):
    B, T, _ = qvm.shape
    ts = nc * lc
    nt = T // ts
    W = MLSTM_WIDTH
    H, d = MLSTM_HEADS, MLSTM_HEAD_DIM
    kern = functools.partial(_mlstm_kernel, lc=lc, bb=bb, nc=nc)
    return pl.pallas_call(
        kern,
        grid=(B // bb, nt),
        in_specs=[
            pl.BlockSpec((bb, ts, W), lambda b, i: (b, i, 0)),
            pl.BlockSpec((bb, W, ts), lambda b, i: (b, 0, i)),
            pl.BlockSpec((bb, ts, W), lambda b, i: (b, i, 1)),
            pl.BlockSpec((nc, bb, GATE_ROWS, lc), lambda b, i: (i, b, 0, 0)),
            pl.BlockSpec((bb, ts, W), lambda b, i: (b, i, 0)),
            pl.BlockSpec((1, W), lambda b, i: (0, 0)),
            pl.BlockSpec((bb, H, d, d), lambda b, i: (b, 0, 0, 0)),
            pl.BlockSpec((bb, H, d), lambda b, i: (b, 0, 0)),
            pl.BlockSpec((bb, H, 1), lambda b, i: (b, 0, 0)),
        ],
        out_specs=[
            pl.BlockSpec((bb, ts, W), lambda b, i: (b, i, 0)),
            pl.BlockSpec((bb, H, d, d), lambda b, i: (b, 0, 0, 0)),
            pl.BlockSpec((bb, H, d), lambda b, i: (b, 0, 0)),
            pl.BlockSpec((bb, H, 1), lambda b, i: (b, 0, 0)),
        ],
        out_shape=[
            jax.ShapeDtypeStruct((B, T, W), BF16),
            jax.ShapeDtypeStruct((B, H, d, d), F32),
            jax.ShapeDtypeStruct((B, H, d), F32),
            jax.ShapeDtypeStruct((B, H, 1), F32),
        ],
        scratch_shapes=[
            pltpu.VMEM((bb * H, d, 2 * d), F32),
            pltpu.VMEM((bb, 8, LANES), F32),
        ],
        compiler_params=_cparams(("parallel", "arbitrary")),
        name="mlstm",
    )(qvm, kt, qvm, gates, og, norm_g, c0, n0, m0)


def _layer_norm(z, g, b):
    mu = jnp.mean(z, axis=-1, keepdims=True)
    zc = z - mu
    var = jnp.mean(zc * zc, axis=-1, keepdims=True)
    return zc * lax.rsqrt(var + LN_EPS) * g + b


def _merge_ffn_kernel(x_ref, att_ref, mb_ref, wo_ref, g1_ref, b1_ref, w1_ref, bf1_ref, w2_ref, bf2_ref,
                      g2_ref, b2_ref, y_ref, *, fc, parts):
    W = ATT_WIDTH
    tm = x_ref.shape[1]
    rp = tm // parts
    groups = [slice(p * rp, (p + 1) * rp) for p in range(parts)]
    mixes = [jnp.dot(att_ref[0, rs, :], wo_ref[0:W, :], preferred_element_type=F32)
             + jnp.dot(mb_ref[0, rs, :], wo_ref[W:2 * W, :], preferred_element_type=F32) for rs in groups]
    for rs, mix in zip(groups, mixes):
        h = _layer_norm(DEEPNORM_ALPHA * x_ref[0, rs, :] + mix, g1_ref[...], b1_ref[...])
        hb = h.astype(BF16)
        f = jnp.zeros(h.shape, F32)
        for j in range(D_FF // fc):
            cs = slice(j * fc, (j + 1) * fc)
            a = jnp.dot(hb, w1_ref[:, cs], preferred_element_type=F32) + bf1_ref[:, cs]
            a = jnp.square(jnp.maximum(a, 0.0)).astype(BF16)
            f = f + jnp.dot(a, w2_ref[cs, :], preferred_element_type=F32)
        f = f + bf2_ref[...]
        y_ref[0, rs, :] = _layer_norm(DEEPNORM_ALPHA * h + f, g2_ref[...], b2_ref[...])


def _merge_ffn(x, att, mb, wo, g1, b1, w1, bf1, w2, bf2, g2, b2, *, tm, fc, parts):
    B, T, D = x.shape
    nt = T // tm
    W = ATT_WIDTH
    const = lambda shape: pl.BlockSpec(shape, lambda b, i: (0, 0), pipeline_mode=pl.Buffered(1))
    kern = functools.partial(_merge_ffn_kernel, fc=fc, parts=parts)
    return pl.pallas_call(
        kern,
        grid=(B, nt),
        in_specs=[
            pl.BlockSpec((1, tm, D), lambda b, i: (b, i, 0)),
            pl.BlockSpec((1, tm, W), lambda b, i: (b, i, 0)),
            pl.BlockSpec((1, tm, W), lambda b, i: (b, i, 0)),
            const((D, D)), const((1, D)), const((1, D)),
            const((D, D_FF)), const((1, D_FF)), const((D_FF, D)), const((1, D)),
            const((1, D)), const((1, D)),
        ],
        out_specs=pl.BlockSpec((1, tm, D), lambda b, i: (b, i, 0)),
        out_shape=jax.ShapeDtypeStruct((B, T, D), F32),
        compiler_params=_cparams(("parallel", "parallel")),
        name="merge_ffn",
    )(x, att, mb, wo, g1, b1, w1, bf1, w2, bf2, g2, b2)


def _token_tile(t):
    return 512 if t % 512 == 0 else t


def kernel(x_prompt, x_sample, cache_k, cache_v, state_C, state_n, state_m, w_in, b_in, rel_bias,
           mlstm_norm_g, w_out, ln1_g, ln1_b, w_ff1, b_ff1, w_ff2, b_ff2, ln2_g, ln2_b):
    assert w_in.shape[0] == DEPTH == 1
    B, S, D = x_prompt.shape
    DB, T, _ = x_sample.shape
    H, d = MLSTM_HEADS, MLSTM_HEAD_DIM
    W = MLSTM_WIDTH
    assert S % ATT_REACH == 0 and T == CHUNK and cache_k.shape[2] == ATT_REACH

    n_gate = 2 * H
    wi, bi_ = w_in[0], b_in[0]
    gate_pad = GATE_ROWS_PAD - n_gate
    w_all, wkt = _proj_weights(wi.T)
    b_all = bi_[None, :]
    bkt = jnp.pad(jnp.concatenate([bi_[OFF_KB:OFF_VB], bi_[OFF_GATES:]]), (0, gate_pad))[:, None]
    wo = w_out[0].astype(BF16)
    w1 = w_ff1[0].astype(BF16)
    w2 = w_ff2[0].astype(BF16)
    row = lambda p: p[0][None, :]
    ffn_params = (wo, row(ln1_g), row(ln1_b), w1, row(b_ff1), w2, row(b_ff2), row(ln2_g), row(ln2_b))
    norm_g = row(mlstm_norm_g)
    cps = ATTN_CHUNKS_PER_WINDOW
    keys = -(-(ATT_REACH + cps * CHUNK) // LANES) * LANES
    bias_tab = _bias_table(rel_bias[0], cps=cps, keys=keys)

    def layer(x, k_prev, v_prev, prev_map, c0, n0, m0, *, seqs, mask_first):
        Bx, Tx, _ = x.shape
        n_tok = Bx * Tx
        t_seq = n_tok // seqs
        tm = _token_tile(Tx)
        tp = IN_PROJ_TILE if Tx % IN_PROJ_TILE == 0 else tm
        keep = min(ATT_REACH, Tx)
        lc = min(MLSTM_CHUNK, t_seq)
        qkva, qvm, kt, og, gates, k_last, v_last = _in_proj(x, w_all, b_all, wkt, bkt, tm=tp, keep=keep, lc=lc)
        seq = lambda a: a.reshape(seqs, t_seq, a.shape[-1])
        tq = ATTN_TILE if t_seq % ATTN_TILE == 0 else _token_tile(t_seq)
        past_feature_major = k_prev is not None
        if k_prev is None:
            k_prev = v_prev = seq(qkva)
            step = tq // ATT_REACH
            prev_map = (lambda b, i: (b, jnp.maximum(step * i - 1, 0), 1),
                        lambda b, i: (b, jnp.maximum(step * i - 1, 0), 2))
        att = _band_attn(seq(qkva), k_prev, v_prev, prev_map, bias_tab,
                         tq=tq, cps=cps, keys=keys, mask_first=mask_first,
                         past_feature_major=past_feature_major)
        gates = gates.reshape(t_seq // lc, seqs, GATE_ROWS, lc)
        kt = kt.reshape(Bx, W, seqs // Bx, t_seq).transpose(0, 2, 1, 3).reshape(seqs, W, t_seq)
        mb, c_new, n_new, m_new = _mlstm(seq(qvm), kt, gates, seq(og), norm_g, c0, n0, m0,
                                         lc=lc, bb=min(seqs, MLSTM_STREAMS),
                                         nc=min(MLSTM_CHUNKS_PER_STEP, t_seq // lc))
        tok = lambda a: a.reshape(Bx, Tx, a.shape[-1])
        tf = FFN_TILE if Tx % FFN_TILE == 0 else tm
        y = _merge_ffn(x, tok(att), tok(mb), *ffn_params, tm=tf, fc=FFN_CHUNK, parts=max(tf // FFN_GROUP, 1))
        return y, k_last, v_last, c_new, n_new, m_new

    zeros_c = jnp.zeros((B, H, d, d), F32)
    zeros_n = jnp.zeros((B, H, d), F32)
    zeros_m = jnp.zeros((B, H, 1), F32)
    yp, kp, vp, cp, np_, mp = layer(x_prompt, None, None, None, zeros_c, zeros_n, zeros_m,
                                    seqs=B, mask_first=True)

    ck = cache_k[0].transpose(0, 2, 3, 1).reshape(DB, ATT_WIDTH, ATT_REACH)
    cv = cache_v[0].transpose(0, 2, 3, 1).reshape(DB, ATT_WIDTH, ATT_REACH)
    prev_sample = (lambda b, i: (b, 0, 0), lambda b, i: (b, 0, 0))
    ys, ks, vs, cs, ns, ms = layer(
        x_sample.reshape(1, DB * T, D), ck, cv, prev_sample,
        state_C[0].astype(F32), state_n[0].astype(F32), state_m[0].astype(F32).reshape(DB, H, 1),
        seqs=DB, mask_first=False)

    sd = state_C.dtype
    def heads(a, nb, t):
        bx = a.shape[0]
        a = a.reshape(bx, ATT_HEADS, ATT_HEAD_DIM, nb // bx, t)
        return a.transpose(0, 3, 4, 1, 2).reshape(nb, t, ATT_HEADS, ATT_HEAD_DIM)[None]
    keep_p = min(ATT_REACH, S)
    return (yp, ys.reshape(DB, T, D),
            heads(kp, B, keep_p).astype(cache_k.dtype), heads(vp, B, keep_p).astype(cache_v.dtype),
            cp[None].astype(sd), np_[None].astype(sd), mp.reshape(1, B, H).astype(sd),
            heads(ks, DB, T).astype(cache_k.dtype), heads(vs, DB, T).astype(cache_v.dtype),
            cs[None].astype(sd), ns[None].astype(sd), ms.reshape(1, DB, H).astype(sd))
```

```python
import functools

import jax
import jax.numpy as jnp
from jax import lax
from jax.experimental import pallas as pl
from jax.experimental.pallas import tpu as pltpu

F32 = jnp.float32
BF16 = jnp.bfloat16

D_MODEL = 1024
CHUNK = 64
LEFT_CHUNKS = 8
ATT_REACH = LEFT_CHUNKS * CHUNK
ATT_WIDTH = 512
MLSTM_WIDTH = 512
ATT_HEADS = 8
ATT_HEAD_DIM = 64
MLSTM_HEADS = 4
MLSTM_HEAD_DIM = 128
MAX_REL = 128
D_FF = 4 * D_MODEL
DEPTH = 1
DEEPNORM_ALPHA = (2 * DEPTH) ** 0.25
LN_EPS = 1e-5
NEG_INF = -1e30
ATT_SCALE = ATT_HEAD_DIM ** -0.5
KB_SCALE = MLSTM_HEAD_DIM ** -0.5

OFF_QB = 3 * ATT_WIDTH
OFF_KB = OFF_QB + MLSTM_WIDTH
OFF_VB = OFF_KB + MLSTM_WIDTH
OFF_OB = OFF_VB + MLSTM_WIDTH
OFF_GATES = OFF_OB + MLSTM_WIDTH

LANES = 128
GATE_ROWS_PAD = 16
MLSTM_CHUNK = 256
MLSTM_CHUNKS_PER_STEP = 4
MLSTM_STREAMS = 2
GATE_ROWS = 24
IN_PROJ_TILE = 1024
ATTN_TILE = 1024
ATTN_CHUNKS_PER_WINDOW = 1
FFN_TILE = 1024
FFN_GROUP = 256
FFN_CHUNK = 2048
HEAD_PAIRS = ATT_HEADS // 2
VMEM_LIMIT = 56 * 1024 * 1024
NT_DIMS = (((1,), (1,)), ((), ()))


def _cparams(sem):
    return pltpu.CompilerParams(dimension_semantics=sem, vmem_limit_bytes=VMEM_LIMIT)


def _in_proj_kernel(x_ref, w_ref, b_ref, wkt_ref, bkt_ref,
                    qkva_ref, qvm_ref, kt_ref, og_ref, gates_ref, klast_ref, vlast_ref, *, tm, keep, lc):
    i = pl.program_id(1)
    last = pl.num_programs(1) - 1
    xb = x_ref[0].astype(BF16)

    def proj(c0, width):
        return (jnp.dot(xb, w_ref[:, c0:c0 + width], preferred_element_type=F32)
                + b_ref[:, c0:c0 + width])

    W = ATT_WIDTH
    H = MLSTM_HEADS
    ktg = lax.dot_general(wkt_ref[...], xb, NT_DIMS, preferred_element_type=F32) + bkt_ref[...]
    kt_ref[0] = (ktg[0:W] * KB_SCALE).astype(BF16)
    g = ktg[W:W + 2 * H]
    log_sig = jnp.minimum(g, 0.0) - jnp.log1p(jnp.exp(-jnp.abs(g)))
    head_row = lax.broadcasted_iota(jnp.int32, g.shape, 0) < H
    g8 = jnp.where(head_row, g, log_sig)
    pos = lax.broadcasted_iota(jnp.int32, g8.shape, 1) % lc
    csum = g8
    sh = 1
    while sh < lc:
        csum = csum + jnp.where(pos >= sh, pltpu.roll(csum, sh, 1), 0.0)
        sh *= 2
    b = pltpu.roll(csum, H, 0)
    a = g8 - b
    cm = a
    sh = 1
    while sh < lc:
        cm = jnp.maximum(cm, jnp.where(pos >= sh, pltpu.roll(cm, sh, 1), -jnp.inf))
        sh *= 2
    rows = [jnp.where(head_row, t, 0.0) for t in (b, a, cm)]
    for c in range(tm // lc):
        for j, t in enumerate(rows):
            gates_ref[c, 0, 8 * j:8 * j + 8, :] = t[:, c * lc:(c + 1) * lc]

    qvm_ref[0, :, 0:W] = proj(OFF_QB, W).astype(BF16)
    qvm_ref[0, :, W:2 * W] = proj(OFF_VB, W).astype(BF16)
    ob = proj(OFF_OB, W)
    og_ref[0] = 1.0 / (1.0 + jnp.exp(-ob))
    qa = proj(0, W)
    qkva_ref[0, :, 0:W] = (qa * ATT_SCALE).astype(BF16)
    ka = proj(W, W)
    qkva_ref[0, :, W:2 * W] = ka.astype(BF16)
    va = proj(2 * W, W)
    qkva_ref[0, :, 2 * W:3 * W] = va.astype(BF16)

    @pl.when(i == last)
    def _():
        klast_ref[0] = ka[tm - keep:, :].T
        vlast_ref[0] = va[tm - keep:, :].T


def _in_proj(x, w, b, wkt, bkt, *, tm, keep, lc):
    B, T, D = x.shape
    nt = T // tm
    W = ATT_WIDTH
    kern = functools.partial(_in_proj_kernel, tm=tm, keep=keep, lc=lc)
    return pl.pallas_call(
        kern,
        grid=(B, nt),
        in_specs=[
            pl.BlockSpec((1, tm, D), lambda b, i: (b, i, 0)),
            pl.BlockSpec(w.shape, lambda b, i: (0, 0), pipeline_mode=pl.Buffered(1)),
            pl.BlockSpec(b.shape, lambda b, i: (0, 0), pipeline_mode=pl.Buffered(1)),
            pl.BlockSpec(wkt.shape, lambda b, i: (0, 0), pipeline_mode=pl.Buffered(1)),
            pl.BlockSpec(bkt.shape, lambda b, i: (0, 0), pipeline_mode=pl.Buffered(1)),
        ],
        out_specs=[
            pl.BlockSpec((1, tm, 3 * W), lambda b, i: (b, i, 0)),
            pl.BlockSpec((1, tm, 2 * W), lambda b, i: (b, i, 0)),
            pl.BlockSpec((1, W, tm), lambda b, i: (b, 0, i)),
            pl.BlockSpec((1, tm, W), lambda b, i: (b, i, 0)),
            pl.BlockSpec((tm // lc, 1, GATE_ROWS, lc), lambda b, i: (i, b, 0, 0)),
            pl.BlockSpec((1, W, keep), lambda b, i: (b, 0, 0)),
            pl.BlockSpec((1, W, keep), lambda b, i: (b, 0, 0)),
        ],
        out_shape=[
            jax.ShapeDtypeStruct((B, T, 3 * W), BF16),
            jax.ShapeDtypeStruct((B, T, 2 * W), BF16),
            jax.ShapeDtypeStruct((B, W, T), BF16),
            jax.ShapeDtypeStruct((B, T, W), F32),
            jax.ShapeDtypeStruct((T // lc, B, GATE_ROWS, lc), F32),
            jax.ShapeDtypeStruct((B, W, keep), F32),
            jax.ShapeDtypeStruct((B, W, keep), F32),
        ],
        compiler_params=_cparams(("parallel", "arbitrary")),
        name="in_proj",
    )(x, w, b, wkt, bkt)


def _proj_weights_kernel(wt_ref, wg_ref, w_ref, wkt_ref):
    j = pl.program_id(0)
    W = wt_ref.shape[0]
    w_ref[...] = wt_ref[...].T.astype(BF16)

    @pl.when(j == OFF_KB // W)
    def _():
        wkt_ref[0:W, :] = wt_ref[...].astype(BF16)

    @pl.when(j == 0)
    def _():
        pad = jnp.zeros((GATE_ROWS_PAD - wg_ref.shape[0], wg_ref.shape[1]), F32)
        wkt_ref[W:W + GATE_ROWS_PAD, :] = jnp.concatenate([wg_ref[...], pad], axis=0).astype(BF16)


def _proj_weights(wt):
    n_in, D = wt.shape
    W = MLSTM_WIDTH
    n_gate = 2 * MLSTM_HEADS
    return pl.pallas_call(
        _proj_weights_kernel,
        grid=(OFF_GATES // W,),
        in_specs=[pl.BlockSpec((W, D), lambda j: (j, 0)),
                  pl.BlockSpec((n_gate, D), lambda j: (OFF_GATES // n_gate, 0))],
        out_specs=[pl.BlockSpec((D, W), lambda j: (0, j)),
                   pl.BlockSpec((W + GATE_ROWS_PAD, D), lambda j: (0, 0))],
        out_shape=[jax.ShapeDtypeStruct((D, OFF_GATES), BF16),
                   jax.ShapeDtypeStruct((W + GATE_ROWS_PAD, D), BF16)],
        compiler_params=_cparams(("arbitrary",)),
        name="proj_weights",
    )(wt, wt)


def _bias_table_kernel(g_ref, o_ref, *, rows, keys):
    L = g_ref.shape[-1]
    r = lax.broadcasted_iota(jnp.int32, (rows, keys), 0)
    c = lax.broadcasted_iota(jnp.int32, (rows, keys), 1)
    band_start = (r // CHUNK) * CHUNK
    in_band = (c >= band_start) & (c < band_start + ATT_REACH + CHUNK)
    for h in range(ATT_HEADS):
        g = jnp.broadcast_to(g_ref[h], (rows, L))
        t = pltpu.roll(g, L - rows, 1, stride=1, stride_axis=0)
        tab = jnp.where(in_band, t[:, :keys], NEG_INF)
        o_ref[h // 2, (h % 2) * rows:(h % 2 + 1) * rows, :] = tab


def _bias_table(rel_bias, *, cps, keys):
    rows = cps * CHUNK
    L = ((rows + keys + LANES - 1) // LANES) * LANES
    n_const = rows + ATT_REACH - MAX_REL
    const = jnp.broadcast_to(rel_bias[:, 2 * MAX_REL:], (ATT_HEADS, n_const))
    ramp = rel_bias[:, ::-1]
    tail = jnp.broadcast_to(rel_bias[:, :1], (ATT_HEADS, LANES))
    g = jnp.concatenate([const, ramp, tail], axis=1)[:, :L].reshape(ATT_HEADS, 1, L)
    kern = functools.partial(_bias_table_kernel, rows=rows, keys=keys)
    return pl.pallas_call(
        kern,
        out_shape=jax.ShapeDtypeStruct((HEAD_PAIRS, 2 * rows, keys), F32),
        name="bias_table",
    )(g)


def _attn_kernel(q_ref, kc_ref, vc_ref, kp_ref, vp_ref, bias_ref, o_ref, kk_ref, vv_ref,
                 *, tq, cps, keys, mask_first, past_feature_major):
    i = pl.program_id(1)
    rows = cps * CHUNK
    if past_feature_major:
        kk_ref[0:ATT_REACH, :] = kp_ref[0].T.astype(BF16)
        vv_ref[0:ATT_REACH, :] = vp_ref[0].T.astype(BF16)
    else:
        kk_ref[0:ATT_REACH, :] = kp_ref[0].astype(BF16)
        vv_ref[0:ATT_REACH, :] = vp_ref[0].astype(BF16)
    kk_ref[ATT_REACH:ATT_REACH + tq, :] = kc_ref[0]
    vv_ref[ATT_REACH:ATT_REACH + tq, :] = vc_ref[0]
    total = kk_ref.shape[0]
    if total > ATT_REACH + tq:
        kk_ref[ATT_REACH + tq:, :] = jnp.zeros((total - ATT_REACH - tq, ATT_WIDTH), BF16)
        vv_ref[ATT_REACH + tq:, :] = jnp.zeros((total - ATT_REACH - tq, ATT_WIDTH), BF16)

    lane = lax.broadcasted_iota(jnp.int32, (rows, LANES), 1)
    first_head = lane < ATT_HEAD_DIM

    def tile(no_past):
        for sub in range(tq // rows):
            off = sub * rows
            first_valid = max(ATT_REACH - off, 0) if no_past else 0
            k0 = first_valid // LANES * LANES
            ones_cols = jnp.ones((keys - k0, LANES), BF16)
            for pair in range(HEAD_PAIRS):
                ls = slice(pair * LANES, (pair + 1) * LANES)
                q2 = q_ref[0, off:off + rows, ls]
                zero = jnp.zeros_like(q2)
                qs = jnp.concatenate([jnp.where(first_head, q2, zero), jnp.where(first_head, zero, q2)], axis=0)
                s = lax.dot_general(qs, kk_ref[off + k0:off + keys, ls], NT_DIMS,
                                    preferred_element_type=F32)
                s = s + bias_ref[pair, :, k0:keys]
                if first_valid > k0:
                    col = lax.broadcasted_iota(jnp.int32, s.shape, 1)
                    s = jnp.where(col >= first_valid - k0, s, NEG_INF)
                m = jnp.max(s, axis=-1, keepdims=True)
                e = jnp.exp(s - m).astype(BF16)
                v_ext = jnp.concatenate([vv_ref[off + k0:off + keys, ls], ones_cols], axis=1)
                o2 = jnp.dot(e, v_ext, preferred_element_type=F32)
                o2 = o2[:, 0:LANES] * (1.0 / o2[:, LANES:2 * LANES])
                o = jnp.where(first_head, o2[:rows], o2[rows:])
                o_ref[0, off:off + rows, ls] = o.astype(BF16)

    if mask_first:
        pl.when(i == 0)(lambda: tile(True))
        pl.when(i > 0)(lambda: tile(False))
    else:
        tile(False)


def _band_attn(qkva, k_prev, v_prev, prev_map, bias_tab, *, tq, cps, keys, mask_first, past_feature_major):
    B, T, _ = qkva.shape
    nt = T // tq
    rows_total = max(ATT_REACH + tq, (tq // (cps * CHUNK) - 1) * cps * CHUNK + keys)
    kern = functools.partial(_attn_kernel, tq=tq, cps=cps, keys=keys, mask_first=mask_first,
                             past_feature_major=past_feature_major)
    W = ATT_WIDTH
    prev_block = (1, ATT_REACH, W)
    return pl.pallas_call(
        kern,
        grid=(B, nt),
        in_specs=[
            pl.BlockSpec((1, tq, W), lambda b, i: (b, i, 0)),
            pl.BlockSpec((1, tq, W), lambda b, i: (b, i, 1)),
            pl.BlockSpec((1, tq, W), lambda b, i: (b, i, 2)),
            pl.BlockSpec(prev_block, prev_map[0]),
            pl.BlockSpec(prev_block, prev_map[1]),
            pl.BlockSpec(bias_tab.shape, lambda b, i: (0, 0, 0)),
        ],
        out_specs=pl.BlockSpec((1, tq, W), lambda b, i: (b, i, 0)),
        out_shape=jax.ShapeDtypeStruct((B, T, W), BF16),
        scratch_shapes=[pltpu.VMEM((rows_total, W), BF16), pltpu.VMEM((rows_total, W), BF16)],
        compiler_params=_cparams(("parallel", "arbitrary")),
        name="band_attn",
    )(qkva, qkva, qkva, k_prev, v_prev, bias_tab)


def _lane_mean(x, mean_w):
    hi = x.astype(BF16)
    lo = (x - hi.astype(F32)).astype(BF16)
    return jnp.dot(jnp.concatenate([hi, lo], axis=1), mean_w, preferred_element_type=F32)


def _mlstm_kernel(q_ref, kt_ref, v_ref, gates_ref, og_ref, ng_ref, c0_ref, n0_ref, m0_ref,
                  mb_ref, c_out_ref, n_out_ref, m_out_ref, cn_s, m_s, *, lc, bb, nc):
    i = pl.program_id(1)
    last = pl.num_programs(1) - 1
    H, d = MLSTM_HEADS, MLSTM_HEAD_DIM

    @pl.when(i == 0)
    def _():
        m_s[...] = jnp.zeros(m_s.shape, F32)
        for bi in range(bb):
            for h in range(H):
                cn_s[bi * H + h, :, 0:d] = c0_ref[bi, h].T
                cn_s[bi * H + h, :, d:2 * d] = jnp.broadcast_to(n0_ref[bi, h:h + 1, :], (d, d)).T
            m_s[bi, 0:H, :] = jnp.broadcast_to(m0_ref[bi], (H, LANES))

    ti = lax.broadcasted_iota(jnp.int32, (lc, lc), 0)
    si = lax.broadcasted_iota(jnp.int32, (lc, lc), 1)
    causal = ti >= si
    ones_cols = jnp.ones((lc, d), BF16)
    mean_w = jnp.full((2 * d, d), 1.0 / d, BF16)

    a_rows, decays, col_sets = {}, {}, {}
    m_cur = [m_s[bi, :, 0:1] for bi in range(bb)]
    for c in range(nc):
        for bi in range(bb):
            b = gates_ref[c, bi, 0:8, :]
            a = gates_ref[c, bi, 8:16, :]
            cm = gates_ref[c, bi, 16:24, :]
            m_prev = m_cur[bi]
            inter = b + m_prev
            m_t = jnp.maximum(inter, b + cm)
            b_last = b[:, lc - 1:lc]
            m_new = jnp.maximum(b_last + m_prev, b_last + cm[:, lc - 1:lc])
            w_end = jnp.exp(b_last + a - m_new)
            per_frame = [b - m_t, jnp.exp(inter - m_t), jnp.exp(-m_t), w_end]
            stacked = jnp.concatenate(per_frame, axis=0)
            col_sets[c, bi] = stacked.T
            a_rows[c, bi] = a
            decays[c, bi] = jnp.exp(b_last + m_prev - m_new)
            m_cur[bi] = m_new
    for bi in range(bb):
        m_s[bi] = jnp.broadcast_to(m_cur[bi], (8, LANES))

    heads = [(bi, h) for bi in range(bb) for h in range(H)]
    inst = [(c, bi, h) for c in range(nc) for bi, h in heads]
    hsl = lambda h: slice(h * d, (h + 1) * d)
    rsl = lambda c: slice(c * lc, (c + 1) * lc)
    col = lambda c, bi, h, j: col_sets[c, bi][:, 8 * j + h:8 * j + h + 1]
    qs = {(c, bi, h): q_ref[bi, rsl(c), hsl(h)] for c, bi, h in inst}
    kts = {(c, bi, h): kt_ref[bi, hsl(h), rsl(c)] for c, bi, h in inst}
    vs = {(c, bi, h): v_ref[bi, rsl(c), hsl(h)] for c, bi, h in inst}

    qk = {k_: jnp.dot(qs[k_], kts[k_], preferred_element_type=F32) for k_ in inst}
    cn_cur = {(bi, h): cn_s[bi * H + h] for bi, h in heads}
    qcn = {}
    for c in range(nc):
        for bi, h in heads:
            qcn[c, bi, h] = jnp.dot(qs[c, bi, h], cn_cur[bi, h].astype(BF16), preferred_element_type=F32)
        for bi, h in heads:
            w_end_col = jnp.broadcast_to(col(c, bi, h, 3), (lc, d))
            vw = jnp.concatenate(
                [(vs[c, bi, h].astype(F32) * w_end_col).astype(BF16), w_end_col.astype(BF16)], axis=1)
            cn_cur[bi, h] = (decays[c, bi][h:h + 1, :] * cn_cur[bi, h]
                             + jnp.dot(kts[c, bi, h], vw, preferred_element_type=F32))
    for bi, h in heads:
        cn_s[bi * H + h] = cn_cur[bi, h]
    w = {(c, bi, h): jnp.exp(jnp.where(causal, col(c, bi, h, 0) + a_rows[c, bi][h:h + 1, :], -jnp.inf))
         * qk[c, bi, h] for c, bi, h in inst}
    wv = {k_: jnp.dot(w[k_].astype(BF16), jnp.concatenate([vs[k_], ones_cols], axis=1),
                      preferred_element_type=F32) for k_ in inst}
    hh = {}
    for c, bi, h in inst:
        tot = col(c, bi, h, 1) * qcn[c, bi, h] + wv[c, bi, h]
        den = tot[:, d:2 * d]
        hh[c, bi, h] = tot[:, 0:d] * (1.0 / jnp.maximum(jnp.abs(den), col(c, bi, h, 2)))
    xc = {k_: hh[k_] - _lane_mean(hh[k_], mean_w) for k_ in inst}
    var = {k_: _lane_mean(xc[k_] * xc[k_], mean_w) for k_ in inst}
    for c, bi, h in inst:
        hn = xc[c, bi, h] * lax.rsqrt(var[c, bi, h] + LN_EPS) * ng_ref[:, hsl(h)]
        mb_ref[bi, rsl(c), hsl(h)] = (og_ref[bi, rsl(c), hsl(h)] * hn).astype(BF16)

    @pl.when(i == last)
    def _():
        for bi in range(bb):
            for h in range(H):
                cn = cn_s[bi * H + h]
                c_out_ref[bi, h] = cn[:, 0:d].T
                n_out_ref[bi, h:h + 1, :] = cn[:, d:2 * d].T[0:1, :]
            m_out_ref[bi] = m_s[bi, 0:H, 0:1]


def _mlstm(qvm, kt, gates, og, norm_g, c0, n0, m0, *, lc, bb, nc):
    B, T, _ = qvm.shape
    ts = nc * lc
    nt = T // ts
    W = MLSTM_WIDTH
    H, d = MLSTM_HEADS, MLSTM_HEAD_DIM
    kern = functools.partial(_mlstm_kernel, lc=lc, bb=bb, nc=nc)
    return pl.pallas_call(
        kern,
        grid=(B // bb, nt),
        in_specs=[
            pl.BlockSpec((bb, ts, W), lambda b, i: (b, i, 0)),
            pl.BlockSpec((bb, W, ts), lambda b, i: (b, 0, i)),
            pl.BlockSpec((bb, ts, W), lambda b, i: (b, i, 1)),
            pl.BlockSpec((nc, bb, GATE_ROWS, lc), lambda b, i: (i, b, 0, 0)),
            pl.BlockSpec((bb, ts, W), lambda b, i: (b, i, 0)),
            pl.BlockSpec((1, W), lambda b, i: (0, 0)),
            pl.BlockSpec((bb, H, d, d), lambda b, i: (b, 0, 0, 0)),
            pl.BlockSpec((bb, H, d), lambda b, i: (b, 0, 0)),
            pl.BlockSpec((bb, H, 1), lambda b, i: (b, 0, 0)),
        ],
        out_specs=[
            pl.BlockSpec((bb, ts, W), lambda b, i: (b, i, 0)),
            pl.BlockSpec((bb, H, d, d), lambda b, i: (b, 0, 0, 0)),
            pl.BlockSpec((bb, H, d), lambda b, i: (b, 0, 0)),
            pl.BlockSpec((bb, H, 1), lambda b, i: (b, 0, 0)),
        ],
        out_shape=[
            jax.ShapeDtypeStruct((B, T, W), BF16),
            jax.ShapeDtypeStruct((B, H, d, d), F32),
            jax.ShapeDtypeStruct((B, H, d), F32),
            jax.ShapeDtypeStruct((B, H, 1), F32),
        ],
        scratch_shapes=[
            pltpu.VMEM((bb * H, d, 2 * d), F32),
            pltpu.VMEM((bb, 8, LANES), F32),
        ],
        compiler_params=_cparams(("parallel", "arbitrary")),
        name="mlstm",
    )(qvm, kt, qvm, gates, og, norm_g, c0, n0, m0)


def _layer_norm(z, g, b):
    mu = jnp.mean(z, axis=-1, keepdims=True)
    zc = z - mu
    var = jnp.mean(zc * zc, axis=-1, keepdims=True)
    return zc * lax.rsqrt(var + LN_EPS) * g + b


def _merge_ffn_kernel(x_ref, att_ref, mb_ref, wo_ref, g1_ref, b1_ref, w1_ref, bf1_ref, w2_ref, bf2_ref,
                      g2_ref, b2_ref, y_ref, *, fc, parts):
    W = ATT_WIDTH
    tm = x_ref.shape[1]
    rp = tm // parts
    groups = [slice(p * rp, (p + 1) * rp) for p in range(parts)]
    mixes = [jnp.dot(att_ref[0, rs, :], wo_ref[0:W, :], preferred_element_type=F32)
             + jnp.dot(mb_ref[0, rs, :], wo_ref[W:2 * W, :], preferred_element_type=F32) for rs in groups]
    for rs, mix in zip(groups, mixes):
        h = _layer_norm(DEEPNORM_ALPHA * x_ref[0, rs, :] + mix, g1_ref[...], b1_ref[...])
        hb = h.astype(BF16)
        f = jnp.zeros(h.shape, F32)
        for j in range(D_FF // fc):
            cs = slice(j * fc, (j + 1) * fc)
            a = jnp.dot(hb, w1_ref[:, cs], preferred_element_type=F32) + bf1_ref[:, cs]
            a = jnp.square(jnp.maximum(a, 0.0)).astype(BF16)
            f = f + jnp.dot(a, w2_ref[cs, :], preferred_element_type=F32)
        f = f + bf2_ref[...]
        y_ref[0, rs, :] = _layer_norm(DEEPNORM_ALPHA * h + f, g2_ref[...], b2_ref[...])


def _merge_ffn(x, att, mb, wo, g1, b1, w1, bf1, w2, bf2, g2, b2, *, tm, fc, parts):
    B, T, D = x.shape
    nt = T // tm
    W = ATT_WIDTH
    const = lambda shape: pl.BlockSpec(shape, lambda b, i: (0, 0), pipeline_mode=pl.Buffered(1))
    kern = functools.partial(_merge_ffn_kernel, fc=fc, parts=parts)
    return pl.pallas_call(
        kern,
        grid=(B, nt),
        in_specs=[
            pl.BlockSpec((1, tm, D), lambda b, i: (b, i, 0)),
            pl.BlockSpec((1, tm, W), lambda b, i: (b, i, 0)),
            pl.BlockSpec((1, tm, W), lambda b, i: (b, i, 0)),
            const((D, D)), const((1, D)), const((1, D)),
            const((D, D_FF)), const((1, D_FF)), const((D_FF, D)), const((1, D)),
            const((1, D)), const((1, D)),
        ],
        out_specs=pl.BlockSpec((1, tm, D), lambda b, i: (b, i, 0)),
        out_shape=jax.ShapeDtypeStruct((B, T, D), F32),
        compiler_params=_cparams(("parallel", "parallel")),
        name="merge_ffn",
    )(x, att, mb, wo, g1, b1, w1, bf1, w2, bf2, g2, b2)


def _token_tile(t):
    return 512 if t % 512 == 0 else t


def kernel(x_prompt, x_sample, cache_k, cache_v, state_C, state_n, state_m, w_in, b_in, rel_bias,
           mlstm_norm_g, w_out, ln1_g, ln1_b, w_ff1, b_ff1, w_ff2, b_ff2, ln2_g, ln2_b):
    assert w_in.shape[0] == DEPTH == 1
    B, S, D = x_prompt.shape
    DB, T, _ = x_sample.shape
    H, d = MLSTM_HEADS, MLSTM_HEAD_DIM
    W = MLSTM_WIDTH
    assert S % ATT_REACH == 0 and T == CHUNK and cache_k.shape[2] == ATT_REACH

    n_gate = 2 * H
    wi, bi_ = w_in[0], b_in[0]
    gate_pad = GATE_ROWS_PAD - n_gate
    w_all, wkt = _proj_weights(wi.T)
    b_all = bi_[None, :]
    bkt = jnp.pad(jnp.concatenate([bi_[OFF_KB:OFF_VB], bi_[OFF_GATES:]]), (0, gate_pad))[:, None]
    wo = w_out[0].astype(BF16)
    w1 = w_ff1[0].astype(BF16)
    w2 = w_ff2[0].astype(BF16)
    row = lambda p: p[0][None, :]
    ffn_params = (wo, row(ln1_g), row(ln1_b), w1, row(b_ff1), w2, row(b_ff2), row(ln2_g), row(ln2_b))
    norm_g = row(mlstm_norm_g)
    cps = ATTN_CHUNKS_PER_WINDOW
    keys = -(-(ATT_REACH + cps * CHUNK) // LANES) * LANES
    bias_tab = _bias_table(rel_bias[0], cps=cps, keys=keys)

    def layer(x, k_prev, v_prev, prev_map, c0, n0, m0, *, seqs, mask_first):
        Bx, Tx, _ = x.shape
        n_tok = Bx * Tx
        t_seq = n_tok // seqs
        tm = _token_tile(Tx)
        tp = IN_PROJ_TILE if Tx % IN_PROJ_TILE == 0 else tm
        keep = min(ATT_REACH, Tx)
        lc = min(MLSTM_CHUNK, t_seq)
        qkva, qvm, kt, og, gates, k_last, v_last = _in_proj(x, w_all, b_all, wkt, bkt, tm=tp, keep=keep, lc=lc)
        seq = lambda a: a.reshape(seqs, t_seq, a.shape[-1])
        tq = ATTN_TILE if t_seq % ATTN_TILE == 0 else _token_tile(t_seq)
        past_feature_major = k_prev is not None
        if k_prev is None:
            k_prev = v_prev = seq(qkva)
            step = tq // ATT_REACH
            prev_map = (lambda b, i: (b, jnp.maximum(step * i - 1, 0), 1),
                        lambda b, i: (b, jnp.maximum(step * i - 1, 0), 2))
        att = _band_attn(seq(qkva), k_prev, v_prev, prev_map, bias_tab,
                         tq=tq, cps=cps, keys=keys, mask_first=mask_first,
                         past_feature_major=past_feature_major)
        gates = gates.reshape(t_seq // lc, seqs, GATE_ROWS, lc)
        kt = kt.reshape(Bx, W, seqs // Bx, t_seq).transpose(0, 2, 1, 3).reshape(seqs, W, t_seq)
        mb, c_new, n_new, m_new = _mlstm(seq(qvm), kt, gates, seq(og), norm_g, c0, n0, m0,
                                         lc=lc, bb=min(seqs, MLSTM_STREAMS),
                                         nc=min(MLSTM_CHUNKS_PER_STEP, t_seq // lc))
        tok = lambda a: a.reshape(Bx, Tx, a.shape[-1])
        tf = FFN_TILE if Tx % FFN_TILE == 0 else tm
        y = _merge_ffn(x, tok(att), tok(mb), *ffn_params, tm=tf, fc=FFN_CHUNK, parts=max(tf // FFN_GROUP, 1))
        return y, k_last, v_last, c_new, n_new, m_new

    zeros_c = jnp.zeros((B, H, d, d), F32)
    zeros_n = jnp.zeros((B, H, d), F32)
    zeros_m = jnp.zeros((B, H, 1), F32)
    yp, kp, vp, cp, np_, mp = layer(x_prompt, None, None, None, zeros_c, zeros_n, zeros_m,
                                    seqs=B, mask_first=True)

    ck = cache_k[0].transpose(0, 2, 3, 1).reshape(DB, ATT_WIDTH, ATT_REACH)
    cv = cache_v[0].transpose(0, 2, 3, 1).reshape(DB, ATT_WIDTH, ATT_REACH)
    prev_sample = (lambda b, i: (b, 0, 0), lambda b, i: (b, 0, 0))
    ys, ks, vs, cs, ns, ms = layer(
        x_sample.reshape(1, DB * T, D), ck, cv, prev_sample,
        state_C[0].astype(F32), state_n[0].astype(F32), state_m[0].astype(F32).reshape(DB, H, 1),
        seqs=DB, mask_first=False)

    sd = state_C.dtype
    def heads(a, nb, t):
        bx = a.shape[0]
        a = a.reshape(bx, ATT_HEADS, ATT_HEAD_DIM, nb // bx, t)
        return a.transpose(0, 3, 4, 1, 2).reshape(nb, t, ATT_HEADS, ATT_HEAD_DIM)[None]
    keep_p = min(ATT_REACH, S)
    return (yp, ys.reshape(DB, T, D),
            heads(kp, B, keep_p).astype(cache_k.dtype), heads(vp, B, keep_p).astype(cache_v.dtype),
            cp[None].astype(sd), np_[None].astype(sd), mp.reshape(1, B, H).astype(sd),
            heads(ks, DB, T).astype(cache_k.dtype), heads(vs, DB, T).astype(cache_v.dtype),
            cs[None].astype(sd), ns[None].astype(sd), ms.reshape(1, DB, H).astype(sd))
```

```python
import functools

import jax
import jax.numpy as jnp
from jax import lax
from jax.experimental import pallas as pl
from jax.experimental.pallas import tpu as pltpu

F32 = jnp.float32
BF16 = jnp.bfloat16

D_MODEL = 1024
CHUNK = 64
LEFT_CHUNKS = 8
ATT_REACH = LEFT_CHUNKS * CHUNK
ATT_WIDTH = 512
MLSTM_WIDTH = 512
ATT_HEADS = 8
ATT_HEAD_DIM = 64
MLSTM_HEADS = 4
MLSTM_HEAD_DIM = 128
MAX_REL = 128
D_FF = 4 * D_MODEL
DEPTH = 1
DEEPNORM_ALPHA = (2 * DEPTH) ** 0.25
LN_EPS = 1e-5
NEG_INF = -1e30
ATT_SCALE = ATT_HEAD_DIM ** -0.5
KB_SCALE = MLSTM_HEAD_DIM ** -0.5

OFF_QB = 3 * ATT_WIDTH
OFF_KB = OFF_QB + MLSTM_WIDTH
OFF_VB = OFF_KB + MLSTM_WIDTH
OFF_OB = OFF_VB + MLSTM_WIDTH
OFF_GATES = OFF_OB + MLSTM_WIDTH

LANES = 128
GATE_ROWS_PAD = 16
MLSTM_CHUNK = 256
MLSTM_CHUNKS_PER_STEP = 4
MLSTM_STREAMS = 4
PROJ_WEIGHT_STEPS = 2
GATE_ROWS = 24
IN_PROJ_TILE = 1024
ATTN_TILE = 1024
ATTN_CHUNKS_PER_WINDOW = 1
FFN_TILE = 1024
FFN_GROUP = 256
FFN_CHUNK = 2048
HEAD_PAIRS = ATT_HEADS // 2
VMEM_LIMIT = 56 * 1024 * 1024
NT_DIMS = (((1,), (1,)), ((), ()))


def _cparams(sem):
    return pltpu.CompilerParams(dimension_semantics=sem, vmem_limit_bytes=VMEM_LIMIT)


def _in_proj_kernel(x_ref, w_ref, b_ref, wkt_ref, bkt_ref,
                    qkva_ref, qvm_ref, kt_ref, og_ref, gates_ref, klast_ref, vlast_ref, *, tm, keep, lc):
    i = pl.program_id(1)
    last = pl.num_programs(1) - 1
    xb = x_ref[0].astype(BF16)

    def proj(c0, width):
        return (jnp.dot(xb, w_ref[:, c0:c0 + width], preferred_element_type=F32)
                + b_ref[:, c0:c0 + width])

    W = ATT_WIDTH
    H = MLSTM_HEADS
    ktg = lax.dot_general(wkt_ref[...], xb, NT_DIMS, preferred_element_type=F32) + bkt_ref[...]
    kt_ref[0] = (ktg[0:W] * KB_SCALE).astype(BF16)
    g = ktg[W:W + 2 * H]
    log_sig = jnp.minimum(g, 0.0) - jnp.log1p(jnp.exp(-jnp.abs(g)))
    head_row = lax.broadcasted_iota(jnp.int32, g.shape, 0) < H
    g8 = jnp.where(head_row, g, log_sig)
    pos = lax.broadcasted_iota(jnp.int32, g8.shape, 1) % lc
    csum = g8
    sh = 1
    while sh < lc:
        csum = csum + jnp.where(pos >= sh, pltpu.roll(csum, sh, 1), 0.0)
        sh *= 2
    b = pltpu.roll(csum, H, 0)
    a = g8 - b
    cm = a
    sh = 1
    while sh < lc:
        cm = jnp.maximum(cm, jnp.where(pos >= sh, pltpu.roll(cm, sh, 1), -jnp.inf))
        sh *= 2
    rows = [jnp.where(head_row, t, 0.0) for t in (b, a, cm)]
    for c in range(tm // lc):
        for j, t in enumerate(rows):
            gates_ref[c, 0, 8 * j:8 * j + 8, :] = t[:, c * lc:(c + 1) * lc]

    qvm_ref[0, :, 0:W] = proj(OFF_QB, W).astype(BF16)
    qvm_ref[0, :, W:2 * W] = proj(OFF_VB, W).astype(BF16)
    ob = proj(OFF_OB, W)
    og_ref[0] = 1.0 / (1.0 + jnp.exp(-ob))
    qa = proj(0, W)
    qkva_ref[0, :, 0:W] = (qa * ATT_SCALE).astype(BF16)
    ka = proj(W, W)
    qkva_ref[0, :, W:2 * W] = ka.astype(BF16)
    va = proj(2 * W, W)
    qkva_ref[0, :, 2 * W:3 * W] = va.astype(BF16)

    @pl.when(i == last)
    def _():
        klast_ref[0] = ka[tm - keep:, :].T
        vlast_ref[0] = va[tm - keep:, :].T


def _in_proj(x, w, b, wkt, bkt, *, tm, keep, lc):
    B, T, D = x.shape
    nt = T // tm
    W = ATT_WIDTH
    kern = functools.partial(_in_proj_kernel, tm=tm, keep=keep, lc=lc)
    return pl.pallas_call(
        kern,
        grid=(B, nt),
        in_specs=[
            pl.BlockSpec((1, tm, D), lambda b, i: (b, i, 0)),
            pl.BlockSpec(w.shape, lambda b, i: (0, 0), pipeline_mode=pl.Buffered(1)),
            pl.BlockSpec(b.shape, lambda b, i: (0, 0), pipeline_mode=pl.Buffered(1)),
            pl.BlockSpec(wkt.shape, lambda b, i: (0, 0), pipeline_mode=pl.Buffered(1)),
            pl.BlockSpec(bkt.shape, lambda b, i: (0, 0), pipeline_mode=pl.Buffered(1)),
        ],
        out_specs=[
            pl.BlockSpec((1, tm, 3 * W), lambda b, i: (b, i, 0)),
            pl.BlockSpec((1, tm, 2 * W), lambda b, i: (b, i, 0)),
            pl.BlockSpec((1, W, tm), lambda b, i: (b, 0, i)),
            pl.BlockSpec((1, tm, W), lambda b, i: (b, i, 0)),
            pl.BlockSpec((tm // lc, 1, GATE_ROWS, lc), lambda b, i: (i, b, 0, 0)),
            pl.BlockSpec((1, W, keep), lambda b, i: (b, 0, 0)),
            pl.BlockSpec((1, W, keep), lambda b, i: (b, 0, 0)),
        ],
        out_shape=[
            jax.ShapeDtypeStruct((B, T, 3 * W), BF16),
            jax.ShapeDtypeStruct((B, T, 2 * W), BF16),
            jax.ShapeDtypeStruct((B, W, T), BF16),
            jax.ShapeDtypeStruct((B, T, W), F32),
            jax.ShapeDtypeStruct((T // lc, B, GATE_ROWS, lc), F32),
            jax.ShapeDtypeStruct((B, W, keep), F32),
            jax.ShapeDtypeStruct((B, W, keep), F32),
        ],
        compiler_params=_cparams(("parallel", "arbitrary")),
        name="in_proj",
    )(x, w, b, wkt, bkt)


def _proj_weights_kernel(wt_ref, wg_ref, w_ref, wkt_ref):
    j = pl.program_id(0)
    rows = wt_ref.shape[0]
    W = MLSTM_WIDTH
    w_ref[...] = wt_ref[...].T.astype(BF16)

    @pl.when(j == OFF_KB // rows)
    def _():
        k0 = OFF_KB % rows
        wkt_ref[0:W, :] = wt_ref[k0:k0 + W, :].astype(BF16)

    @pl.when(j == 0)
    def _():
        pad = jnp.zeros((GATE_ROWS_PAD - wg_ref.shape[0], wg_ref.shape[1]), F32)
        wkt_ref[W:W + GATE_ROWS_PAD, :] = jnp.concatenate([wg_ref[...], pad], axis=0).astype(BF16)


def _proj_weights(wt):
    n_in, D = wt.shape
    W = MLSTM_WIDTH
    n_gate = 2 * MLSTM_HEADS
    rows = OFF_GATES // PROJ_WEIGHT_STEPS
    assert OFF_KB // rows == (OFF_VB - 1) // rows
    return pl.pallas_call(
        _proj_weights_kernel,
        grid=(PROJ_WEIGHT_STEPS,),
        in_specs=[pl.BlockSpec((rows, D), lambda j: (j, 0)),
                  pl.BlockSpec((n_gate, D), lambda j: (OFF_GATES // n_gate, 0))],
        out_specs=[pl.BlockSpec((D, rows), lambda j: (0, j)),
                   pl.BlockSpec((W + GATE_ROWS_PAD, D), lambda j: (0, 0))],
        out_shape=[jax.ShapeDtypeStruct((D, OFF_GATES), BF16),
                   jax.ShapeDtypeStruct((W + GATE_ROWS_PAD, D), BF16)],
        compiler_params=_cparams(("arbitrary",)),
        name="proj_weights",
    )(wt, wt)


def _bias_table_kernel(g_ref, o_ref, *, rows, keys):
    L = g_ref.shape[-1]
    r = lax.broadcasted_iota(jnp.int32, (rows, keys), 0)
    c = lax.broadcasted_iota(jnp.int32, (rows, keys), 1)
    band_start = (r // CHUNK) * CHUNK
    in_band = (c >= band_start) & (c < band_start + ATT_REACH + CHUNK)
    for h in range(ATT_HEADS):
        g = jnp.broadcast_to(g_ref[h], (rows, L))
        t = pltpu.roll(g, L - rows, 1, stride=1, stride_axis=0)
        tab = jnp.where(in_band, t[:, :keys], NEG_INF)
        o_ref[h // 2, (h % 2) * rows:(h % 2 + 1) * rows, :] = tab


def _bias_table(rel_bias, *, cps, keys):
    rows = cps * CHUNK
    L = ((rows + keys + LANES - 1) // LANES) * LANES
    n_const = rows + ATT_REACH - MAX_REL
    const = jnp.broadcast_to(rel_bias[:, 2 * MAX_REL:], (ATT_HEADS, n_const))
    ramp = rel_bias[:, ::-1]
    tail = jnp.broadcast_to(rel_bias[:, :1], (ATT_HEADS, LANES))
    g = jnp.concatenate([const, ramp, tail], axis=1)[:, :L].reshape(ATT_HEADS, 1, L)
    kern = functools.partial(_bias_table_kernel, rows=rows, keys=keys)
    return pl.pallas_call(
        kern,
        out_shape=jax.ShapeDtypeStruct((HEAD_PAIRS, 2 * rows, keys), F32),
        name="bias_table",
    )(g)


def _attn_kernel(q_ref, kc_ref, vc_ref, kp_ref, vp_ref, bias_ref, o_ref, kk_ref, vv_ref,
                 *, tq, cps, keys, mask_first, past_feature_major):
    i = pl.program_id(1)
    rows = cps * CHUNK
    if past_feature_major:
        kk_ref[0:ATT_REACH, :] = kp_ref[0].T.astype(BF16)
        vv_ref[0:ATT_REACH, :] = vp_ref[0].T.astype(BF16)
    else:
        kk_ref[0:ATT_REACH, :] = kp_ref[0].astype(BF16)
        vv_ref[0:ATT_REACH, :] = vp_ref[0].astype(BF16)
    kk_ref[ATT_REACH:ATT_REACH + tq, :] = kc_ref[0]
    vv_ref[ATT_REACH:ATT_REACH + tq, :] = vc_ref[0]
    total = kk_ref.shape[0]
    if total > ATT_REACH + tq:
        kk_ref[ATT_REACH + tq:, :] = jnp.zeros((total - ATT_REACH - tq, ATT_WIDTH), BF16)
        vv_ref[ATT_REACH + tq:, :] = jnp.zeros((total - ATT_REACH - tq, ATT_WIDTH), BF16)

    lane = lax.broadcasted_iota(jnp.int32, (rows, LANES), 1)
    first_head = lane < ATT_HEAD_DIM

    def tile(no_past):
        for sub in range(tq // rows):
            off = sub * rows
            first_valid = max(ATT_REACH - off, 0) if no_past else 0
            k0 = first_valid // LANES * LANES
            ones_cols = jnp.ones((keys - k0, LANES), BF16)
            for pair in range(HEAD_PAIRS):
                ls = slice(pair * LANES, (pair + 1) * LANES)
                q2 = q_ref[0, off:off + rows, ls]
                zero = jnp.zeros_like(q2)
                qs = jnp.concatenate([jnp.where(first_head, q2, zero), jnp.where(first_head, zero, q2)], axis=0)
                s = lax.dot_general(qs, kk_ref[off + k0:off + keys, ls], NT_DIMS,
                                    preferred_element_type=F32)
                s = s + bias_ref[pair, :, k0:keys]
                if first_valid > k0:
                    col = lax.broadcasted_iota(jnp.int32, s.shape, 1)
                    s = jnp.where(col >= first_valid - k0, s, NEG_INF)
                m = jnp.max(s, axis=-1, keepdims=True)
                e = jnp.exp(s - m).astype(BF16)
                v_ext = jnp.concatenate([vv_ref[off + k0:off + keys, ls], ones_cols], axis=1)
                o2 = jnp.dot(e, v_ext, preferred_element_type=F32)
                o2 = o2[:, 0:LANES] * (1.0 / o2[:, LANES:2 * LANES])
                o = jnp.where(first_head, o2[:rows], o2[rows:])
                o_ref[0, off:off + rows, ls] = o.astype(BF16)

    if mask_first:
        pl.when(i == 0)(lambda: tile(True))
        pl.when(i > 0)(lambda: tile(False))
    else:
        tile(False)


def _band_attn(qkva, k_prev, v_prev, prev_map, bias_tab, *, tq, cps, keys, mask_first, past_feature_major):
    B, T, _ = qkva.shape
    nt = T // tq
    rows_total = max(ATT_REACH + tq, (tq // (cps * CHUNK) - 1) * cps * CHUNK + keys)
    kern = functools.partial(_attn_kernel, tq=tq, cps=cps, keys=keys, mask_first=mask_first,
                             past_feature_major=past_feature_major)
    W = ATT_WIDTH
    prev_block = (1, ATT_REACH, W)
    return pl.pallas_call(
        kern,
        grid=(B, nt),
        in_specs=[
            pl.BlockSpec((1, tq, W), lambda b, i: (b, i, 0)),
            pl.BlockSpec((1, tq, W), lambda b, i: (b, i, 1)),
            pl.BlockSpec((1, tq, W), lambda b, i: (b, i, 2)),
            pl.BlockSpec(prev_block, prev_map[0]),
            pl.BlockSpec(prev_block, prev_map[1]),
            pl.BlockSpec(bias_tab.shape, lambda b, i: (0, 0, 0)),
        ],
        out_specs=pl.BlockSpec((1, tq, W), lambda b, i: (b, i, 0)),
        out_shape=jax.ShapeDtypeStruct((B, T, W), BF16),
        scratch_shapes=[pltpu.VMEM((rows_total, W), BF16), pltpu.VMEM((rows_total, W), BF16)],
        compiler_params=_cparams(("parallel", "arbitrary")),
        name="band_attn",
    )(qkva, qkva, qkva, k_prev, v_prev, bias_tab)


def _lane_mean(x, mean_w):
    hi = x.astype(BF16)
    lo = (x - hi.astype(F32)).astype(BF16)
    return jnp.dot(jnp.concatenate([hi, lo], axis=1), mean_w, preferred_element_type=F32)


def _mlstm_kernel(q_ref, kt_ref, v_ref, gates_ref, og_ref, ng_ref, c0_ref, n0_ref, m0_ref,
                  mb_ref, c_out_ref, n_out_ref, m_out_ref, cn_s, m_s, *, lc, bb, nc):
    i = pl.program_id(1)
    last = pl.num_programs(1) - 1
    H, d = MLSTM_HEADS, MLSTM_HEAD_DIM

    @pl.when(i == 0)
    def _():
        m_s[...] = jnp.zeros(m_s.shape, F32)
        for bi in range(bb):
            for h in range(H):
                cn_s[bi * H + h, :, 0:d] = c0_ref[bi, h].T
                cn_s[bi * H + h, :, d:2 * d] = jnp.broadcast_to(n0_ref[bi, h:h + 1, :], (d, d)).T
            m_s[bi, 0:H, :] = jnp.broadcast_to(m0_ref[bi], (H, LANES))

    ti = lax.broadcasted_iota(jnp.int32, (lc, lc), 0)
    si = lax.broadcasted_iota(jnp.int32, (lc, lc), 1)
    causal = ti >= si
    ones_cols = jnp.ones((lc, d), BF16)
    mean_w = jnp.full((2 * d, d), 1.0 / d, BF16)

    a_rows, decays, col_sets = {}, {}, {}
    m_cur = [m_s[bi, :, 0:1] for bi in range(bb)]
    for c in range(nc):
        for bi in range(bb):
            b = gates_ref[c, bi, 0:8, :]
            a = gates_ref[c, bi, 8:16, :]
            cm = gates_ref[c, bi, 16:24, :]
            m_prev = m_cur[bi]
            inter = b + m_prev
            m_t = jnp.maximum(inter, b + cm)
            b_last = b[:, lc - 1:lc]
            m_new = jnp.maximum(b_last + m_prev, b_last + cm[:, lc - 1:lc])
            w_end = jnp.exp(b_last + a - m_new)
            per_frame = [b - m_t, jnp.exp(inter - m_t), jnp.exp(-m_t), w_end]
            stacked = jnp.concatenate(per_frame, axis=0)
            col_sets[c, bi] = stacked.T
            a_rows[c, bi] = a
            decays[c, bi] = jnp.exp(b_last + m_prev - m_new)
            m_cur[bi] = m_new
    for bi in range(bb):
        m_s[bi] = jnp.broadcast_to(m_cur[bi], (8, LANES))

    heads = [(bi, h) for bi in range(bb) for h in range(H)]
    inst = [(c, bi, h) for c in range(nc) for bi, h in heads]
    hsl = lambda h: slice(h * d, (h + 1) * d)
    rsl = lambda c: slice(c * lc, (c + 1) * lc)
    col = lambda c, bi, h, j: col_sets[c, bi][:, 8 * j + h:8 * j + h + 1]
    qs = {(c, bi, h): q_ref[bi, rsl(c), hsl(h)] for c, bi, h in inst}
    kts = {(c, bi, h): kt_ref[bi, hsl(h), rsl(c)] for c, bi, h in inst}
    vs = {(c, bi, h): v_ref[bi, rsl(c), hsl(h)] for c, bi, h in inst}

    qk = {k_: jnp.dot(qs[k_], kts[k_], preferred_element_type=F32) for k_ in inst}
    cn_cur = {(bi, h): cn_s[bi * H + h] for bi, h in heads}
    qcn = {}
    for c in range(nc):
        for bi, h in heads:
            qcn[c, bi, h] = jnp.dot(qs[c, bi, h], cn_cur[bi, h].astype(BF16), preferred_element_type=F32)
        for bi, h in heads:
            w_end_col = jnp.broadcast_to(col(c, bi, h, 3), (lc, d))
            vw = jnp.concatenate(
                [(vs[c, bi, h].astype(F32) * w_end_col).astype(BF16), w_end_col.astype(BF16)], axis=1)
            cn_cur[bi, h] = (decays[c, bi][h:h + 1, :] * cn_cur[bi, h]
                             + jnp.dot(kts[c, bi, h], vw, preferred_element_type=F32))
    for bi, h in heads:
        cn_s[bi * H + h] = cn_cur[bi, h]
    w = {(c, bi, h): jnp.exp(jnp.where(causal, col(c, bi, h, 0) + a_rows[c, bi][h:h + 1, :], -jnp.inf))
         * qk[c, bi, h] for c, bi, h in inst}
    wv = {k_: jnp.dot(w[k_].astype(BF16), jnp.concatenate([vs[k_], ones_cols], axis=1),
                      preferred_element_type=F32) for k_ in inst}
    hh = {}
    for c, bi, h in inst:
        tot = col(c, bi, h, 1) * qcn[c, bi, h] + wv[c, bi, h]
        den = tot[:, d:2 * d]
        hh[c, bi, h] = tot[:, 0:d] * (1.0 / jnp.maximum(jnp.abs(den), col(c, bi, h, 2)))
    xc = {k_: hh[k_] - _lane_mean(hh[k_], mean_w) for k_ in inst}
    var = {k_: _lane_mean(xc[k_] * xc[k_], mean_w) for k_ in inst}
    for c, bi, h in inst:
        hn = xc[c, bi, h] * lax.rsqrt(var[c, bi, h] + LN_EPS) * ng_ref[:, hsl(h)]
        mb_ref[bi, rsl(c), hsl(h)] = (og_ref[bi, rsl(c), hsl(h)] * hn).astype(BF16)

    @pl.when(i == last)
    def _():
        for bi in range(bb):
            for h in range(H):
                cn = cn_s[bi * H + h]
                c_out_ref[bi, h] = cn[:, 0:d].T
                n_out_ref[bi, h:h + 1, :] = cn[:, d:2 * d].T[0:1, :]
            m_out_ref[bi] = m_s[bi, 0:H, 0:1]


def _mlstm(qvm, kt, gates, og, norm_g, c0, n0, m0, *, lc, bb, nc):
    B, T, _ = qvm.shape
    ts = nc * lc
    nt = T // ts
    W = MLSTM_WIDTH
    H, d = MLSTM_HEADS, MLSTM_HEAD_DIM
    kern = functools.partial(_mlstm_kernel, lc=lc, bb=bb, nc=nc)
    return pl.pallas_call(
        kern,
        grid=(B // bb, nt),
        in_specs=[
            pl.BlockSpec((bb, ts, W), lambda b, i: (b, i, 0)),
            pl.BlockSpec((bb, W, ts), lambda b, i: (b, 0, i)),
            pl.BlockSpec((bb, ts, W), lambda b, i: (b, i, 1)),
            pl.BlockSpec((nc, bb, GATE_ROWS, lc), lambda b, i: (i, b, 0, 0)),
            pl.BlockSpec((bb, ts, W), lambda b, i: (b, i, 0)),
            pl.BlockSpec((1, W), lambda b, i: (0, 0)),
            pl.BlockSpec((bb, H, d, d), lambda b, i: (b, 0, 0, 0)),
            pl.BlockSpec((bb, H, d), lambda b, i: (b, 0, 0)),
            pl.BlockSpec((bb, H, 1), lambda b, i: (b, 0, 0)),
        ],
        out_specs=[
            pl.BlockSpec((bb, ts, W), lambda b, i: (b, i, 0)),
            pl.BlockSpec((bb, H, d, d), lambda b, i: (b, 0, 0, 0)),
            pl.BlockSpec((bb, H, d), lambda b, i: (b, 0, 0)),
            pl.BlockSpec((bb, H, 1), lambda b, i: (b, 0, 0)),
        ],
        out_shape=[
            jax.ShapeDtypeStruct((B, T, W), BF16),
            jax.ShapeDtypeStruct((B, H, d, d), F32),
            jax.ShapeDtypeStruct((B, H, d), F32),
            jax.ShapeDtypeStruct((B, H, 1), F32),
        ],
        scratch_shapes=[
            pltpu.VMEM((bb * H, d, 2 * d), F32),
            pltpu.VMEM((bb, 8, LANES), F32),
        ],
        compiler_params=_cparams(("parallel", "arbitrary")),
        name="mlstm",
    )(qvm, kt, qvm, gates, og, norm_g, c0, n0, m0)


def _layer_norm(z, g, b):
    mu = jnp.mean(z, axis=-1, keepdims=True)
    zc = z - mu
    var = jnp.mean(zc * zc, axis=-1, keepdims=True)
    return zc * lax.rsqrt(var + LN_EPS) * g + b


def _merge_ffn_kernel(x_ref, att_ref, mb_ref, wo_ref, g1_ref, b1_ref, w1_ref, bf1_ref, w2_ref, bf2_ref,
                      g2_ref, b2_ref, y_ref, *, fc, parts):
    W = ATT_WIDTH
    tm = x_ref.shape[1]
    rp = tm // parts
    groups = [slice(p * rp, (p + 1) * rp) for p in range(parts)]
    mixes = [jnp.dot(att_ref[0, rs, :], wo_ref[0:W, :], preferred_element_type=F32)
             + jnp.dot(mb_ref[0, rs, :], wo_ref[W:2 * W, :], preferred_element_type=F32) for rs in groups]
    for rs, mix in zip(groups, mixes):
        h = _layer_norm(DEEPNORM_ALPHA * x_ref[0, rs, :] + mix, g1_ref[...], b1_ref[...])
        hb = h.astype(BF16)
        f = jnp.zeros(h.shape, F32)
        for j in range(D_FF // fc):
            cs = slice(j * fc, (j + 1) * fc)
            a = jnp.dot(hb, w1_ref[:, cs], preferred_element_type=F32) + bf1_ref[:, cs]
            a = jnp.square(jnp.maximum(a, 0.0)).astype(BF16)
            f = f + jnp.dot(a, w2_ref[cs, :], preferred_element_type=F32)
        f = f + bf2_ref[...]
        y_ref[0, rs, :] = _layer_norm(DEEPNORM_ALPHA * h + f, g2_ref[...], b2_ref[...])


def _merge_ffn(x, att, mb, wo, g1, b1, w1, bf1, w2, bf2, g2, b2, *, tm, fc, parts):
    B, T, D = x.shape
    nt = T // tm
    W = ATT_WIDTH
    const = lambda shape: pl.BlockSpec(shape, lambda b, i: (0, 0), pipeline_mode=pl.Buffered(1))
    kern = functools.partial(_merge_ffn_kernel, fc=fc, parts=parts)
    return pl.pallas_call(
        kern,
        grid=(B, nt),
        in_specs=[
            pl.BlockSpec((1, tm, D), lambda b, i: (b, i, 0)),
            pl.BlockSpec((1, tm, W), lambda b, i: (b, i, 0)),
            pl.BlockSpec((1, tm, W), lambda b, i: (b, i, 0)),
            const((D, D)), const((1, D)), const((1, D)),
            const((D, D_FF)), const((1, D_FF)), const((D_FF, D)), const((1, D)),
            const((1, D)), const((1, D)),
        ],
        out_specs=pl.BlockSpec((1, tm, D), lambda b, i: (b, i, 0)),
        out_shape=jax.ShapeDtypeStruct((B, T, D), F32),
        compiler_params=_cparams(("parallel", "parallel")),
        name="merge_ffn",
    )(x, att, mb, wo, g1, b1, w1, bf1, w2, bf2, g2, b2)


def _token_tile(t):
    return 512 if t % 512 == 0 else t


def kernel(x_prompt, x_sample, cache_k, cache_v, state_C, state_n, state_m, w_in, b_in, rel_bias,
           mlstm_norm_g, w_out, ln1_g, ln1_b, w_ff1, b_ff1, w_ff2, b_ff2, ln2_g, ln2_b):
    assert w_in.shape[0] == DEPTH == 1
    B, S, D = x_prompt.shape
    DB, T, _ = x_sample.shape
    H, d = MLSTM_HEADS, MLSTM_HEAD_DIM
    W = MLSTM_WIDTH
    assert S % ATT_REACH == 0 and T == CHUNK and cache_k.shape[2] == ATT_REACH

    n_gate = 2 * H
    wi, bi_ = w_in[0], b_in[0]
    gate_pad = GATE_ROWS_PAD - n_gate
    w_all, wkt = _proj_weights(wi.T)
    b_all = bi_[None, :]
    bkt = jnp.pad(jnp.concatenate([bi_[OFF_KB:OFF_VB], bi_[OFF_GATES:]]), (0, gate_pad))[:, None]
    wo = w_out[0].astype(BF16)
    w1 = w_ff1[0].astype(BF16)
    w2 = w_ff2[0].astype(BF16)
    row = lambda p: p[0][None, :]
    ffn_params = (wo, row(ln1_g), row(ln1_b), w1, row(b_ff1), w2, row(b_ff2), row(ln2_g), row(ln2_b))
    norm_g = row(mlstm_norm_g)
    cps = ATTN_CHUNKS_PER_WINDOW
    keys = -(-(ATT_REACH + cps * CHUNK) // LANES) * LANES
    bias_tab = _bias_table(rel_bias[0], cps=cps, keys=keys)

    def layer(x, k_prev, v_prev, prev_map, c0, n0, m0, *, seqs, mask_first):
        Bx, Tx, _ = x.shape
        n_tok = Bx * Tx
        t_seq = n_tok // seqs
        tm = _token_tile(Tx)
        tp = IN_PROJ_TILE if Tx % IN_PROJ_TILE == 0 else tm
        keep = min(ATT_REACH, Tx)
        lc = min(MLSTM_CHUNK, t_seq)
        qkva, qvm, kt, og, gates, k_last, v_last = _in_proj(x, w_all, b_all, wkt, bkt, tm=tp, keep=keep, lc=lc)
        seq = lambda a: a.reshape(seqs, t_seq, a.shape[-1])
        tq = ATTN_TILE if t_seq % ATTN_TILE == 0 else _token_tile(t_seq)
        past_feature_major = k_prev is not None
        if k_prev is None:
            k_prev = v_prev = seq(qkva)
            step = tq // ATT_REACH
            prev_map = (lambda b, i: (b, jnp.maximum(step * i - 1, 0), 1),
                        lambda b, i: (b, jnp.maximum(step * i - 1, 0), 2))
        att = _band_attn(seq(qkva), k_prev, v_prev, prev_map, bias_tab,
                         tq=tq, cps=cps, keys=keys, mask_first=mask_first,
                         past_feature_major=past_feature_major)
        gates = gates.reshape(t_seq // lc, seqs, GATE_ROWS, lc)
        kt = kt.reshape(Bx, W, seqs // Bx, t_seq).transpose(0, 2, 1, 3).reshape(seqs, W, t_seq)
        mb, c_new, n_new, m_new = _mlstm(seq(qvm), kt, gates, seq(og), norm_g, c0, n0, m0,
                                         lc=lc, bb=min(seqs, MLSTM_STREAMS),
                                         nc=min(MLSTM_CHUNKS_PER_STEP, t_seq // lc))
        tok = lambda a: a.reshape(Bx, Tx, a.shape[-1])
        tf = FFN_TILE if Tx % FFN_TILE == 0 else tm
        y = _merge_ffn(x, tok(att), tok(mb), *ffn_params, tm=tf, fc=FFN_CHUNK, parts=max(tf // FFN_GROUP, 1))
        return y, k_last, v_last, c_new, n_new, m_new

    zeros_c = jnp.zeros((B, H, d, d), F32)
    zeros_n = jnp.zeros((B, H, d), F32)
    zeros_m = jnp.zeros((B, H, 1), F32)
    yp, kp, vp, cp, np_, mp = layer(x_prompt, None, None, None, zeros_c, zeros_n, zeros_m,
                                    seqs=B, mask_first=True)

    ck = cache_k[0].transpose(0, 2, 3, 1).reshape(DB, ATT_WIDTH, ATT_REACH)
    cv = cache_v[0].transpose(0, 2, 3, 1).reshape(DB, ATT_WIDTH, ATT_REACH)
    prev_sample = (lambda b, i: (b, 0, 0), lambda b, i: (b, 0, 0))
    ys, ks, vs, cs, ns, ms = layer(
        x_sample.reshape(1, DB * T, D), ck, cv, prev_sample,
        state_C[0].astype(F32), state_n[0].astype(F32), state_m[0].astype(F32).reshape(DB, H, 1),
        seqs=DB, mask_first=False)

    sd = state_C.dtype
    def heads(a, nb, t):
        bx = a.shape[0]
        a = a.reshape(bx, ATT_HEADS, ATT_HEAD_DIM, nb // bx, t)
        return a.transpose(0, 3, 4, 1, 2).reshape(nb, t, ATT_HEADS, ATT_HEAD_DIM)[None]
    keep_p = min(ATT_REACH, S)
    return (yp, ys.reshape(DB, T, D),
            heads(kp, B, keep_p).astype(cache_k.dtype), heads(vp, B, keep_p).astype(cache_v.dtype),
            cp[None].astype(sd), np_[None].astype(sd), mp.reshape(1, B, H).astype(sd),
            heads(ks, DB, T).astype(cache_k.dtype), heads(vs, DB, T).astype(cache_v.dtype),
            cs[None].astype(sd), ns[None].astype(sd), ms.reshape(1, DB, H).astype(sd))
```

```python
import functools

import jax
import jax.numpy as jnp
from jax import lax
from jax.experimental import pallas as pl
from jax.experimental.pallas import tpu as pltpu

F32 = jnp.float32
BF16 = jnp.bfloat16

D_MODEL = 1024
CHUNK = 64
LEFT_CHUNKS = 8
ATT_REACH = LEFT_CHUNKS * CHUNK
ATT_WIDTH = 512
MLSTM_WIDTH = 512
ATT_HEADS = 8
ATT_HEAD_DIM = 64
MLSTM_HEADS = 4
MLSTM_HEAD_DIM = 128
MAX_REL = 128
D_FF = 4 * D_MODEL
DEPTH = 1
DEEPNORM_ALPHA = (2 * DEPTH) ** 0.25
LN_EPS = 1e-5
NEG_INF = -1e30
ATT_SCALE = ATT_HEAD_DIM ** -0.5
KB_SCALE = MLSTM_HEAD_DIM ** -0.5

OFF_QB = 3 * ATT_WIDTH
OFF_KB = OFF_QB + MLSTM_WIDTH
OFF_VB = OFF_KB + MLSTM_WIDTH
OFF_OB = OFF_VB + MLSTM_WIDTH
OFF_GATES = OFF_OB + MLSTM_WIDTH

LANES = 128
GATE_ROWS_PAD = 16
MLSTM_CHUNK = 256
MLSTM_CHUNKS_PER_STEP = 4
MLSTM_STREAMS = 2
GATE_ROWS = 24
IN_PROJ_TILE = 1024
ATTN_TILE = 1024
ATTN_CHUNKS_PER_WINDOW = 1
FFN_TILE = 1024
FFN_GROUP = 256
FFN_CHUNK = 2048
HEAD_PAIRS = ATT_HEADS // 2
VMEM_LIMIT = 56 * 1024 * 1024
NT_DIMS = (((1,), (1,)), ((), ()))


def _cparams(sem):
    return pltpu.CompilerParams(dimension_semantics=sem, vmem_limit_bytes=VMEM_LIMIT)


def _in_proj_kernel(x_ref, w_ref, b_ref, wkt_ref, bkt_ref,
                    qkva_ref, qvm_ref, kt_ref, og_ref, gates_ref, klast_ref, vlast_ref, *, tm, keep, lc):
    i = pl.program_id(1)
    last = pl.num_programs(1) - 1
    xb = x_ref[0].astype(BF16)

    def proj(c0, width):
        return (jnp.dot(xb, w_ref[:, c0:c0 + width], preferred_element_type=F32)
                + b_ref[:, c0:c0 + width])

    W = ATT_WIDTH
    H = MLSTM_HEADS
    ktg = lax.dot_general(wkt_ref[...], xb, NT_DIMS, preferred_element_type=F32) + bkt_ref[...]
    kt_ref[0] = (ktg[0:W] * KB_SCALE).astype(BF16)
    g = ktg[W:W + 2 * H]
    log_sig = jnp.minimum(g, 0.0) - jnp.log1p(jnp.exp(-jnp.abs(g)))
    head_row = lax.broadcasted_iota(jnp.int32, g.shape, 0) < H
    g8 = jnp.where(head_row, g, log_sig)
    pos = lax.broadcasted_iota(jnp.int32, g8.shape, 1) % lc
    csum = g8
    sh = 1
    while sh < lc:
        csum = csum + jnp.where(pos >= sh, pltpu.roll(csum, sh, 1), 0.0)
        sh *= 2
    b = pltpu.roll(csum, H, 0)
    a = g8 - b
    cm = a
    sh = 1
    while sh < lc:
        cm = jnp.maximum(cm, jnp.where(pos >= sh, pltpu.roll(cm, sh, 1), -jnp.inf))
        sh *= 2
    rows = [jnp.where(head_row, t, 0.0) for t in (b, a, cm)]
    for c in range(tm // lc):
        for j, t in enumerate(rows):
            gates_ref[c, 0, 8 * j:8 * j + 8, :] = t[:, c * lc:(c + 1) * lc]

    qvm_ref[0, :, 0:W] = proj(OFF_QB, W).astype(BF16)
    qvm_ref[0, :, W:2 * W] = proj(OFF_VB, W).astype(BF16)
    ob = proj(OFF_OB, W)
    og_ref[0] = 1.0 / (1.0 + jnp.exp(-ob))
    qa = proj(0, W)
    qkva_ref[0, :, 0:W] = (qa * ATT_SCALE).astype(BF16)
    ka = proj(W, W)
    qkva_ref[0, :, W:2 * W] = ka.astype(BF16)
    va = proj(2 * W, W)
    qkva_ref[0, :, 2 * W:3 * W] = va.astype(BF16)

    @pl.when(i == last)
    def _():
        klast_ref[0] = ka[tm - keep:, :].T
        vlast_ref[0] = va[tm - keep:, :].T


def _in_proj(x, w, b, wkt, bkt, *, tm, keep, lc):
    B, T, D = x.shape
    nt = T // tm
    W = ATT_WIDTH
    kern = functools.partial(_in_proj_kernel, tm=tm, keep=keep, lc=lc)
    return pl.pallas_call(
        kern,
        grid=(B, nt),
        in_specs=[
            pl.BlockSpec((1, tm, D), lambda b, i: (b, i, 0)),
            pl.BlockSpec(w.shape, lambda b, i: (0, 0), pipeline_mode=pl.Buffered(1)),
            pl.BlockSpec(b.shape, lambda b, i: (0, 0), pipeline_mode=pl.Buffered(1)),
            pl.BlockSpec(wkt.shape, lambda b, i: (0, 0), pipeline_mode=pl.Buffered(1)),
            pl.BlockSpec(bkt.shape, lambda b, i: (0, 0), pipeline_mode=pl.Buffered(1)),
        ],
        out_specs=[
            pl.BlockSpec((1, tm, 3 * W), lambda b, i: (b, i, 0)),
            pl.BlockSpec((1, tm, 2 * W), lambda b, i: (b, i, 0)),
            pl.BlockSpec((1, W, tm), lambda b, i: (b, 0, i)),
            pl.BlockSpec((1, tm, W), lambda b, i: (b, i, 0)),
            pl.BlockSpec((tm // lc, 1, GATE_ROWS, lc), lambda b, i: (i, b, 0, 0)),
            pl.BlockSpec((1, W, keep), lambda b, i: (b, 0, 0)),
            pl.BlockSpec((1, W, keep), lambda b, i: (b, 0, 0)),
        ],
        out_shape=[
            jax.ShapeDtypeStruct((B, T, 3 * W), BF16),
            jax.ShapeDtypeStruct((B, T, 2 * W), BF16),
            jax.ShapeDtypeStruct((B, W, T), BF16),
            jax.ShapeDtypeStruct((B, T, W), F32),
            jax.ShapeDtypeStruct((T // lc, B, GATE_ROWS, lc), F32),
            jax.ShapeDtypeStruct((B, W, keep), F32),
            jax.ShapeDtypeStruct((B, W, keep), F32),
        ],
        compiler_params=_cparams(("parallel", "arbitrary")),
        name="in_proj",
    )(x, w, b, wkt, bkt)


def _proj_weights_kernel(wt_ref, wg_ref, w_ref, wkt_ref):
    j = pl.program_id(0)
    W = wt_ref.shape[0]
    w_ref[...] = wt_ref[...].T.astype(BF16)

    @pl.when(j == OFF_KB // W)
    def _():
        wkt_ref[0:W, :] = wt_ref[...].astype(BF16)

    @pl.when(j == 0)
    def _():
        pad = jnp.zeros((GATE_ROWS_PAD - wg_ref.shape[0], wg_ref.shape[1]), F32)
        wkt_ref[W:W + GATE_ROWS_PAD, :] = jnp.concatenate([wg_ref[...], pad], axis=0).astype(BF16)


def _proj_weights(wt):
    n_in, D = wt.shape
    W = MLSTM_WIDTH
    n_gate = 2 * MLSTM_HEADS
    return pl.pallas_call(
        _proj_weights_kernel,
        grid=(OFF_GATES // W,),
        in_specs=[pl.BlockSpec((W, D), lambda j: (j, 0)),
                  pl.BlockSpec((n_gate, D), lambda j: (OFF_GATES // n_gate, 0))],
        out_specs=[pl.BlockSpec((D, W), lambda j: (0, j)),
                   pl.BlockSpec((W + GATE_ROWS_PAD, D), lambda j: (0, 0))],
        out_shape=[jax.ShapeDtypeStruct((D, OFF_GATES), BF16),
                   jax.ShapeDtypeStruct((W + GATE_ROWS_PAD, D), BF16)],
        compiler_params=_cparams(("arbitrary",)),
        name="proj_weights",
    )(wt, wt)


def _bias_table_kernel(g_ref, o_ref, *, rows, keys):
    L = g_ref.shape[-1]
    r = lax.broadcasted_iota(jnp.int32, (rows, keys), 0)
    c = lax.broadcasted_iota(jnp.int32, (rows, keys), 1)
    band_start = (r // CHUNK) * CHUNK
    in_band = (c >= band_start) & (c < band_start + ATT_REACH + CHUNK)
    for h in range(ATT_HEADS):
        g = jnp.broadcast_to(g_ref[h], (rows, L))
        t = pltpu.roll(g, L - rows, 1, stride=1, stride_axis=0)
        tab = jnp.where(in_band, t[:, :keys], NEG_INF)
        o_ref[h // 2, (h % 2) * rows:(h % 2 + 1) * rows, :] = tab


def _bias_table(rel_bias, *, cps, keys):
    rows = cps * CHUNK
    L = ((rows + keys + LANES - 1) // LANES) * LANES
    n_const = rows + ATT_REACH - MAX_REL
    const = jnp.broadcast_to(rel_bias[:, 2 * MAX_REL:], (ATT_HEADS, n_const))
    ramp = rel_bias[:, ::-1]
    tail = jnp.broadcast_to(rel_bias[:, :1], (ATT_HEADS, LANES))
    g = jnp.concatenate([const, ramp, tail], axis=1)[:, :L].reshape(ATT_HEADS, 1, L)
    kern = functools.partial(_bias_table_kernel, rows=rows, keys=keys)
    return pl.pallas_call(
        kern,
        out_shape=jax.ShapeDtypeStruct((HEAD_PAIRS, 2 * rows, keys), F32),
        name="bias_table",
    )(g)


def _attn_kernel(q_ref, kc_ref, vc_ref, kp_ref, vp_ref, bias_ref, o_ref, kk_ref, vv_ref,
                 *, tq, cps, keys, mask_first, past_feature_major):
    i = pl.program_id(1)
    rows = cps * CHUNK
    if past_feature_major:
        kk_ref[0:ATT_REACH, :] = kp_ref[0].T.astype(BF16)
        vv_ref[0:ATT_REACH, :] = vp_ref[0].T.astype(BF16)
    else:
        kk_ref[0:ATT_REACH, :] = kp_ref[0].astype(BF16)
        vv_ref[0:ATT_REACH, :] = vp_ref[0].astype(BF16)
    kk_ref[ATT_REACH:ATT_REACH + tq, :] = kc_ref[0]
    vv_ref[ATT_REACH:ATT_REACH + tq, :] = vc_ref[0]
    total = kk_ref.shape[0]
    if total > ATT_REACH + tq:
        kk_ref[ATT_REACH + tq:, :] = jnp.zeros((total - ATT_REACH - tq, ATT_WIDTH), BF16)
        vv_ref[ATT_REACH + tq:, :] = jnp.zeros((total - ATT_REACH - tq, ATT_WIDTH), BF16)

    lane = lax.broadcasted_iota(jnp.int32, (rows, LANES), 1)
    first_head = lane < ATT_HEAD_DIM

    def tile(no_past):
        for sub in range(tq // rows):
            off = sub * rows
            first_valid = max(ATT_REACH - off, 0) if no_past else 0
            k0 = first_valid // LANES * LANES
            ones_cols = jnp.ones((keys - k0, LANES), BF16)
            for pair in range(HEAD_PAIRS):
                ls = slice(pair * LANES, (pair + 1) * LANES)
                q2 = q_ref[0, off:off + rows, ls]
                zero = jnp.zeros_like(q2)
                qs = jnp.concatenate([jnp.where(first_head, q2, zero), jnp.where(first_head, zero, q2)], axis=0)
                s = lax.dot_general(qs, kk_ref[off + k0:off + keys, ls], NT_DIMS,
                                    preferred_element_type=F32)
                s = s + bias_ref[pair, :, k0:keys]
                if first_valid > k0:
                    col = lax.broadcasted_iota(jnp.int32, s.shape, 1)
                    s = jnp.where(col >= first_valid - k0, s, NEG_INF)
                m = jnp.max(s, axis=-1, keepdims=True)
                e = jnp.exp(s - m).astype(BF16)
                v_ext = jnp.concatenate([vv_ref[off + k0:off + keys, ls], ones_cols], axis=1)
                o2 = jnp.dot(e, v_ext, preferred_element_type=F32)
                o2 = o2[:, 0:LANES] * (1.0 / o2[:, LANES:2 * LANES])
                o = jnp.where(first_head, o2[:rows], o2[rows:])
                o_ref[0, off:off + rows, ls] = o.astype(BF16)

    if mask_first:
        pl.when(i == 0)(lambda: tile(True))
        pl.when(i > 0)(lambda: tile(False))
    else:
        tile(False)


def _band_attn(qkva, k_prev, v_prev, prev_map, bias_tab, *, tq, cps, keys, mask_first, past_feature_major):
    B, T, _ = qkva.shape
    nt = T // tq
    rows_total = max(ATT_REACH + tq, (tq // (cps * CHUNK) - 1) * cps * CHUNK + keys)
    kern = functools.partial(_attn_kernel, tq=tq, cps=cps, keys=keys, mask_first=mask_first,
                             past_feature_major=past_feature_major)
    W = ATT_WIDTH
    prev_block = (1, ATT_REACH, W)
    return pl.pallas_call(
        kern,
        grid=(B, nt),
        in_specs=[
            pl.BlockSpec((1, tq, W), lambda b, i: (b, i, 0)),
            pl.BlockSpec((1, tq, W), lambda b, i: (b, i, 1)),
            pl.BlockSpec((1, tq, W), lambda b, i: (b, i, 2)),
            pl.BlockSpec(prev_block, prev_map[0]),
            pl.BlockSpec(prev_block, prev_map[1]),
            pl.BlockSpec(bias_tab.shape, lambda b, i: (0, 0, 0)),
        ],
        out_specs=pl.BlockSpec((1, tq, W), lambda b, i: (b, i, 0)),
        out_shape=jax.ShapeDtypeStruct((B, T, W), BF16),
        scratch_shapes=[pltpu.VMEM((rows_total, W), BF16), pltpu.VMEM((rows_total, W), BF16)],
        compiler_params=_cparams(("parallel", "arbitrary")),
        name="band_attn",
    )(qkva, qkva, qkva, k_prev, v_prev, bias_tab)


def _lane_mean(x, mean_w):
    hi = x.astype(BF16)
    lo = (x - hi.astype(F32)).astype(BF16)
    return jnp.dot(jnp.concatenate([hi, lo], axis=1), mean_w, preferred_element_type=F32)


def _mlstm_kernel(q_ref, kt_ref, v_ref, gates_ref, og_ref, ng_ref, c0_ref, n0_ref, m0_ref,
                  mb_ref, c_out_ref, n_out_ref, m_out_ref, cn_s, m_s, *, lc, bb, nc):
    i = pl.program_id(1)
    last = pl.num_programs(1) - 1
    H, d = MLSTM_HEADS, MLSTM_HEAD_DIM

    @pl.when(i == 0)
    def _():
        m_s[...] = jnp.zeros(m_s.shape, F32)
        for bi in range(bb):
            for h in range(H):
                cn_s[bi * H + h, :, 0:d] = c0_ref[bi, h].T
                cn_s[bi * H + h, :, d:2 * d] = jnp.broadcast_to(n0_ref[bi, h:h + 1, :], (d, d)).T
            m_s[bi, 0:H, :] = jnp.broadcast_to(m0_ref[bi], (H, LANES))

    ti = lax.broadcasted_iota(jnp.int32, (lc, lc), 0)
    si = lax.broadcasted_iota(jnp.int32, (lc, lc), 1)
    causal = ti >= si
    ones_cols = jnp.ones((lc, d), BF16)
    mean_w = jnp.full((2 * d, d), 1.0 / d, BF16)

    a_rows, decays, col_sets = {}, {}, {}
    m_cur = [m_s[bi, :, 0:1] for bi in range(bb)]
    for c in range(nc):
        for bi in range(bb):
            b = gates_ref[c, bi, 0:8, :]
            a = gates_ref[c, bi, 8:16, :]
            cm = gates_ref[c, bi, 16:24, :]
            m_prev = m_cur[bi]
            inter = b + m_prev
            m_t = jnp.maximum(inter, b + cm)
            b_last = b[:, lc - 1:lc]
            m_new = jnp.maximum(b_last + m_prev, b_last + cm[:, lc - 1:lc])
            w_end = jnp.exp(b_last + a - m_new)
            per_frame = [b - m_t, jnp.exp(inter - m_t), jnp.exp(-m_t), w_end]
            stacked = jnp.concatenate(per_frame, axis=0)
            col_sets[c, bi] = stacked.T
            a_rows[c, bi] = a
            decays[c, bi] = jnp.exp(b_last + m_prev - m_new)
            m_cur[bi] = m_new
    for bi in range(bb):
        m_s[bi] = jnp.broadcast_to(m_cur[bi], (8, LANES))

    heads = [(bi, h) for bi in range(bb) for h in range(H)]
    inst = [(c, bi, h) for c in range(nc) for bi, h in heads]
    hsl = lambda h: slice(h * d, (h + 1) * d)
    rsl = lambda c: slice(c * lc, (c + 1) * lc)
    col = lambda c, bi, h, j: col_sets[c, bi][:, 8 * j + h:8 * j + h + 1]
    qs = {(c, bi, h): q_ref[bi, rsl(c), hsl(h)] for c, bi, h in inst}
    kts = {(c, bi, h): kt_ref[bi, hsl(h), rsl(c)] for c, bi, h in inst}
    vs = {(c, bi, h): v_ref[bi, rsl(c), hsl(h)] for c, bi, h in inst}

    qk = {k_: jnp.dot(qs[k_], kts[k_], preferred_element_type=F32) for k_ in inst}
    cn_cur = {(bi, h): cn_s[bi * H + h] for bi, h in heads}
    qcn = {}
    for c in range(nc):
        for bi, h in heads:
            qcn[c, bi, h] = jnp.dot(qs[c, bi, h], cn_cur[bi, h].astype(BF16), preferred_element_type=F32)
        for bi, h in heads:
            w_end_col = jnp.broadcast_to(col(c, bi, h, 3), (lc, d))
            vw = jnp.concatenate(
                [(vs[c, bi, h].astype(F32) * w_end_col).astype(BF16), w_end_col.astype(BF16)], axis=1)
            cn_cur[bi, h] = (decays[c, bi][h:h + 1, :] * cn_cur[bi, h]
                             + jnp.dot(kts[c, bi, h], vw, preferred_element_type=F32))
    for bi, h in heads:
        cn_s[bi * H + h] = cn_cur[bi, h]
    w = {(c, bi, h): jnp.exp(jnp.where(causal, col(c, bi, h, 0) + a_rows[c, bi][h:h + 1, :], -jnp.inf))
         * qk[c, bi, h] for c, bi, h in inst}
    wv = {k_: jnp.dot(w[k_].astype(BF16), jnp.concatenate([vs[k_], ones_cols], axis=1),
                      preferred_element_type=F32) for k_ in inst}
    hh = {}
    for c, bi, h in inst:
        tot = col(c, bi, h, 1) * qcn[c, bi, h] + wv[c, bi, h]
        den = tot[:, d:2 * d]
        hh[c, bi, h] = tot[:, 0:d] * (1.0 / jnp.maximum(jnp.abs(den), col(c, bi, h, 2)))
    xc = {k_: hh[k_] - _lane_mean(hh[k_], mean_w) for k_ in inst}
    var = {k_: _lane_mean(xc[k_] * xc[k_], mean_w) for k_ in inst}
    for c, bi, h in inst:
        hn = xc[c, bi, h] * lax.rsqrt(var[c, bi, h] + LN_EPS) * ng_ref[:, hsl(h)]
        mb_ref[bi, rsl(c), hsl(h)] = (og_ref[bi, rsl(c), hsl(h)] * hn).astype(BF16)

    @pl.when(i == last)
    def _():
        for bi in range(bb):
            for h in range(H):
                cn = cn_s[bi * H + h]
                c_out_ref[bi, h] = cn[:, 0:d].T
                n_out_ref[bi, h:h + 1, :] = cn[:, d:2 * d].T[0:1, :]
            m_out_ref[bi] = m_s[bi, 0:H, 0:1]


def _mlstm(qvm, kt, gates, og, norm_g, c0, n0, m0, *, lc, bb, nc):
    B, T, _ = qvm.shape
    ts = nc * lc
    nt = T // ts
    W = MLSTM_WIDTH
    H, d = MLSTM_HEADS, MLSTM_HEAD_DIM
    kern = functools.partial(_mlstm_kernel, lc=lc, bb=bb, nc=nc)
    return pl.pallas_call(
        kern,
        grid=(B // bb, nt),
        in_specs=[
            pl.BlockSpec((bb, ts, W), lambda b, i: (b, i, 0)),
            pl.BlockSpec((bb, W, ts), lambda b, i: (b, 0, i)),
            pl.BlockSpec((bb, ts, W), lambda b, i: (b, i, 1)),
            pl.BlockSpec((nc, bb, GATE_ROWS, lc), lambda b, i: (i, b, 0, 0)),
            pl.BlockSpec((bb, ts, W), lambda b, i: (b, i, 0)),
            pl.BlockSpec((1, W), lambda b, i: (0, 0)),
            pl.BlockSpec((bb, H, d, d), lambda b, i: (b, 0, 0, 0)),
            pl.BlockSpec((bb, H, d), lambda b, i: (b, 0, 0)),
            pl.BlockSpec((bb, H, 1), lambda b, i: (b, 0, 0)),
        ],
        out_specs=[
            pl.BlockSpec((bb, ts, W), lambda b, i: (b, i, 0)),
            pl.BlockSpec((bb, H, d, d), lambda b, i: (b, 0, 0, 0)),
            pl.BlockSpec((bb, H, d), lambda b, i: (b, 0, 0)),
            pl.BlockSpec((bb, H, 1), lambda b, i: (b, 0, 0)),
        ],
        out_shape=[
            jax.ShapeDtypeStruct((B, T, W), BF16),
            jax.ShapeDtypeStruct((B, H, d, d), F32),
            jax.ShapeDtypeStruct((B, H, d), F32),
            jax.ShapeDtypeStruct((B, H, 1), F32),
        ],
        scratch_shapes=[
            pltpu.VMEM((bb * H, d, 2 * d), F32),
            pltpu.VMEM((bb, 8, LANES), F32),
        ],
        compiler_params=_cparams(("parallel", "arbitrary")),
        name="mlstm",
    )(qvm, kt, qvm, gates, og, norm_g, c0, n0, m0)


def _layer_norm(z, g, b):
    mu = jnp.mean(z, axis=-1, keepdims=True)
    zc = z - mu
    var = jnp.mean(zc * zc, axis=-1, keepdims=True)
    return zc * lax.rsqrt(var + LN_EPS) * g + b


def _merge_ffn_kernel(x_ref, att_ref, mb_ref, wo_ref, g1_ref, b1_ref, w1_ref, bf1_ref, w2_ref, bf2_ref,
                      g2_ref, b2_ref, y_ref, *, fc, parts):
    W = ATT_WIDTH
    tm = x_ref.shape[1]
    rp = tm // parts
    groups = [slice(p * rp, (p + 1) * rp) for p in range(parts)]
    mixes = [jnp.dot(att_ref[0, rs, :], wo_ref[0:W, :], preferred_element_type=F32)
             + jnp.dot(mb_ref[0, rs, :], wo_ref[W:2 * W, :], preferred_element_type=F32) for rs in groups]
    for rs, mix in zip(groups, mixes):
        h = _layer_norm(DEEPNORM_ALPHA * x_ref[0, rs, :] + mix, g1_ref[...], b1_ref[...])
        hb = h.astype(BF16)
        f = jnp.zeros(h.shape, F32)
        for j in range(D_FF // fc):
            cs = slice(j * fc, (j + 1) * fc)
            a = jnp.dot(hb, w1_ref[:, cs], preferred_element_type=F32) + bf1_ref[:, cs]
            a = jnp.square(jnp.maximum(a, 0.0)).astype(BF16)
            f = f + jnp.dot(a, w2_ref[cs, :], preferred_element_type=F32)
        f = f + bf2_ref[...]
        y_ref[0, rs, :] = _layer_norm(DEEPNORM_ALPHA * h + f, g2_ref[...], b2_ref[...])


def _merge_ffn(x, att, mb, wo, g1, b1, w1, bf1, w2, bf2, g2, b2, *, tm, fc, parts):
    B, T, D = x.shape
    nt = T // tm
    W = ATT_WIDTH
    const = lambda shape: pl.BlockSpec(shape, lambda b, i: (0, 0), pipeline_mode=pl.Buffered(1))
    kern = functools.partial(_merge_ffn_kernel, fc=fc, parts=parts)
    return pl.pallas_call(
        kern,
        grid=(B, nt),
        in_specs=[
            pl.BlockSpec((1, tm, D), lambda b, i: (b, i, 0)),
            pl.BlockSpec((1, tm, W), lambda b, i: (b, i, 0)),
            pl.BlockSpec((1, tm, W), lambda b, i: (b, i, 0)),
            const((D, D)), const((1, D)), const((1, D)),
            const((D, D_FF)), const((1, D_FF)), const((D_FF, D)), const((1, D)),
            const((1, D)), const((1, D)),
        ],
        out_specs=pl.BlockSpec((1, tm, D), lambda b, i: (b, i, 0)),
        out_shape=jax.ShapeDtypeStruct((B, T, D), F32),
        compiler_params=_cparams(("parallel", "parallel")),
        name="merge_ffn",
    )(x, att, mb, wo, g1, b1, w1, bf1, w2, bf2, g2, b2)


def _token_tile(t):
    return 512 if t % 512 == 0 else t


def kernel(x_prompt, x_sample, cache_k, cache_v, state_C, state_n, state_m, w_in, b_in, rel_bias,
           mlstm_norm_g, w_out, ln1_g, ln1_b, w_ff1, b_ff1, w_ff2, b_ff2, ln2_g, ln2_b):
    assert w_in.shape[0] == DEPTH == 1
    B, S, D = x_prompt.shape
    DB, T, _ = x_sample.shape
    H, d = MLSTM_HEADS, MLSTM_HEAD_DIM
    W = MLSTM_WIDTH
    assert S % ATT_REACH == 0 and T == CHUNK and cache_k.shape[2] == ATT_REACH

    n_gate = 2 * H
    wi, bi_ = w_in[0], b_in[0]
    gate_pad = GATE_ROWS_PAD - n_gate
    w_all, wkt = _proj_weights(wi.T)
    b_all = bi_[None, :]
    bkt = jnp.pad(jnp.concatenate([bi_[OFF_KB:OFF_VB], bi_[OFF_GATES:]]), (0, gate_pad))[:, None]
    wo = w_out[0].astype(BF16)
    w1 = w_ff1[0].astype(BF16)
    w2 = w_ff2[0].astype(BF16)
    row = lambda p: p[0][None, :]
    ffn_params = (wo, row(ln1_g), row(ln1_b), w1, row(b_ff1), w2, row(b_ff2), row(ln2_g), row(ln2_b))
    norm_g = row(mlstm_norm_g)
    cps = ATTN_CHUNKS_PER_WINDOW
    keys = -(-(ATT_REACH + cps * CHUNK) // LANES) * LANES
    bias_tab = _bias_table(rel_bias[0], cps=cps, keys=keys)

    def layer(x, k_prev, v_prev, prev_map, c0, n0, m0, *, seqs, mask_first):
        Bx, Tx, _ = x.shape
        n_tok = Bx * Tx
        t_seq = n_tok // seqs
        tm = _token_tile(Tx)
        tp = IN_PROJ_TILE if Tx % IN_PROJ_TILE == 0 else tm
        keep = min(ATT_REACH, Tx)
        lc = min(MLSTM_CHUNK, t_seq)
        qkva, qvm, kt, og, gates, k_last, v_last = _in_proj(x, w_all, b_all, wkt, bkt, tm=tp, keep=keep, lc=lc)
        seq = lambda a: a.reshape(seqs, t_seq, a.shape[-1])
        tq = ATTN_TILE if t_seq % ATTN_TILE == 0 else _token_tile(t_seq)
        past_feature_major = k_prev is not None
        if k_prev is None:
            k_prev = v_prev = seq(qkva)
            step = tq // ATT_REACH
            prev_map = (lambda b, i: (b, jnp.maximum(step * i - 1, 0), 1),
                        lambda b, i: (b, jnp.maximum(step * i - 1, 0), 2))
        att = _band_attn(seq(qkva), k_prev, v_prev, prev_map, bias_tab,
                         tq=tq, cps=cps, keys=keys, mask_first=mask_first,
                         past_feature_major=past_feature_major)
        gates = gates.reshape(t_seq // lc, seqs, GATE_ROWS, lc)
        kt = kt.reshape(Bx, W, seqs // Bx, t_seq).transpose(0, 2, 1, 3).reshape(seqs, W, t_seq)
        mb, c_new, n_new, m_new = _mlstm(seq(qvm), kt, gates, seq(og), norm_g, c0, n0, m0,
                                         lc=lc, bb=min(seqs, MLSTM_STREAMS),
                                         nc=min(MLSTM_CHUNKS_PER_STEP, t_seq // lc))
        tok = lambda a: a.reshape(Bx, Tx, a.shape[-1])
        tf = FFN_TILE if Tx % FFN_TILE == 0 else tm
        y = _merge_ffn(x, tok(att), tok(mb), *ffn_params, tm=tf, fc=FFN_CHUNK, parts=max(tf // FFN_GROUP, 1))
        return y, k_last, v_last, c_new, n_new, m_new

    zeros_c = jnp.zeros((B, H, d, d), F32)
    zeros_n = jnp.zeros((B, H, d), F32)
    zeros_m = jnp.zeros((B, H, 1), F32)
    yp, kp, vp, cp, np_, mp = layer(x_prompt, None, None, None, zeros_c, zeros_n, zeros_m,
                                    seqs=B, mask_first=True)

    ck = cache_k[0].transpose(0, 2, 3, 1).reshape(DB, ATT_WIDTH, ATT_REACH)
    cv = cache_v[0].transpose(0, 2, 3, 1).reshape(DB, ATT_WIDTH, ATT_REACH)
    prev_sample = (lambda b, i: (b, 0, 0), lambda b, i: (b, 0, 0))
    ys, ks, vs, cs, ns, ms = layer(
        x_sample.reshape(1, DB * T, D), ck, cv, prev_sample,
        state_C[0].astype(F32), state_n[0].astype(F32), state_m[0].astype(F32).reshape(DB, H, 1),
        seqs=DB, mask_first=False)

    sd = state_C.dtype
    def heads(a, nb, t):
        bx = a.shape[0]
        a = a.reshape(bx, ATT_HEADS, ATT_HEAD_DIM, nb // bx, t)
        return a.transpose(0, 3, 4, 1, 2).reshape(nb, t, ATT_HEADS, ATT_HEAD_DIM)[None]
    keep_p = min(ATT_REACH, S)
    return (yp, ys.reshape(DB, T, D),
            heads(kp, B, keep_p).astype(cache_k.dtype), heads(vp, B, keep_p).astype(cache_v.dtype),
            cp[None].astype(sd), np_[None].astype(sd), mp.reshape(1, B, H).astype(sd),
            heads(ks, DB, T).astype(cache_k.dtype), heads(vs, DB, T).astype(cache_v.dtype),
            cs[None].astype(sd), ns[None].astype(sd), ms.reshape(1, DB, H).astype(sd))
```

```python
import functools

import jax
import jax.numpy as jnp
from jax import lax
from jax.experimental import pallas as pl
from jax.experimental.pallas import tpu as pltpu

F32 = jnp.float32
BF16 = jnp.bfloat16

D_MODEL = 1024
CHUNK = 64
LEFT_CHUNKS = 8
ATT_REACH = LEFT_CHUNKS * CHUNK
ATT_WIDTH = 512
MLSTM_WIDTH = 512
ATT_HEADS = 8
ATT_HEAD_DIM = 64
MLSTM_HEADS = 4
MLSTM_HEAD_DIM = 128
MAX_REL = 128
D_FF = 4 * D_MODEL
DEPTH = 1
DEEPNORM_ALPHA = (2 * DEPTH) ** 0.25
LN_EPS = 1e-5
NEG_INF = -1e30
ATT_SCALE = ATT_HEAD_DIM ** -0.5
KB_SCALE = MLSTM_HEAD_DIM ** -0.5

OFF_QB = 3 * ATT_WIDTH
OFF_KB = OFF_QB + MLSTM_WIDTH
OFF_VB = OFF_KB + MLSTM_WIDTH
OFF_OB = OFF_VB + MLSTM_WIDTH
OFF_GATES = OFF_OB + MLSTM_WIDTH

LANES = 128
GATE_ROWS_PAD = 16
MLSTM_CHUNK = 256
MLSTM_CHUNKS_PER_STEP = 4
MLSTM_STREAMS = 2
GATE_ROWS = 24
IN_PROJ_TILE = 1024
ATTN_TILE = 1024
ATTN_CHUNKS_PER_WINDOW = 1
FFN_TILE = 1024
FFN_GROUP = 256
FFN_CHUNK = 2048
HEAD_PAIRS = ATT_HEADS // 2
VMEM_LIMIT = 56 * 1024 * 1024
NT_DIMS = (((1,), (1,)), ((), ()))


def _cparams(sem):
    return pltpu.CompilerParams(dimension_semantics=sem, vmem_limit_bytes=VMEM_LIMIT)


def _in_proj_kernel(x_ref, w_ref, b_ref, wkt_ref, bkt_ref,
                    qkva_ref, qvm_ref, kt_ref, og_ref, gates_ref, klast_ref, vlast_ref, *, tm, keep, lc):
    i = pl.program_id(1)
    last = pl.num_programs(1) - 1
    xb = x_ref[0].astype(BF16)

    def proj(c0, width):
        return (jnp.dot(xb, w_ref[:, c0:c0 + width], preferred_element_type=F32)
                + b_ref[:, c0:c0 + width])

    W = ATT_WIDTH
    H = MLSTM_HEADS
    ktg = lax.dot_general(wkt_ref[...], xb, NT_DIMS, preferred_element_type=F32) + bkt_ref[...]
    kt_ref[0] = (ktg[0:W] * KB_SCALE).astype(BF16)
    g = ktg[W:W + 2 * H]
    log_sig = jnp.minimum(g, 0.0) - jnp.log1p(jnp.exp(-jnp.abs(g)))
    head_row = lax.broadcasted_iota(jnp.int32, g.shape, 0) < H
    g8 = jnp.where(head_row, g, log_sig)
    pos = lax.broadcasted_iota(jnp.int32, g8.shape, 1) % lc
    csum = g8
    sh = 1
    while sh < lc:
        csum = csum + jnp.where(pos >= sh, pltpu.roll(csum, sh, 1), 0.0)
        sh *= 2
    b = pltpu.roll(csum, H, 0)
    a = g8 - b
    cm = a
    sh = 1
    while sh < lc:
        cm = jnp.maximum(cm, jnp.where(pos >= sh, pltpu.roll(cm, sh, 1), -jnp.inf))
        sh *= 2
    rows = [jnp.where(head_row, t, 0.0) for t in (b, a, cm)]
    for c in range(tm // lc):
        for j, t in enumerate(rows):
            gates_ref[c, 0, 8 * j:8 * j + 8, :] = t[:, c * lc:(c + 1) * lc]

    qvm_ref[0, :, 0:W] = proj(OFF_QB, W).astype(BF16)
    qvm_ref[0, :, W:2 * W] = proj(OFF_VB, W).astype(BF16)
    ob = proj(OFF_OB, W)
    og_ref[0] = 1.0 / (1.0 + jnp.exp(-ob))
    qa = proj(0, W)
    qkva_ref[0, :, 0:W] = (qa * ATT_SCALE).astype(BF16)
    ka = proj(W, W)
    qkva_ref[0, :, W:2 * W] = ka.astype(BF16)
    va = proj(2 * W, W)
    qkva_ref[0, :, 2 * W:3 * W] = va.astype(BF16)

    @pl.when(i == last)
    def _():
        klast_ref[0] = ka[tm - keep:, :].T
        vlast_ref[0] = va[tm - keep:, :].T


def _in_proj(x, w, b, wkt, bkt, *, tm, keep, lc):
    B, T, D = x.shape
    nt = T // tm
    W = ATT_WIDTH
    kern = functools.partial(_in_proj_kernel, tm=tm, keep=keep, lc=lc)
    return pl.pallas_call(
        kern,
        grid=(B, nt),
        in_specs=[
            pl.BlockSpec((1, tm, D), lambda b, i: (b, i, 0)),
            pl.BlockSpec(w.shape, lambda b, i: (0, 0), pipeline_mode=pl.Buffered(1)),
            pl.BlockSpec(b.shape, lambda b, i: (0, 0), pipeline_mode=pl.Buffered(1)),
            pl.BlockSpec(wkt.shape, lambda b, i: (0, 0), pipeline_mode=pl.Buffered(1)),
            pl.BlockSpec(bkt.shape, lambda b, i: (0, 0), pipeline_mode=pl.Buffered(1)),
        ],
        out_specs=[
            pl.BlockSpec((1, tm, 3 * W), lambda b, i: (b, i, 0)),
            pl.BlockSpec((1, tm, 2 * W), lambda b, i: (b, i, 0)),
            pl.BlockSpec((1, W, tm), lambda b, i: (b, 0, i)),
            pl.BlockSpec((1, tm, W), lambda b, i: (b, i, 0)),
            pl.BlockSpec((tm // lc, 1, GATE_ROWS, lc), lambda b, i: (i, b, 0, 0)),
            pl.BlockSpec((1, W, keep), lambda b, i: (b, 0, 0)),
            pl.BlockSpec((1, W, keep), lambda b, i: (b, 0, 0)),
        ],
        out_shape=[
            jax.ShapeDtypeStruct((B, T, 3 * W), BF16),
            jax.ShapeDtypeStruct((B, T, 2 * W), BF16),
            jax.ShapeDtypeStruct((B, W, T), BF16),
            jax.ShapeDtypeStruct((B, T, W), F32),
            jax.ShapeDtypeStruct((T // lc, B, GATE_ROWS, lc), F32),
            jax.ShapeDtypeStruct((B, W, keep), F32),
            jax.ShapeDtypeStruct((B, W, keep), F32),
        ],
        compiler_params=_cparams(("parallel", "arbitrary")),
        name="in_proj",
    )(x, w, b, wkt, bkt)


def _proj_weights_kernel(wt_ref, wg_ref, w_ref, wkt_ref):
    j = pl.program_id(0)
    W = wt_ref.shape[0]
    w_ref[...] = wt_ref[...].T.astype(BF16)

    @pl.when(j == OFF_KB // W)
    def _():
        wkt_ref[0:W, :] = wt_ref[...].astype(BF16)

    @pl.when(j == 0)
    def _():
        pad = jnp.zeros((GATE_ROWS_PAD - wg_ref.shape[0], wg_ref.shape[1]), F32)
        wkt_ref[W:W + GATE_ROWS_PAD, :] = jnp.concatenate([wg_ref[...], pad], axis=0).astype(BF16)


def _proj_weights(wt):
    n_in, D = wt.shape
    W = MLSTM_WIDTH
    n_gate = 2 * MLSTM_HEADS
    return pl.pallas_call(
        _proj_weights_kernel,
        grid=(OFF_GATES // W,),
        in_specs=[pl.BlockSpec((W, D), lambda j: (j, 0)),
                  pl.BlockSpec((n_gate, D), lambda j: (OFF_GATES // n_gate, 0))],
        out_specs=[pl.BlockSpec((D, W), lambda j: (0, j)),
                   pl.BlockSpec((W + GATE_ROWS_PAD, D), lambda j: (0, 0))],
        out_shape=[jax.ShapeDtypeStruct((D, OFF_GATES), BF16),
                   jax.ShapeDtypeStruct((W + GATE_ROWS_PAD, D), BF16)],
        compiler_params=_cparams(("arbitrary",)),
        name="proj_weights",
    )(wt, wt)


def _bias_table_kernel(g_ref, o_ref, *, rows, keys):
    L = g_ref.shape[-1]
    r = lax.broadcasted_iota(jnp.int32, (rows, keys), 0)
    c = lax.broadcasted_iota(jnp.int32, (rows, keys), 1)
    band_start = (r // CHUNK) * CHUNK
    in_band = (c >= band_start) & (c < band_start + ATT_REACH + CHUNK)
    for h in range(ATT_HEADS):
        g = jnp.broadcast_to(g_ref[h], (rows, L))
        t = pltpu.roll(g, L - rows, 1, stride=1, stride_axis=0)
        tab = jnp.where(in_band, t[:, :keys], NEG_INF)
        o_ref[h // 2, (h % 2) * rows:(h % 2 + 1) * rows, :] = tab


def _bias_table(rel_bias, *, cps, keys):
    rows = cps * CHUNK
    L = ((rows + keys + LANES - 1) // LANES) * LANES
    n_const = rows + ATT_REACH - MAX_REL
    const = jnp.broadcast_to(rel_bias[:, 2 * MAX_REL:], (ATT_HEADS, n_const))
    ramp = rel_bias[:, ::-1]
    tail = jnp.broadcast_to(rel_bias[:, :1], (ATT_HEADS, LANES))
    g = jnp.concatenate([const, ramp, tail], axis=1)[:, :L].reshape(ATT_HEADS, 1, L)
    kern = functools.partial(_bias_table_kernel, rows=rows, keys=keys)
    return pl.pallas_call(
        kern,
        out_shape=jax.ShapeDtypeStruct((HEAD_PAIRS, 2 * rows, keys), F32),
        name="bias_table",
    )(g)


def _attn_kernel(q_ref, kc_ref, vc_ref, kp_ref, vp_ref, bias_ref, o_ref, kk_ref, vv_ref,
                 *, tq, cps, keys, mask_first, past_feature_major):
    i = pl.program_id(1)
    rows = cps * CHUNK
    if past_feature_major:
        kk_ref[0:ATT_REACH, :] = kp_ref[0].T.astype(BF16)
        vv_ref[0:ATT_REACH, :] = vp_ref[0].T.astype(BF16)
    else:
        kk_ref[0:ATT_REACH, :] = kp_ref[0].astype(BF16)
        vv_ref[0:ATT_REACH, :] = vp_ref[0].astype(BF16)
    kk_ref[ATT_REACH:ATT_REACH + tq, :] = kc_ref[0]
    vv_ref[ATT_REACH:ATT_REACH + tq, :] = vc_ref[0]
    total = kk_ref.shape[0]
    if total > ATT_REACH + tq:
        kk_ref[ATT_REACH + tq:, :] = jnp.zeros((total - ATT_REACH - tq, ATT_WIDTH), BF16)
        vv_ref[ATT_REACH + tq:, :] = jnp.zeros((total - ATT_REACH - tq, ATT_WIDTH), BF16)

    lane = lax.broadcasted_iota(jnp.int32, (rows, LANES), 1)
    first_head = lane < ATT_HEAD_DIM

    def tile(no_past):
        for sub in range(tq // rows):
            off = sub * rows
            first_valid = max(ATT_REACH - off, 0) if no_past else 0
            k0 = first_valid // LANES * LANES
            ones_cols = jnp.ones((keys - k0, LANES), BF16)
            for pair in range(HEAD_PAIRS):
                ls = slice(pair * LANES, (pair + 1) * LANES)
                q2 = q_ref[0, off:off + rows, ls]
                zero = jnp.zeros_like(q2)
                qs = jnp.concatenate([jnp.where(first_head, q2, zero), jnp.where(first_head, zero, q2)], axis=0)
                s = lax.dot_general(qs, kk_ref[off + k0:off + keys, ls], NT_DIMS,
                                    preferred_element_type=F32)
                s = s + bias_ref[pair, :, k0:keys]
                if first_valid > k0:
                    col = lax.broadcasted_iota(jnp.int32, s.shape, 1)
                    s = jnp.where(col >= first_valid - k0, s, NEG_INF)
                m = jnp.max(s, axis=-1, keepdims=True)
                e = jnp.exp(s - m).astype(BF16)
                v_ext = jnp.concatenate([vv_ref[off + k0:off + keys, ls], ones_cols], axis=1)
                o2 = jnp.dot(e, v_ext, preferred_element_type=F32)
                o2 = o2[:, 0:LANES] * (1.0 / o2[:, LANES:2 * LANES])
                o = jnp.where(first_head, o2[:rows], o2[rows:])
                o_ref[0, off:off + rows, ls] = o.astype(BF16)

    if mask_first:
        pl.when(i == 0)(lambda: tile(True))
        pl.when(i > 0)(lambda: tile(False))
    else:
        tile(False)


def _band_attn(qkva, k_prev, v_prev, prev_map, bias_tab, *, tq, cps, keys, mask_first, past_feature_major):
    B, T, _ = qkva.shape
    nt = T // tq
    rows_total = max(ATT_REACH + tq, (tq // (cps * CHUNK) - 1) * cps * CHUNK + keys)
    kern = functools.partial(_attn_kernel, tq=tq, cps=cps, keys=keys, mask_first=mask_first,
                             past_feature_major=past_feature_major)
    W = ATT_WIDTH
    prev_block = (1, ATT_REACH, W)
    return pl.pallas_call(
        kern,
        grid=(B, nt),
        in_specs=[
            pl.BlockSpec((1, tq, W), lambda b, i: (b, i, 0)),
            pl.BlockSpec((1, tq, W), lambda b, i: (b, i, 1)),
            pl.BlockSpec((1, tq, W), lambda b, i: (b, i, 2)),
            pl.BlockSpec(prev_block, prev_map[0]),
            pl.BlockSpec(prev_block, prev_map[1]),
            pl.BlockSpec(bias_tab.shape, lambda b, i: (0, 0, 0)),
        ],
        out_specs=pl.BlockSpec((1, tq, W), lambda b, i: (b, i, 0)),
        out_shape=jax.ShapeDtypeStruct((B, T, W), BF16),
        scratch_shapes=[pltpu.VMEM((rows_total, W), BF16), pltpu.VMEM((rows_total, W), BF16)],
        compiler_params=_cparams(("parallel", "arbitrary")),
        name="band_attn",
    )(qkva, qkva, qkva, k_prev, v_prev, bias_tab)


def _lane_mean(x, mean_w):
    hi = x.astype(BF16)
    lo = (x - hi.astype(F32)).astype(BF16)
    return jnp.dot(jnp.concatenate([hi, lo], axis=1), mean_w, preferred_element_type=F32)


def _mlstm_kernel(q_ref, kt_ref, v_ref, gates_ref, og_ref, ng_ref, c0_ref, n0_ref, m0_ref,
                  mb_ref, c_out_ref, n_out_ref, m_out_ref, cn_s, m_s, *, lc, bb, nc):
    i = pl.program_id(1)
    last = pl.num_programs(1) - 1
    H, d = MLSTM_HEADS, MLSTM_HEAD_DIM

    @pl.when(i == 0)
    def _():
        m_s[...] = jnp.zeros(m_s.shape, F32)
        for bi in range(bb):
            for h in range(H):
                cn_s[bi * H + h, :, 0:d] = c0_ref[bi, h].T
                cn_s[bi * H + h, :, d:2 * d] = jnp.broadcast_to(n0_ref[bi, h:h + 1, :], (d, d)).T
            m_s[bi, 0:H, :] = jnp.broadcast_to(m0_ref[bi], (H, LANES))

    ti = lax.broadcasted_iota(jnp.int32, (lc, lc), 0)
    si = lax.broadcasted_iota(jnp.int32, (lc, lc), 1)
    causal = ti >= si
    ones_cols = jnp.ones((lc, d), BF16)
    mean_w = jnp.full((2 * d, d), 1.0 / d, BF16)

    a_rows, decays, col_sets = {}, {}, {}
    m_cur = [m_s[bi, :, 0:1] for bi in range(bb)]
    for c in range(nc):
        for bi in range(bb):
            b = gates_ref[c, bi, 0:8, :]
            a = gates_ref[c, bi, 8:16, :]
            cm = gates_ref[c, bi, 16:24, :]
            m_prev = m_cur[bi]
            inter = b + m_prev
            m_t = jnp.maximum(inter, b + cm)
            b_last = b[:, lc - 1:lc]
            m_new = jnp.maximum(b_last + m_prev, b_last + cm[:, lc - 1:lc])
            w_end = jnp.exp(b_last + a - m_new)
            per_frame = [b - m_t, jnp.exp(inter - m_t), jnp.exp(-m_t), w_end]
            stacked = jnp.concatenate(per_frame, axis=0)
            col_sets[c, bi] = stacked.T
            a_rows[c, bi] = a
            decays[c, bi] = jnp.exp(b_last + m_prev - m_new)
            m_cur[bi] = m_new
    for bi in range(bb):
        m_s[bi] = jnp.broadcast_to(m_cur[bi], (8, LANES))

    heads = [(bi, h) for bi in range(bb) for h in range(H)]
    inst = [(c, bi, h) for c in range(nc) for bi, h in heads]
    hsl = lambda h: slice(h * d, (h + 1) * d)
    rsl = lambda c: slice(c * lc, (c + 1) * lc)
    col = lambda c, bi, h, j: col_sets[c, bi][:, 8 * j + h:8 * j + h + 1]

    class _Loads:
        def __init__(self, load):
            self._load = load

        def __getitem__(self, key):
            return self._load(*key)

    qs = _Loads(lambda c, bi, h: q_ref[bi, rsl(c), hsl(h)])
    kts = _Loads(lambda c, bi, h: kt_ref[bi, hsl(h), rsl(c)])
    vs = _Loads(lambda c, bi, h: v_ref[bi, rsl(c), hsl(h)])

    qk = {k_: jnp.dot(qs[k_], kts[k_], preferred_element_type=F32) for k_ in inst}
    cn_cur = {(bi, h): cn_s[bi * H + h] for bi, h in heads}
    qcn = {}
    for c in range(nc):
        for bi, h in heads:
            qcn[c, bi, h] = jnp.dot(qs[c, bi, h], cn_cur[bi, h].astype(BF16), preferred_element_type=F32)
        for bi, h in heads:
            w_end_col = jnp.broadcast_to(col(c, bi, h, 3), (lc, d))
            vw = jnp.concatenate(
                [(vs[c, bi, h].astype(F32) * w_end_col).astype(BF16), w_end_col.astype(BF16)], axis=1)
            cn_cur[bi, h] = (decays[c, bi][h:h + 1, :] * cn_cur[bi, h]
                             + jnp.dot(kts[c, bi, h], vw, preferred_element_type=F32))
    for bi, h in heads:
        cn_s[bi * H + h] = cn_cur[bi, h]
    w = {(c, bi, h): jnp.exp(jnp.where(causal, col(c, bi, h, 0) + a_rows[c, bi][h:h + 1, :], -jnp.inf))
         * qk[c, bi, h] for c, bi, h in inst}
    wv = {k_: jnp.dot(w[k_].astype(BF16), jnp.concatenate([vs[k_], ones_cols], axis=1),
                      preferred_element_type=F32) for k_ in inst}
    hh = {}
    for c, bi, h in inst:
        tot = col(c, bi, h, 1) * qcn[c, bi, h] + wv[c, bi, h]
        den = tot[:, d:2 * d]
        hh[c, bi, h] = tot[:, 0:d] * (1.0 / jnp.maximum(jnp.abs(den), col(c, bi, h, 2)))
    xc = {k_: hh[k_] - _lane_mean(hh[k_], mean_w) for k_ in inst}
    var = {k_: _lane_mean(xc[k_] * xc[k_], mean_w) for k_ in inst}
    for c, bi, h in inst:
        hn = xc[c, bi, h] * lax.rsqrt(var[c, bi, h] + LN_EPS) * ng_ref[:, hsl(h)]
        mb_ref[bi, rsl(c), hsl(h)] = (og_ref[bi, rsl(c), hsl(h)] * hn).astype(BF16)

    @pl.when(i == last)
    def _():
        for bi in range(bb):
            for h in range(H):
                cn = cn_s[bi * H + h]
                c_out_ref[bi, h] = cn[:, 0:d].T
                n_out_ref[bi, h:h + 1, :] = cn[:, d:2 * d].T[0:1, :]
            m_out_ref[bi] = m_s[bi, 0:H, 0:1]


def _mlstm(qvm, kt, gates, og, norm_g, c0, n0, m0, *, lc, bb, nc):
    B, T, _ = qvm.shape
    ts = nc * lc
    nt = T // ts
    W = MLSTM_WIDTH
    H, d = MLSTM_HEADS, MLSTM_HEAD_DIM
    kern = functools.partial(_mlstm_kernel, lc=lc, bb=bb, nc=nc)
    return pl.pallas_call(
        kern,
        grid=(B // bb, nt),
        in_specs=[
            pl.BlockSpec((bb, ts, W), lambda b, i: (b, i, 0)),
            pl.BlockSpec((bb, W, ts), lambda b, i: (b, 0, i)),
            pl.BlockSpec((bb, ts, W), lambda b, i: (b, i, 1)),
            pl.BlockSpec((nc, bb, GATE_ROWS, lc), lambda b, i: (i, b, 0, 0)),
            pl.BlockSpec((bb, ts, W), lambda b, i: (b, i, 0)),
            pl.BlockSpec((1, W), lambda b, i: (0, 0)),
            pl.BlockSpec((bb, H, d, d), lambda b, i: (b, 0, 0, 0)),
            pl.BlockSpec((bb, H, d), lambda b, i: (b, 0, 0)),
            pl.BlockSpec((bb, H, 1), lambda b, i: (b, 0, 0)),
        ],
        out_specs=[
            pl.BlockSpec((bb, ts, W), lambda b, i: (b, i, 0)),
            pl.BlockSpec((bb, H, d, d), lambda b, i: (b, 0, 0, 0)),
            pl.BlockSpec((bb, H, d), lambda b, i: (b, 0, 0)),
            pl.BlockSpec((bb, H, 1), lambda b, i: (b, 0, 0)),
        ],
        out_shape=[
            jax.ShapeDtypeStruct((B, T, W), BF16),
            jax.ShapeDtypeStruct((B, H, d, d), F32),
            jax.ShapeDtypeStruct((B, H, d), F32),
            jax.ShapeDtypeStruct((B, H, 1), F32),
        ],
        scratch_shapes=[
            pltpu.VMEM((bb * H, d, 2 * d), F32),
            pltpu.VMEM((bb, 8, LANES), F32),
        ],
        compiler_params=_cparams(("parallel", "arbitrary")),
        name="mlstm",
    )(qvm, kt, qvm, gates, og, norm_g, c0, n0, m0)


def _layer_norm(z, g, b):
    mu = jnp.mean(z, axis=-1, keepdims=True)
    zc = z - mu
    var = jnp.mean(zc * zc, axis=-1, keepdims=True)
    return zc * lax.rsqrt(var + LN_EPS) * g + b


def _merge_ffn_kernel(x_ref, att_ref, mb_ref, wo_ref, g1_ref, b1_ref, w1_ref, bf1_ref, w2_ref, bf2_ref,
                      g2_ref, b2_ref, y_ref, *, fc, parts):
    W = ATT_WIDTH
    tm = x_ref.shape[1]
    rp = tm // parts
    groups = [slice(p * rp, (p + 1) * rp) for p in range(parts)]
    mixes = [jnp.dot(att_ref[0, rs, :], wo_ref[0:W, :], preferred_element_type=F32)
             + jnp.dot(mb_ref[0, rs, :], wo_ref[W:2 * W, :], preferred_element_type=F32) for rs in groups]
    for rs, mix in zip(groups, mixes):
        h = _layer_norm(DEEPNORM_ALPHA * x_ref[0, rs, :] + mix, g1_ref[...], b1_ref[...])
        hb = h.astype(BF16)
        f = jnp.zeros(h.shape, F32)
        for j in range(D_FF // fc):
            cs = slice(j * fc, (j + 1) * fc)
            a = jnp.dot(hb, w1_ref[:, cs], preferred_element_type=F32) + bf1_ref[:, cs]
            a = jnp.square(jnp.maximum(a, 0.0)).astype(BF16)
            f = f + jnp.dot(a, w2_ref[cs, :], preferred_element_type=F32)
        f = f + bf2_ref[...]
        y_ref[0, rs, :] = _layer_norm(DEEPNORM_ALPHA * h + f, g2_ref[...], b2_ref[...])


def _merge_ffn(x, att, mb, wo, g1, b1, w1, bf1, w2, bf2, g2, b2, *, tm, fc, parts):
    B, T, D = x.shape
    nt = T // tm
    W = ATT_WIDTH
    const = lambda shape: pl.BlockSpec(shape, lambda b, i: (0, 0), pipeline_mode=pl.Buffered(1))
    kern = functools.partial(_merge_ffn_kernel, fc=fc, parts=parts)
    return pl.pallas_call(
        kern,
        grid=(B, nt),
        in_specs=[
            pl.BlockSpec((1, tm, D), lambda b, i: (b, i, 0)),
            pl.BlockSpec((1, tm, W), lambda b, i: (b, i, 0)),
            pl.BlockSpec((1, tm, W), lambda b, i: (b, i, 0)),
            const((D, D)), const((1, D)), const((1, D)),
            const((D, D_FF)), const((1, D_FF)), const((D_FF, D)), const((1, D)),
            const((1, D)), const((1, D)),
        ],
        out_specs=pl.BlockSpec((1, tm, D), lambda b, i: (b, i, 0)),
        out_shape=jax.ShapeDtypeStruct((B, T, D), F32),
        compiler_params=_cparams(("parallel", "parallel")),
        name="merge_ffn",
    )(x, att, mb, wo, g1, b1, w1, bf1, w2, bf2, g2, b2)


def _token_tile(t):
    return 512 if t % 512 == 0 else t


def kernel(x_prompt, x_sample, cache_k, cache_v, state_C, state_n, state_m, w_in, b_in, rel_bias,
           mlstm_norm_g, w_out, ln1_g, ln1_b, w_ff1, b_ff1, w_ff2, b_ff2, ln2_g, ln2_b):
    assert w_in.shape[0] == DEPTH == 1
    B, S, D = x_prompt.shape
    DB, T, _ = x_sample.shape
    H, d = MLSTM_HEADS, MLSTM_HEAD_DIM
    W = MLSTM_WIDTH
    assert S % ATT_REACH == 0 and T == CHUNK and cache_k.shape[2] == ATT_REACH

    n_gate = 2 * H
    wi, bi_ = w_in[0], b_in[0]
    gate_pad = GATE_ROWS_PAD - n_gate
    w_all, wkt = _proj_weights(wi.T)
    b_all = bi_[None, :]
    bkt = jnp.pad(jnp.concatenate([bi_[OFF_KB:OFF_VB], bi_[OFF_GATES:]]), (0, gate_pad))[:, None]
    wo = w_out[0].astype(BF16)
    w1 = w_ff1[0].astype(BF16)
    w2 = w_ff2[0].astype(BF16)
    row = lambda p: p[0][None, :]
    ffn_params = (wo, row(ln1_g), row(ln1_b), w1, row(b_ff1), w2, row(b_ff2), row(ln2_g), row(ln2_b))
    norm_g = row(mlstm_norm_g)
    cps = ATTN_CHUNKS_PER_WINDOW
    keys = -(-(ATT_REACH + cps * CHUNK) // LANES) * LANES
    bias_tab = _bias_table(rel_bias[0], cps=cps, keys=keys)

    def layer(x, k_prev, v_prev, prev_map, c0, n0, m0, *, seqs, mask_first):
        Bx, Tx, _ = x.shape
        n_tok = Bx * Tx
        t_seq = n_tok // seqs
        tm = _token_tile(Tx)
        tp = IN_PROJ_TILE if Tx % IN_PROJ_TILE == 0 else tm
        keep = min(ATT_REACH, Tx)
        lc = min(MLSTM_CHUNK, t_seq)
        qkva, qvm, kt, og, gates, k_last, v_last = _in_proj(x, w_all, b_all, wkt, bkt, tm=tp, keep=keep, lc=lc)
        seq = lambda a: a.reshape(seqs, t_seq, a.shape[-1])
        tq = ATTN_TILE if t_seq % ATTN_TILE == 0 else _token_tile(t_seq)
        past_feature_major = k_prev is not None
        if k_prev is None:
            k_prev = v_prev = seq(qkva)
            step = tq // ATT_REACH
            prev_map = (lambda b, i: (b, jnp.maximum(step * i - 1, 0), 1),
                        lambda b, i: (b, jnp.maximum(step * i - 1, 0), 2))
        att = _band_attn(seq(qkva), k_prev, v_prev, prev_map, bias_tab,
                         tq=tq, cps=cps, keys=keys, mask_first=mask_first,
                         past_feature_major=past_feature_major)
        gates = gates.reshape(t_seq // lc, seqs, GATE_ROWS, lc)
        kt = kt.reshape(Bx, W, seqs // Bx, t_seq).transpose(0, 2, 1, 3).reshape(seqs, W, t_seq)
        mb, c_new, n_new, m_new = _mlstm(seq(qvm), kt, gates, seq(og), norm_g, c0, n0, m0,
                                         lc=lc, bb=min(seqs, MLSTM_STREAMS),
                                         nc=min(MLSTM_CHUNKS_PER_STEP, t_seq // lc))
        tok = lambda a: a.reshape(Bx, Tx, a.shape[-1])
        tf = FFN_TILE if Tx % FFN_TILE == 0 else tm
        y = _merge_ffn(x, tok(att), tok(mb), *ffn_params, tm=tf, fc=FFN_CHUNK, parts=max(tf // FFN_GROUP, 1))
        return y, k_last, v_last, c_new, n_new, m_new

    zeros_c = jnp.zeros((B, H, d, d), F32)
    zeros_n = jnp.zeros((B, H, d), F32)
    zeros_m = jnp.zeros((B, H, 1), F32)
    yp, kp, vp, cp, np_, mp = layer(x_prompt, None, None, None, zeros_c, zeros_n, zeros_m,
                                    seqs=B, mask_first=True)

    ck = cache_k[0].transpose(0, 2, 3, 1).reshape(DB, ATT_WIDTH, ATT_REACH)
    cv = cache_v[0].transpose(0, 2, 3, 1).reshape(DB, ATT_WIDTH, ATT_REACH)
    prev_sample = (lambda b, i: (b, 0, 0), lambda b, i: (b, 0, 0))
    ys, ks, vs, cs, ns, ms = layer(
        x_sample.reshape(1, DB * T, D), ck, cv, prev_sample,
        state_C[0].astype(F32), state_n[0].astype(F32), state_m[0].astype(F32).reshape(DB, H, 1),
        seqs=DB, mask_first=False)

    sd = state_C.dtype
    def heads(a, nb, t):
        bx = a.shape[0]
        a = a.reshape(bx, ATT_HEADS, ATT_HEAD_DIM, nb // bx, t)
        return a.transpose(0, 3, 4, 1, 2).reshape(nb, t, ATT_HEADS, ATT_HEAD_DIM)[None]
    keep_p = min(ATT_REACH, S)
    return (yp, ys.reshape(DB, T, D),
            heads(kp, B, keep_p).astype(cache_k.dtype), heads(vp, B, keep_p).astype(cache_v.dtype),
            cp[None].astype(sd), np_[None].astype(sd), mp.reshape(1, B, H).astype(sd),
            heads(ks, DB, T).astype(cache_k.dtype), heads(vs, DB, T).astype(cache_v.dtype),
            cs[None].astype(sd), ns[None].astype(sd), ms.reshape(1, DB, H).astype(sd))
```
